```python
import jax, jax.numpy as jnp
from jax import lax
import numpy as np

D_MODEL = 1024
BATCH = 4
SEQ = 4096
DEPTH = 1

ATT_HEADS = 8
HEAD_DIM = 64
ATT_WIDTH = ATT_HEADS * HEAD_DIM
CONV_GROUPS = 8
CONV_WIDTH = D_MODEL - ATT_WIDTH
CONV_K = 3
IDX_HEADS = 8
IDX_DIM = 64
INDEX_TOPK_MAX = 256
Q_BLOCK = 128
ROPE_THETA = 10000.0
D_FF = -(-8 * D_MODEL // (3 * 256)) * 256
DEEPNORM_ALPHA = (2 * DEPTH) ** 0.25
DEEPNORM_BETA = (8 * DEPTH) ** -0.25
NORM_EPS = 1e-5

IN_SPLIT_SIZES = (ATT_WIDTH, ATT_WIDTH, ATT_WIDTH, IDX_HEADS * IDX_DIM, IDX_DIM, IDX_HEADS,
                  CONV_WIDTH, CONV_WIDTH, CONV_WIDTH)
IN_COLS = sum(IN_SPLIT_SIZES)

kernel_name = "hymba_conv_dsa_deepnorm_layer"


def _split_cols(a, sizes):
    offsets = list(np.cumsum(sizes)[:-1])
    return jnp.split(a, offsets, axis=-1)


def _layer_norm(x, g, b):
    x32 = x.astype(jnp.float32)
    mu = jnp.mean(x32, axis=-1, keepdims=True)
    var = jnp.mean(jnp.square(x32 - mu), axis=-1, keepdims=True)
    y = (x32 - mu) * lax.rsqrt(var + NORM_EPS) * g.astype(jnp.float32) + b.astype(jnp.float32)
    return y.astype(x.dtype)


def _rms_norm(x, g):
    x32 = x.astype(jnp.float32)
    y = x32 * lax.rsqrt(jnp.mean(jnp.square(x32), axis=-1, keepdims=True) + NORM_EPS)
    return (y * g.astype(jnp.float32)).astype(x.dtype)


def _rope_tables(seq, dim):
    inv_freq = 1.0 / (ROPE_THETA ** (jnp.arange(0, dim, 2, dtype=jnp.float32) / dim))
    ang = jnp.arange(seq, dtype=jnp.float32)[:, None] * inv_freq[None, :]
    return jnp.cos(ang), jnp.sin(ang)


def _apply_rope(x, cos, sin):
    x32 = x.astype(jnp.float32)
    x1, x2 = jnp.split(x32, 2, axis=-1)
    c = cos[None, :, None, :]
    s = sin[None, :, None, :]
    return jnp.concatenate([x1 * c - x2 * s, x2 * c + x1 * s], axis=-1).astype(x.dtype)


def _short_conv(h, gate_b, gate_c, conv_w):
    u = gate_c * h
    y = lax.conv_general_dilated(
        u, conv_w[:, None, :].astype(u.dtype), window_strides=(1,), padding=[(CONV_K - 1, 0)],
        dimension_numbers=('NWC', 'WIO', 'NWC'), feature_group_count=CONV_WIDTH)
    return gate_b * y


def _dsa_attention(q, k, v, qi, ki, wi):
    bsz, seq = q.shape[0], q.shape[1]
    topk = min(INDEX_TOPK_MAX, seq // 4)
    nblk = seq // Q_BLOCK
    key_pos = jnp.arange(seq)
    ki32 = ki.astype(jnp.float32)

    def to_blocks(a):
        return jnp.moveaxis(a.reshape(bsz, nblk, Q_BLOCK, *a.shape[2:]), 1, 0)

    def block(args):
        q_b, qi_b, wi_b, start = args
        qpos = start + jnp.arange(Q_BLOCK)
        dots = jnp.einsum('bqhd,bsd->bqhs', qi_b.astype(jnp.float32), ki32) * (IDX_DIM ** -0.5)
        w_b = wi_b.astype(jnp.float32) * (IDX_HEADS ** -0.5)
        score = jnp.einsum('bqh,bqhs->bqs', w_b, jax.nn.relu(dots))
        admissible = key_pos[None, :] <= qpos[:, None]
        score = jnp.where(admissible[None], score, -jnp.inf)
        _, idx = lax.top_k(score, topk)
        valid = idx <= qpos[None, :, None]
        k_sel = jax.vmap(lambda kb, ib: kb[ib])(k, idx)
        v_sel = jax.vmap(lambda vb, ib: vb[ib])(v, idx)
        logits = jnp.einsum('bqhd,bqkhd->bqhk', q_b.astype(jnp.float32),
                            k_sel.astype(jnp.float32)) * (HEAD_DIM ** -0.5)
        logits = jnp.where(valid[:, :, None, :], logits, -jnp.inf)
        p = jax.nn.softmax(logits, axis=-1)
        out = jnp.einsum('bqhk,bqkhd->bqhd', p, v_sel.astype(jnp.float32))
        return out.astype(q_b.dtype)

    starts = jnp.arange(nblk) * Q_BLOCK
    out = lax.map(block, (to_blocks(q), to_blocks(qi), to_blocks(wi), starts))
    return jnp.moveaxis(out, 0, 1).reshape(bsz, seq, ATT_WIDTH)


def setup_inputs(seed: int = 0) -> dict:
    key = jax.random.key(seed)
    ks = jax.random.split(key, 12)
    f32 = jnp.float32
    x = jax.random.normal(ks[0], (BATCH, SEQ, D_MODEL), f32)
    w_in = jax.random.normal(ks[1], (DEPTH, D_MODEL, IN_COLS), f32) * D_MODEL ** -0.5
    conv_w = jax.random.normal(ks[2], (DEPTH, CONV_K, CONV_WIDTH), f32) * CONV_K ** -0.5
    mix_norm_g = 1.0 + 0.02 * jax.random.normal(ks[3], (DEPTH, D_MODEL), f32)
    w_out = jax.random.normal(ks[4], (DEPTH, D_MODEL, D_MODEL), f32) * (D_MODEL ** -0.5 * DEEPNORM_BETA)
    ln1_g = 1.0 + 0.02 * jax.random.normal(ks[5], (DEPTH, D_MODEL), f32)
    ln1_b = 0.02 * jax.random.normal(ks[6], (DEPTH, D_MODEL), f32)
    w_gate_up = jax.random.normal(ks[7], (DEPTH, D_MODEL, 2 * D_FF), f32) * D_MODEL ** -0.5
    w_down = jax.random.normal(ks[8], (DEPTH, D_FF, D_MODEL), f32) * (D_FF ** -0.5 * DEEPNORM_BETA)
    ln2_g = 1.0 + 0.02 * jax.random.normal(ks[9], (DEPTH, D_MODEL), f32)
    ln2_b = 0.02 * jax.random.normal(ks[10], (DEPTH, D_MODEL), f32)
    return {"x": x, "w_in": w_in, "conv_w": conv_w, "mix_norm_g": mix_norm_g, "w_out": w_out,
            "ln1_g": ln1_g, "ln1_b": ln1_b, "w_gate_up": w_gate_up, "w_down": w_down,
            "ln2_g": ln2_g, "ln2_b": ln2_b}


def reference(x, w_in, conv_w, mix_norm_g, w_out, ln1_g, ln1_b, w_gate_up, w_down, ln2_g, ln2_b):
    bsz, seq, _ = x.shape
    cos, sin = _rope_tables(seq, HEAD_DIM)
    for layer in range(DEPTH):
        proj = x @ w_in[layer]
        q, k, v, qi, ki, wi, gate_b, gate_c, h = _split_cols(proj, IN_SPLIT_SIZES)
        q = _apply_rope(q.reshape(bsz, seq, ATT_HEADS, HEAD_DIM), cos, sin)
        k = _apply_rope(k.reshape(bsz, seq, ATT_HEADS, HEAD_DIM), cos, sin)
        v = v.reshape(bsz, seq, ATT_HEADS, HEAD_DIM)
        qi = _apply_rope(qi.reshape(bsz, seq, IDX_HEADS, IDX_DIM), cos, sin)
        ki = _apply_rope(ki[:, :, None, :], cos, sin)[:, :, 0, :]
        attn_out = _dsa_attention(q, k, v, qi, ki, wi)
        conv_out = _short_conv(h, gate_b, gate_c, conv_w[layer])
        g = mix_norm_g[layer]
        mixed = jnp.concatenate([_rms_norm(attn_out, g[:ATT_WIDTH]),
                                 _rms_norm(conv_out, g[ATT_WIDTH:])], axis=-1)
        x = _layer_norm(DEEPNORM_ALPHA * x + mixed @ w_out[layer], ln1_g[layer], ln1_b[layer])
        gate, up = jnp.split(x @ w_gate_up[layer], 2, axis=-1)
        ffn = (jax.nn.silu(gate) * up) @ w_down[layer]
        x = _layer_norm(DEEPNORM_ALPHA * x + ffn, ln2_g[layer], ln2_b[layer])
    return x
```

```python
import functools

import jax
import jax.numpy as jnp
from jax import lax
from jax.experimental import pallas as pl
from jax.experimental.pallas import tpu as pltpu

D_MODEL = 1024
ATT_HEADS = 8
HEAD_DIM = 64
ATT_WIDTH = ATT_HEADS * HEAD_DIM
CONV_WIDTH = D_MODEL - ATT_WIDTH
CONV_K = 3
IDX_HEADS = 8
IDX_DIM = 64
INDEX_TOPK_MAX = 256
ROPE_THETA = 10000.0
NORM_EPS = 1e-5
IN_SPLIT_SIZES = (ATT_WIDTH, ATT_WIDTH, ATT_WIDTH, IDX_HEADS * IDX_DIM, IDX_DIM, IDX_HEADS,
                  CONV_WIDTH, CONV_WIDTH, CONV_WIDTH)

LANES = 128
SUBLANES = 8
SEQ_TILE = 256
ROW_TILE = 256
VMEM_LIMIT = 56 * 1024 * 1024

F32 = jnp.float32
BF16 = jnp.bfloat16
INT_MIN = -2 ** 31
KEY_NEG_INF = (0xFF800000 ^ 0x7FFFFFFF) - 2 ** 32
KEY_POS_INF = 0x7F800000


def _proj_kernel(x_ref, wn_ref, wt_ref, cosn_ref, sinn_ref, cost_ref, sint_ref, convw_ref, gconv_ref,
                 kn_ref, kin_ref, qt_ref, qit_ref, vt_ref, wt_out_ref, cm_ref, u_ref, *, w_scale):
    t = pl.program_id(1)
    tm = x_ref.shape[1]
    xb = x_ref[0].astype(BF16)
    nat = jnp.dot(xb, wn_ref[...], preferred_element_type=F32)
    tr = lax.dot_general(wt_ref[...], xb, (((1,), (1,)), ((), ())),
                         preferred_element_type=F32)

    cos_n = cosn_ref[...]
    sin_n = sinn_ref[...]
    cos4 = jnp.concatenate([cos_n] * (ATT_WIDTH // LANES), axis=1)
    sin4 = jnp.concatenate([sin_n] * (ATT_WIDTH // LANES), axis=1)
    k_rope = nat[:, 0:ATT_WIDTH] * cos4 + nat[:, ATT_WIDTH:2 * ATT_WIDTH] * sin4
    kn_ref[0] = k_rope.astype(BF16)
    o_ki = 2 * ATT_WIDTH + 3 * CONV_WIDTH
    ki_rope = nat[:, o_ki:o_ki + LANES] * cos_n + nat[:, o_ki + LANES:o_ki + 2 * LANES] * sin_n
    kin_ref[0] = ki_rope[:, :IDX_DIM].astype(BF16)

    cos_t = cost_ref[...]
    sin_t = sint_ref[...]
    half = HEAD_DIM // 2
    for h in range(ATT_HEADS):
        r0 = h * HEAD_DIM
        x1 = tr[r0:r0 + half]
        x2 = tr[r0 + half:r0 + HEAD_DIM]
        qt_ref[0, 0, r0:r0 + half, :] = ((x1 * cos_t - x2 * sin_t) * (HEAD_DIM ** -0.5)).astype(BF16)
        qt_ref[0, 0, r0 + half:r0 + HEAD_DIM, :] = ((x2 * cos_t + x1 * sin_t) * (HEAD_DIM ** -0.5)).astype(BF16)
    for h in range(IDX_HEADS):
        r0 = ATT_WIDTH + h * IDX_DIM
        x1 = tr[r0:r0 + half]
        x2 = tr[r0 + half:r0 + IDX_DIM]
        qit_ref[0, 0, h * IDX_DIM:h * IDX_DIM + half, :] = (x1 * cos_t - x2 * sin_t).astype(BF16)
        qit_ref[0, 0, h * IDX_DIM + half:(h + 1) * IDX_DIM, :] = (x2 * cos_t + x1 * sin_t).astype(BF16)
    o_v = ATT_WIDTH + IDX_HEADS * IDX_DIM
    vt_ref[0, 0] = tr[o_v:o_v + ATT_WIDTH].astype(BF16)
    wt_out_ref[0, 0] = tr[o_v + ATT_WIDTH:o_v + ATT_WIDTH + IDX_HEADS] * w_scale

    o_c = 2 * ATT_WIDTH
    gate_b = nat[:, o_c:o_c + CONV_WIDTH]
    gate_c = nat[:, o_c + CONV_WIDTH:o_c + 2 * CONV_WIDTH]
    hh = nat[:, o_c + 2 * CONV_WIDTH:o_c + 3 * CONV_WIDTH]
    u = gate_c * hh

    @pl.when(t == 0)
    def _():
        u_ref[0:SUBLANES, :] = jnp.zeros((SUBLANES, CONV_WIDTH), F32)

    u_ref[SUBLANES:SUBLANES + tm, :] = u
    um1 = u_ref[SUBLANES - 1:SUBLANES - 1 + tm, :]
    um2 = u_ref[SUBLANES - 2:SUBLANES - 2 + tm, :]
    cw = convw_ref[...]
    y = cw[0:1, :] * um2 + cw[1:2, :] * um1 + cw[2:3, :] * u
    u_ref[0:SUBLANES, :] = u[tm - SUBLANES:tm, :]
    conv_out = gate_b * y
    ms = jnp.mean(conv_out * conv_out, axis=-1, keepdims=True)
    cm_ref[0] = (conv_out * lax.rsqrt(ms + NORM_EPS) * gconv_ref[...]).astype(BF16)


def _proj_call(x, wn, wt, cos_n, sin_n, cos_t, sin_t, conv_w, g_conv):
    bsz, seq, _ = x.shape
    tm = SEQ_TILE
    nt = seq // tm
    n_nat = wn.shape[1]
    n_tr = wt.shape[0]
    w_scale = (IDX_DIM ** -0.5) * (IDX_HEADS ** -0.5)
    const2 = lambda b, t: (0, 0)
    out_shapes = (
        jax.ShapeDtypeStruct((bsz, seq, ATT_WIDTH), BF16),
        jax.ShapeDtypeStruct((bsz, seq, IDX_DIM), BF16),
        jax.ShapeDtypeStruct((bsz, nt, ATT_WIDTH, tm), BF16),
        jax.ShapeDtypeStruct((bsz, nt, IDX_HEADS * IDX_DIM, tm), BF16),
        jax.ShapeDtypeStruct((bsz, nt, ATT_WIDTH, tm), BF16),
        jax.ShapeDtypeStruct((bsz, nt, IDX_HEADS, tm), F32),
        jax.ShapeDtypeStruct((bsz, seq, CONV_WIDTH), BF16),
    )
    tile4 = lambda b, t: (b, t, 0, 0)
    return pl.pallas_call(
        functools.partial(_proj_kernel, w_scale=w_scale),
        grid=(bsz, nt),
        in_specs=[
            pl.BlockSpec((1, tm, D_MODEL), lambda b, t: (b, t, 0)),
            pl.BlockSpec((D_MODEL, n_nat), const2),
            pl.BlockSpec((n_tr, D_MODEL), const2),
            pl.BlockSpec((tm, LANES), lambda b, t: (t, 0)),
            pl.BlockSpec((tm, LANES), lambda b, t: (t, 0)),
            pl.BlockSpec((HEAD_DIM // 2, tm), lambda b, t: (0, t)),
            pl.BlockSpec((HEAD_DIM // 2, tm), lambda b, t: (0, t)),
            pl.BlockSpec((CONV_K, CONV_WIDTH), const2),
            pl.BlockSpec((1, CONV_WIDTH), const2),
        ],
        out_specs=(
            pl.BlockSpec((1, tm, ATT_WIDTH), lambda b, t: (b, t, 0)),
            pl.BlockSpec((1, tm, IDX_DIM), lambda b, t: (b, t, 0)),
            pl.BlockSpec((1, 1, ATT_WIDTH, tm), tile4),
            pl.BlockSpec((1, 1, IDX_HEADS * IDX_DIM, tm), tile4),
            pl.BlockSpec((1, 1, ATT_WIDTH, tm), tile4),
            pl.BlockSpec((1, 1, IDX_HEADS, tm), tile4),
            pl.BlockSpec((1, tm, CONV_WIDTH), lambda b, t: (b, t, 0)),
        ),
        out_shape=out_shapes,
        scratch_shapes=[pltpu.VMEM((tm + SUBLANES, CONV_WIDTH), F32)],
        compiler_params=pltpu.CompilerParams(
            dimension_semantics=("arbitrary", "arbitrary"), vmem_limit_bytes=VMEM_LIMIT),
        name="proj_rope_conv",
    )(x, wn, wt, cos_n, sin_n, cos_t, sin_t, conv_w, g_conv)


def _key_to_float(key):
    key = jnp.clip(key, KEY_NEG_INF, KEY_POS_INF)
    bits = jnp.where(key >= 0, key, key ^ 0x7FFFFFFF)
    return lax.bitcast_convert_type(bits, F32)


def _fold_rows(a):
    return a.reshape(a.shape[0] // SUBLANES, SUBLANES, a.shape[1])


def _dsa_kernel(kn_ref, kin_ref, vt_ref, qt_ref, qit_ref, w_ref, g_ref, out_ref,
                score_ref, logit_ref, at_ref, *, topk, seq):
    i = pl.program_id(1)
    tq = qt_ref.shape[3]
    tk = tq
    nchunks = i + 1
    neg_inf = jnp.float32(-jnp.inf)

    row_iota = lax.broadcasted_iota(jnp.int32, (tk, tq), 0)
    qpos = i * tq + lax.broadcasted_iota(jnp.int32, (tk, tq), 1)

    def chunk_off(c):
        return pl.multiple_of(c * tk, tk)

    qi_all = qit_ref[0, 0]
    w_rows = w_ref[0, 0]

    def score_body(c, carry):
        off = chunk_off(c)
        kic = kin_ref[0, pl.ds(off, tk), :]
        sc = jnp.zeros((tk, tq), F32)
        for h in range(IDX_HEADS):
            d = jnp.dot(kic, qi_all[h * IDX_DIM:(h + 1) * IDX_DIM, :], preferred_element_type=F32)
            sc = sc + jnp.maximum(d, 0.0) * w_rows[h:h + 1, :]
        sc = jnp.where(off + row_iota <= qpos, sc, neg_inf)
        score_ref[pl.ds(off, tk), :] = sc
        return carry

    lax.fori_loop(0, nchunks, score_body, 0)

    def count_rows(pred_fn):
        def body(c, acc):
            off = chunk_off(c)
            m = pred_fn(score_ref[pl.ds(off, tk), :], off).astype(jnp.int32)
            return acc + jnp.sum(_fold_rows(m), axis=0)
        acc = lax.fori_loop(0, nchunks, body, jnp.zeros((SUBLANES, tq), jnp.int32))
        return jnp.sum(acc, axis=0, keepdims=True)

    def search_body(p, u):
        trial = u | jnp.left_shift(jnp.int32(1), 31 - p)
        cand = _key_to_float(trial ^ INT_MIN)
        cnt = count_rows(lambda sc, off: sc >= cand)
        return jnp.where(cnt >= topk, trial, u)

    u = lax.fori_loop(0, 32, search_body, jnp.zeros((1, tq), jnp.int32))
    thr = _key_to_float(u ^ INT_MIN)

    n_ge = count_rows(lambda sc, off: sc >= thr)
    n_gt = count_rows(lambda sc, off: sc > thr)
    want = topk - n_gt
    has_excess = jnp.max(n_ge - n_gt - want) > 0

    def tie_search():
        def body(p, jp):
            trial = jp + jnp.left_shift(jnp.int32(1), (seq - 1).bit_length() - 1 - p)
            cnt = count_rows(lambda sc, off: (sc == thr) & (off + row_iota <= trial))
            return jnp.where(cnt < want, trial, jp)
        jp = lax.fori_loop(0, (seq - 1).bit_length(), body, jnp.full((1, tq), -1, jnp.int32))
        return jp + 1

    jlim = lax.cond(has_excess, tie_search, lambda: jnp.full((1, tq), seq, jnp.int32))

    def bias_body(c, carry):
        off = chunk_off(c)
        sc = score_ref[pl.ds(off, tk), :]
        kpos = off + row_iota
        sel = ((sc > thr) | ((sc == thr) & (kpos <= jlim))) & (kpos <= qpos)
        score_ref[pl.ds(off, tk), :] = jnp.where(sel, 0.0, neg_inf)
        return carry

    lax.fori_loop(0, nchunks, bias_body, 0)

    q_all = qt_ref[0, 0]
    zeros_q = jnp.zeros((HEAD_DIM, tq), BF16)
    for j in range(ATT_HEADS // 2):
        r0 = 2 * j * HEAD_DIM
        wbd = jnp.concatenate([
            jnp.concatenate([q_all[r0:r0 + HEAD_DIM], zeros_q], axis=1),
            jnp.concatenate([zeros_q, q_all[r0 + HEAD_DIM:r0 + 2 * HEAD_DIM]], axis=1)], axis=0)

        def logit_body(c, mx, r0=r0, wbd=wbd):
            off = chunk_off(c)
            kc = kn_ref[0, pl.ds(off, tk), r0:r0 + 2 * HEAD_DIM]
            bias = score_ref[pl.ds(off, tk), :]
            lg = jnp.dot(kc, wbd, preferred_element_type=F32) + jnp.concatenate([bias, bias], axis=1)
            logit_ref[pl.ds(off, tk), :] = lg
            return jnp.maximum(mx, jnp.max(_fold_rows(lg), axis=0))

        mx = lax.fori_loop(0, nchunks, logit_body, jnp.full((SUBLANES, 2 * tq), neg_inf, F32))
        m = jnp.max(mx, axis=0, keepdims=True)

        def pv_body(c, carry, r0=r0, m=m):
            oa, ob, l = carry
            off = chunk_off(c)
            p = jnp.exp(logit_ref[pl.ds(off, tk), :] - m)
            l = l + jnp.sum(_fold_rows(p), axis=0)
            pb = p.astype(BF16)
            va = vt_ref[0, c, r0:r0 + HEAD_DIM, :]
            vb = vt_ref[0, c, r0 + HEAD_DIM:r0 + 2 * HEAD_DIM, :]
            oa = oa + jnp.dot(va, pb[:, :tq], preferred_element_type=F32)
            ob = ob + jnp.dot(vb, pb[:, tq:], preferred_element_type=F32)
            return oa, ob, l

        zero_o = jnp.zeros((HEAD_DIM, tq), F32)
        oa, ob, l = lax.fori_loop(0, nchunks, pv_body,
                                  (zero_o, zero_o, jnp.zeros((SUBLANES, 2 * tq), F32)))
        lsum = jnp.sum(l, axis=0, keepdims=True)
        at_ref[r0:r0 + HEAD_DIM, :] = oa / lsum[:, :tq]
        at_ref[r0 + HEAD_DIM:r0 + 2 * HEAD_DIM, :] = ob / lsum[:, tq:]

    a_t = at_ref[...]
    ms = jnp.mean(a_t * a_t, axis=0, keepdims=True)
    y = (a_t * lax.rsqrt(ms + NORM_EPS)).T
    out_ref[0] = (y * g_ref[...]).astype(BF16)


def _dsa_call(kn, kin, vt, qt, qit, w_t, g_attn, topk):
    bsz, seq, _ = kn.shape
    nt, tq = qt.shape[1], qt.shape[3]
    tile4 = lambda b, i: (b, i, 0, 0)
    return pl.pallas_call(
        functools.partial(_dsa_kernel, topk=topk, seq=seq),
        grid=(bsz, nt),
        in_specs=[
            pl.BlockSpec((1, seq, ATT_WIDTH), lambda b, i: (b, 0, 0)),
            pl.BlockSpec((1, seq, IDX_DIM), lambda b, i: (b, 0, 0)),
            pl.BlockSpec((1, nt, ATT_WIDTH, tq), lambda b, i: (b, 0, 0, 0)),
            pl.BlockSpec((1, 1, ATT_WIDTH, tq), tile4),
            pl.BlockSpec((1, 1, IDX_HEADS * IDX_DIM, tq), tile4),
            pl.BlockSpec((1, 1, IDX_HEADS, tq), tile4),
            pl.BlockSpec((1, ATT_WIDTH), lambda b, i: (0, 0)),
        ],
        out_specs=pl.BlockSpec((1, tq, ATT_WIDTH), lambda b, i: (b, i, 0)),
        out_shape=jax.ShapeDtypeStruct((bsz, seq, ATT_WIDTH), BF16),
        scratch_shapes=[
            pltpu.VMEM((seq, tq), F32),
            pltpu.VMEM((seq, 2 * tq), F32),
            pltpu.VMEM((ATT_WIDTH, tq), F32),
        ],
        compiler_params=pltpu.CompilerParams(
            dimension_semantics=("arbitrary", "arbitrary"), vmem_limit_bytes=VMEM_LIMIT),
        name="dsa_attention",
    )(kn, kin, vt, qt, qit, w_t, g_attn)


def _layer_norm(y, g, b):
    mu = jnp.mean(y, axis=-1, keepdims=True)
    d = y - mu
    var = jnp.mean(d * d, axis=-1, keepdims=True)
    return d * lax.rsqrt(var + NORM_EPS) * g + b


def _ffn_kernel(x_ref, am_ref, cm_ref, wout_ref, g1_ref, b1_ref, wgu_ref, wd_ref, g2_ref, b2_ref,
                o_ref, *, alpha, d_ff):
    x = x_ref[...]
    mix = (jnp.dot(am_ref[...], wout_ref[0:ATT_WIDTH, :], preferred_element_type=F32)
           + jnp.dot(cm_ref[...], wout_ref[ATT_WIDTH:D_MODEL, :], preferred_element_type=F32))
    x1 = _layer_norm(alpha * x + mix, g1_ref[...], b1_ref[...])
    gu = jnp.dot(x1.astype(BF16), wgu_ref[...], preferred_element_type=F32)
    gate = gu[:, :d_ff]
    up = gu[:, d_ff:]
    hidden = (gate * jax.nn.sigmoid(gate) * up).astype(BF16)
    ffn = jnp.dot(hidden, wd_ref[...], preferred_element_type=F32)
    o_ref[...] = _layer_norm(alpha * x1 + ffn, g2_ref[...], b2_ref[...])


def _ffn_call(x2d, am, cm, wout, g1, b1, wgu, wd, g2, b2, alpha):
    rows = x2d.shape[0]
    tm = ROW_TILE
    d_ff = wd.shape[0]
    row = lambda r: (r, 0)
    const = lambda r: (0, 0)
    resident = functools.partial(pl.BlockSpec, index_map=const, pipeline_mode=pl.Buffered(1))
    return pl.pallas_call(
        functools.partial(_ffn_kernel, alpha=alpha, d_ff=d_ff),
        grid=(rows // tm,),
        in_specs=[
            pl.BlockSpec((tm, D_MODEL), row),
            pl.BlockSpec((tm, ATT_WIDTH), row),
            pl.BlockSpec((tm, CONV_WIDTH), row),
            resident((D_MODEL, D_MODEL)),
            resident((1, D_MODEL)),
            resident((1, D_MODEL)),
            resident((D_MODEL, 2 * d_ff)),
            resident((d_ff, D_MODEL)),
            resident((1, D_MODEL)),
            resident((1, D_MODEL)),
        ],
        out_specs=pl.BlockSpec((tm, D_MODEL), row),
        out_shape=jax.ShapeDtypeStruct((rows, D_MODEL), F32),
        compiler_params=pltpu.CompilerParams(
            dimension_semantics=("arbitrary",), vmem_limit_bytes=VMEM_LIMIT),
        name="outproj_ffn",
    )(x2d, am, cm, wout, g1, b1, wgu, wd, g2, b2)


def _rotate_half_cols(w):
    d_in, n = w.shape
    w4 = w.reshape(d_in, n // HEAD_DIM, 2, HEAD_DIM // 2)
    return jnp.concatenate([-w4[:, :, 1:2, :], w4[:, :, 0:1, :]], axis=2).reshape(d_in, n)


def _split_cols(w, sizes):
    out, o = [], 0
    for s in sizes:
        out.append(w[:, o:o + s])
        o += s
    return out


def kernel(x, w_in, conv_w, mix_norm_g, w_out, ln1_g, ln1_b, w_gate_up, w_down, ln2_g, ln2_b):
    bsz, seq, _ = x.shape
    depth = w_in.shape[0]
    alpha = (2 * depth) ** 0.25
    topk = min(INDEX_TOPK_MAX, seq // 4)
    assert seq % SEQ_TILE == 0 and (bsz * seq) % ROW_TILE == 0 and topk <= SEQ_TILE

    inv_freq = 1.0 / (ROPE_THETA ** (jnp.arange(0, HEAD_DIM, 2, dtype=F32) / HEAD_DIM))
    ang = jnp.arange(seq, dtype=F32)[:, None] * inv_freq[None, :]
    cos, sin = jnp.cos(ang), jnp.sin(ang)
    cos_n, sin_n = jnp.tile(cos, (1, LANES // cos.shape[1])), jnp.tile(sin, (1, LANES // sin.shape[1]))
    cos_t, sin_t = cos.T, sin.T

    for layer in range(depth):
        wq, wk, wv, wqi, wki, wwi, wgb, wgc, wh = _split_cols(w_in[layer], IN_SPLIT_SIZES)
        pad_ki = jnp.zeros((D_MODEL, LANES - IDX_DIM), F32)
        wn = jnp.concatenate([wk, _rotate_half_cols(wk), wgb, wgc, wh,
                              wki, pad_ki, _rotate_half_cols(wki), pad_ki], axis=1).astype(BF16)
        pad_wi = jnp.zeros((D_MODEL, 2 * SUBLANES - IDX_HEADS), F32)
        wt = jnp.concatenate([wq, wqi, wv, wwi, pad_wi], axis=1).T.astype(BF16)
        g = mix_norm_g[layer]
        kn, kin, qt, qit, vt, w_t, cm = _proj_call(
            x, wn, wt, cos_n, sin_n, cos_t, sin_t, conv_w[layer], g[ATT_WIDTH:][None, :])
        am = _dsa_call(kn, kin, vt, qt, qit, w_t, g[:ATT_WIDTH][None, :], topk)
        y = _ffn_call(
            x.reshape(bsz * seq, D_MODEL), am.reshape(bsz * seq, ATT_WIDTH), cm.reshape(bsz * seq, CONV_WIDTH),
            w_out[layer].astype(BF16), ln1_g[layer][None, :], ln1_b[layer][None, :],
            w_gate_up[layer].astype(BF16), w_down[layer].astype(BF16),
            ln2_g[layer][None, :], ln2_b[layer][None, :], alpha)
        x = y.reshape(bsz, seq, D_MODEL)
    return x
```

```python
import functools

import jax
import jax.numpy as jnp
from jax import lax
from jax.experimental import pallas as pl
from jax.experimental.pallas import tpu as pltpu

D_MODEL = 1024
ATT_HEADS = 8
HEAD_DIM = 64
ATT_WIDTH = ATT_HEADS * HEAD_DIM
CONV_WIDTH = D_MODEL - ATT_WIDTH
CONV_K = 3
IDX_HEADS = 8
IDX_DIM = 64
INDEX_TOPK_MAX = 256
ROPE_THETA = 10000.0
NORM_EPS = 1e-5
IN_SPLIT_SIZES = (ATT_WIDTH, ATT_WIDTH, ATT_WIDTH, IDX_HEADS * IDX_DIM, IDX_DIM, IDX_HEADS,
                  CONV_WIDTH, CONV_WIDTH, CONV_WIDTH)

LANES = 128
SUBLANES = 8
SEQ_TILE = 256
ROW_TILE = 256
V_ROWS = 80
LOG2E = 1.4426950408889634
VMEM_LIMIT = 56 * 1024 * 1024

F32 = jnp.float32
BF16 = jnp.bfloat16
INT_MIN = -2 ** 31
KEY_NEG_INF = (0xFF800000 ^ 0x7FFFFFFF) - 2 ** 32
KEY_POS_INF = 0x7F800000


def _proj_kernel(x_ref, wn_ref, wt_ref, cosn_ref, sinn_ref, cost_ref, sint_ref, convw_ref, gconv_ref,
                 kn_ref, kin_ref, qt_ref, qit_ref, vt_ref, wt_out_ref, cm_ref, u_ref, *, w_scale):
    t = pl.program_id(1)
    tm = x_ref.shape[1]
    xb = x_ref[0].astype(BF16)
    nat = jnp.dot(xb, wn_ref[...], preferred_element_type=F32)
    tr = lax.dot_general(wt_ref[...], xb, (((1,), (1,)), ((), ())),
                         preferred_element_type=F32)

    cos_n = cosn_ref[...]
    sin_n = sinn_ref[...]
    cos4 = jnp.concatenate([cos_n] * (ATT_WIDTH // LANES), axis=1)
    sin4 = jnp.concatenate([sin_n] * (ATT_WIDTH // LANES), axis=1)
    k_rope = nat[:, 0:ATT_WIDTH] * cos4 + nat[:, ATT_WIDTH:2 * ATT_WIDTH] * sin4
    kn_ref[0] = k_rope.astype(BF16)
    o_ki = 2 * ATT_WIDTH + 3 * CONV_WIDTH
    ki_rope = nat[:, o_ki:o_ki + LANES] * cos_n + nat[:, o_ki + LANES:o_ki + 2 * LANES] * sin_n
    kin_ref[0] = ki_rope[:, :IDX_DIM].astype(BF16)

    cos_t = cost_ref[...]
    sin_t = sint_ref[...]
    half = HEAD_DIM // 2
    q_scale = (HEAD_DIM ** -0.5) * LOG2E
    for h in range(ATT_HEADS):
        r0 = h * HEAD_DIM
        x1 = tr[r0:r0 + half]
        x2 = tr[r0 + half:r0 + HEAD_DIM]
        qt_ref[0, 0, r0:r0 + half, :] = ((x1 * cos_t - x2 * sin_t) * q_scale).astype(BF16)
        qt_ref[0, 0, r0 + half:r0 + HEAD_DIM, :] = ((x2 * cos_t + x1 * sin_t) * q_scale).astype(BF16)
    for h in range(IDX_HEADS):
        r0 = ATT_WIDTH + h * IDX_DIM
        x1 = tr[r0:r0 + half]
        x2 = tr[r0 + half:r0 + IDX_DIM]
        qit_ref[0, 0, h * IDX_DIM:h * IDX_DIM + half, :] = (x1 * cos_t - x2 * sin_t).astype(BF16)
        qit_ref[0, 0, h * IDX_DIM + half:(h + 1) * IDX_DIM, :] = (x2 * cos_t + x1 * sin_t).astype(BF16)
    o_v = ATT_WIDTH + IDX_HEADS * IDX_DIM
    pad_row = lax.broadcasted_iota(jnp.int32, (V_ROWS - HEAD_DIM, tm), 0)
    ones_pad = jnp.where(pad_row == 0, 1.0, 0.0).astype(BF16)
    for h in range(ATT_HEADS):
        vt_ref[0, 0, h * V_ROWS:h * V_ROWS + HEAD_DIM, :] = (
            tr[o_v + h * HEAD_DIM:o_v + (h + 1) * HEAD_DIM].astype(BF16))
        vt_ref[0, 0, h * V_ROWS + HEAD_DIM:(h + 1) * V_ROWS, :] = ones_pad
    wt_out_ref[0, 0] = tr[o_v + ATT_WIDTH:o_v + ATT_WIDTH + IDX_HEADS] * w_scale

    o_c = 2 * ATT_WIDTH
    gate_b = nat[:, o_c:o_c + CONV_WIDTH]
    gate_c = nat[:, o_c + CONV_WIDTH:o_c + 2 * CONV_WIDTH]
    hh = nat[:, o_c + 2 * CONV_WIDTH:o_c + 3 * CONV_WIDTH]
    u = gate_c * hh

    @pl.when(t == 0)
    def _():
        u_ref[0:SUBLANES, :] = jnp.zeros((SUBLANES, CONV_WIDTH), F32)

    u_ref[SUBLANES:SUBLANES + tm, :] = u
    um1 = u_ref[SUBLANES - 1:SUBLANES - 1 + tm, :]
    um2 = u_ref[SUBLANES - 2:SUBLANES - 2 + tm, :]
    cw = convw_ref[...]
    y = cw[0:1, :] * um2 + cw[1:2, :] * um1 + cw[2:3, :] * u
    u_ref[0:SUBLANES, :] = u[tm - SUBLANES:tm, :]
    conv_out = gate_b * y
    ms = jnp.mean(conv_out * conv_out, axis=-1, keepdims=True)
    cm_ref[0] = (conv_out * lax.rsqrt(ms + NORM_EPS) * gconv_ref[...]).astype(BF16)


def _proj_call(x, wn, wt, cos_n, sin_n, cos_t, sin_t, conv_w, g_conv):
    bsz, seq, _ = x.shape
    tm = SEQ_TILE
    nt = seq // tm
    n_nat = wn.shape[1]
    n_tr = wt.shape[0]
    w_scale = (IDX_DIM ** -0.5) * (IDX_HEADS ** -0.5)
    const2 = lambda b, t: (0, 0)
    out_shapes = (
        jax.ShapeDtypeStruct((bsz, seq, ATT_WIDTH), BF16),
        jax.ShapeDtypeStruct((bsz, seq, IDX_DIM), BF16),
        jax.ShapeDtypeStruct((bsz, nt, ATT_WIDTH, tm), BF16),
        jax.ShapeDtypeStruct((bsz, nt, IDX_HEADS * IDX_DIM, tm), BF16),
        jax.ShapeDtypeStruct((bsz, nt, ATT_HEADS * V_ROWS, tm), BF16),
        jax.ShapeDtypeStruct((bsz, nt, IDX_HEADS, tm), F32),
        jax.ShapeDtypeStruct((bsz, seq, CONV_WIDTH), BF16),
    )
    tile4 = lambda b, t: (b, t, 0, 0)
    return pl.pallas_call(
        functools.partial(_proj_kernel, w_scale=w_scale),
        grid=(bsz, nt),
        in_specs=[
            pl.BlockSpec((1, tm, D_MODEL), lambda b, t: (b, t, 0)),
            pl.BlockSpec((D_MODEL, n_nat), const2),
            pl.BlockSpec((n_tr, D_MODEL), const2),
            pl.BlockSpec((tm, LANES), lambda b, t: (t, 0)),
            pl.BlockSpec((tm, LANES), lambda b, t: (t, 0)),
            pl.BlockSpec((HEAD_DIM // 2, tm), lambda b, t: (0, t)),
            pl.BlockSpec((HEAD_DIM // 2, tm), lambda b, t: (0, t)),
            pl.BlockSpec((CONV_K, CONV_WIDTH), const2),
            pl.BlockSpec((1, CONV_WIDTH), const2),
        ],
        out_specs=(
            pl.BlockSpec((1, tm, ATT_WIDTH), lambda b, t: (b, t, 0)),
            pl.BlockSpec((1, tm, IDX_DIM), lambda b, t: (b, t, 0)),
            pl.BlockSpec((1, 1, ATT_WIDTH, tm), tile4),
            pl.BlockSpec((1, 1, IDX_HEADS * IDX_DIM, tm), tile4),
            pl.BlockSpec((1, 1, ATT_HEADS * V_ROWS, tm), tile4),
            pl.BlockSpec((1, 1, IDX_HEADS, tm), tile4),
            pl.BlockSpec((1, tm, CONV_WIDTH), lambda b, t: (b, t, 0)),
        ),
        out_shape=out_shapes,
        scratch_shapes=[pltpu.VMEM((tm + SUBLANES, CONV_WIDTH), F32)],
        compiler_params=pltpu.CompilerParams(
            dimension_semantics=("arbitrary", "arbitrary"), vmem_limit_bytes=VMEM_LIMIT),
        name="proj_rope_conv",
    )(x, wn, wt, cos_n, sin_n, cos_t, sin_t, conv_w, g_conv)


def _key_to_float(key):
    key = jnp.clip(key, KEY_NEG_INF, KEY_POS_INF)
    bits = jnp.where(key >= 0, key, key ^ 0x7FFFFFFF)
    return lax.bitcast_convert_type(bits, F32)


def _fold_rows(a):
    return a.reshape(a.shape[0] // SUBLANES, SUBLANES, a.shape[1])


def _dsa_kernel(kn_ref, kin_ref, vt_ref, qt_ref, qit_ref, w_ref, g_ref, out_ref,
                score_ref, wbd_ref, acc_ref, at_ref, *, topk, seq):
    i = pl.program_id(1)
    tq = qt_ref.shape[3]
    tk = tq
    nchunks = i + 1
    neg_inf = jnp.float32(-jnp.inf)

    row_iota = lax.broadcasted_iota(jnp.int32, (tk, tq), 0)
    qpos = i * tq + lax.broadcasted_iota(jnp.int32, (tk, tq), 1)

    def chunk_off(c):
        return pl.multiple_of(c * tk, tk)

    qi_all = qit_ref[0, 0]
    w_rows = w_ref[0, 0]

    def score_body(c, carry):
        off = chunk_off(c)
        kic = kin_ref[0, pl.ds(off, tk), :]
        sc = jnp.zeros((tk, tq), F32)
        for h in range(IDX_HEADS):
            d = jnp.dot(kic, qi_all[h * IDX_DIM:(h + 1) * IDX_DIM, :], preferred_element_type=F32)
            sc = sc + jnp.maximum(d, 0.0) * w_rows[h:h + 1, :]
        sc = jnp.where(off + row_iota <= qpos, sc, neg_inf)
        score_ref[pl.ds(off, tk), :] = sc
        return carry

    lax.fori_loop(0, nchunks, score_body, 0)

    def count_rows(pred_fn):
        def count_chunk(c):
            off = chunk_off(c)
            m = pred_fn(score_ref[pl.ds(off, tk), :], off).astype(jnp.int32)
            return jnp.sum(_fold_rows(m), axis=0)

        def body(c2, acc):
            return acc + count_chunk(2 * c2) + count_chunk(2 * c2 + 1)

        acc = lax.fori_loop(0, nchunks // 2, body, jnp.zeros((SUBLANES, tq), jnp.int32))
        acc = lax.cond(nchunks % 2 == 1, lambda a: a + count_chunk(nchunks - 1), lambda a: a, acc)
        return jnp.sum(acc, axis=0, keepdims=True)

    def search_body(p, carry):
        u, n_u = carry
        trial = u | jnp.left_shift(jnp.int32(1), 31 - p)
        cand = _key_to_float(trial ^ INT_MIN)
        cnt = count_rows(lambda sc, off: sc >= cand)
        take = cnt >= topk
        return jnp.where(take, trial, u), jnp.where(take, cnt, n_u)

    u, n_ge = lax.fori_loop(0, 32, search_body,
                            (jnp.zeros((1, tq), jnp.int32), jnp.zeros((1, tq), jnp.int32) + nchunks * tk))
    thr = _key_to_float(u ^ INT_MIN)

    n_gt = count_rows(lambda sc, off: sc > thr)
    want = topk - n_gt
    has_excess = jnp.max(n_ge - n_gt - want) > 0

    def tie_search():
        def body(p, jp):
            trial = jp + jnp.left_shift(jnp.int32(1), (seq - 1).bit_length() - 1 - p)
            cnt = count_rows(lambda sc, off: (sc == thr) & (off + row_iota <= trial))
            return jnp.where(cnt < want, trial, jp)
        jp = lax.fori_loop(0, (seq - 1).bit_length(), body, jnp.full((1, tq), -1, jnp.int32))
        return jp + 1

    jlim = lax.cond(has_excess, tie_search, lambda: jnp.full((1, tq), seq, jnp.int32))

    def bias_body(c, carry):
        off = chunk_off(c)
        sc = score_ref[pl.ds(off, tk), :]
        kpos = off + row_iota
        sel = ((sc > thr) | ((sc == thr) & (kpos <= jlim))) & (kpos <= qpos)
        score_ref[pl.ds(off, tk), :] = jnp.where(sel, 0.0, neg_inf)
        return carry

    lax.fori_loop(0, nchunks, bias_body, 0)

    q_all = qt_ref[0, 0]
    zeros_q = jnp.zeros((HEAD_DIM, tq), BF16)
    n_pairs = ATT_HEADS // 2
    for j in range(n_pairs):
        r0 = 2 * j * HEAD_DIM
        wbd_ref[j] = jnp.concatenate([
            jnp.concatenate([q_all[r0:r0 + HEAD_DIM], zeros_q], axis=1),
            jnp.concatenate([zeros_q, q_all[r0 + HEAD_DIM:r0 + 2 * HEAD_DIM]], axis=1)], axis=0)
    acc_ref[...] = jnp.zeros(acc_ref.shape, F32)

    def attn_body(c, m_all):
        off = chunk_off(c)
        bias = score_ref[pl.ds(off, tk), :]
        bias2 = jnp.concatenate([bias, bias], axis=1)
        m_next = []
        updates = []

        def pair_logits(j):
            kc = kn_ref[0, pl.ds(off, tk), 2 * j * HEAD_DIM:2 * (j + 1) * HEAD_DIM]
            return jnp.dot(kc, wbd_ref[j], preferred_element_type=F32)

        lg_ahead = pair_logits(0)
        for j in range(n_pairs):
            lg = lg_ahead + bias2
            if j + 1 < n_pairs:
                lg_ahead = pair_logits(j + 1)
            m_old = m_all[:, 2 * j * tq:2 * (j + 1) * tq]
            m_new = jnp.maximum(m_old, jnp.max(jnp.max(_fold_rows(lg), axis=0), axis=0, keepdims=True))
            m_use = jnp.where(m_new == neg_inf, 0.0, m_new)
            alpha = jnp.exp2(m_old - m_use)
            p = jnp.exp2(lg - m_use).astype(BF16)
            for hh in range(2):
                h = 2 * j + hh
                pv = jnp.dot(vt_ref[0, c, h * V_ROWS:(h + 1) * V_ROWS, :], p[:, hh * tq:(hh + 1) * tq],
                             preferred_element_type=F32)
                updates.append((h, alpha[:, hh * tq:(hh + 1) * tq], pv))
            m_next.append(m_new)
        for h, a, pv in updates:
            acc_ref[h] = acc_ref[h] * a + pv
        return jnp.concatenate(m_next, axis=1)

    lax.fori_loop(0, nchunks, attn_body, jnp.full((1, ATT_HEADS * tq), neg_inf, F32))
    for h in range(ATT_HEADS):
        o = acc_ref[h]
        at_ref[h * HEAD_DIM:(h + 1) * HEAD_DIM, :] = o[:HEAD_DIM] / o[HEAD_DIM:HEAD_DIM + 1]

    a_t = at_ref[...]
    ms = jnp.mean(a_t * a_t, axis=0, keepdims=True)
    y = (a_t * lax.rsqrt(ms + NORM_EPS)).T
    out_ref[0] = (y * g_ref[...]).astype(BF16)


def _dsa_call(kn, kin, vt, qt, qit, w_t, g_attn, topk):
    bsz, seq, _ = kn.shape
    nt, tq = qt.shape[1], qt.shape[3]
    tile4 = lambda b, i: (b, i, 0, 0)
    return pl.pallas_call(
        functools.partial(_dsa_kernel, topk=topk, seq=seq),
        grid=(bsz, nt),
        in_specs=[
            pl.BlockSpec((1, seq, ATT_WIDTH), lambda b, i: (b, 0, 0)),
            pl.BlockSpec((1, seq, IDX_DIM), lambda b, i: (b, 0, 0)),
            pl.BlockSpec((1, nt, ATT_HEADS * V_ROWS, tq), lambda b, i: (b, 0, 0, 0)),
            pl.BlockSpec((1, 1, ATT_WIDTH, tq), tile4),
            pl.BlockSpec((1, 1, IDX_HEADS * IDX_DIM, tq), tile4),
            pl.BlockSpec((1, 1, IDX_HEADS, tq), tile4),
            pl.BlockSpec((1, ATT_WIDTH), lambda b, i: (0, 0)),
        ],
        out_specs=pl.BlockSpec((1, tq, ATT_WIDTH), lambda b, i: (b, i, 0)),
        out_shape=jax.ShapeDtypeStruct((bsz, seq, ATT_WIDTH), BF16),
        scratch_shapes=[
            pltpu.VMEM((seq, tq), F32),
            pltpu.VMEM((ATT_HEADS // 2, 2 * HEAD_DIM, 2 * tq), BF16),
            pltpu.VMEM((ATT_HEADS, V_ROWS, tq), F32),
            pltpu.VMEM((ATT_WIDTH, tq), F32),
        ],
        compiler_params=pltpu.CompilerParams(
            dimension_semantics=("arbitrary", "arbitrary"), vmem_limit_bytes=VMEM_LIMIT),
        name="dsa_attention",
    )(kn, kin, vt, qt, qit, w_t, g_attn)


def _layer_norm(y, g, b):
    mu = jnp.mean(y, axis=-1, keepdims=True)
    d = y - mu
    var = jnp.mean(d * d, axis=-1, keepdims=True)
    return d * lax.rsqrt(var + NORM_EPS) * g + b


def _ffn_kernel(x_ref, am_ref, cm_ref, wout_ref, g1_ref, b1_ref, wgu_ref, wd_ref, g2_ref, b2_ref,
                o_ref, *, alpha, d_ff):
    x = x_ref[...]
    mix = (jnp.dot(am_ref[...], wout_ref[0:ATT_WIDTH, :], preferred_element_type=F32)
           + jnp.dot(cm_ref[...], wout_ref[ATT_WIDTH:D_MODEL, :], preferred_element_type=F32))
    x1 = _layer_norm(alpha * x + mix, g1_ref[...], b1_ref[...])
    gu = jnp.dot(x1.astype(BF16), wgu_ref[...], preferred_element_type=F32)
    gate = gu[:, :d_ff]
    up = gu[:, d_ff:]
    hidden = (gate * jax.nn.sigmoid(gate) * up).astype(BF16)
    ffn = jnp.dot(hidden, wd_ref[...], preferred_element_type=F32)
    o_ref[...] = _layer_norm(alpha * x1 + ffn, g2_ref[...], b2_ref[...])


def _ffn_call(x2d, am, cm, wout, g1, b1, wgu, wd, g2, b2, alpha):
    rows = x2d.shape[0]
    tm = ROW_TILE
    d_ff = wd.shape[0]
    row = lambda r: (r, 0)
    const = lambda r: (0, 0)
    resident = functools.partial(pl.BlockSpec, index_map=const, pipeline_mode=pl.Buffered(1))
    return pl.pallas_call(
        functools.partial(_ffn_kernel, alpha=alpha, d_ff=d_ff),
        grid=(rows // tm,),
        in_specs=[
            pl.BlockSpec((tm, D_MODEL), row),
            pl.BlockSpec((tm, ATT_WIDTH), row),
            pl.BlockSpec((tm, CONV_WIDTH), row),
            resident((D_MODEL, D_MODEL)),
            resident((1, D_MODEL)),
            resident((1, D_MODEL)),
            resident((D_MODEL, 2 * d_ff)),
            resident((d_ff, D_MODEL)),
            resident((1, D_MODEL)),
            resident((1, D_MODEL)),
        ],
        out_specs=pl.BlockSpec((tm, D_MODEL), row),
        out_shape=jax.ShapeDtypeStruct((rows, D_MODEL), F32),
        compiler_params=pltpu.CompilerParams(
            dimension_semantics=("arbitrary",), vmem_limit_bytes=VMEM_LIMIT),
        name="outproj_ffn",
    )(x2d, am, cm, wout, g1, b1, wgu, wd, g2, b2)


def _rotate_half_cols(w):
    d_in, n = w.shape
    w4 = w.reshape(d_in, n // HEAD_DIM, 2, HEAD_DIM // 2)
    return jnp.concatenate([-w4[:, :, 1:2, :], w4[:, :, 0:1, :]], axis=2).reshape(d_in, n)


def _split_cols(w, sizes):
    out, o = [], 0
    for s in sizes:
        out.append(w[:, o:o + s])
        o += s
    return out


def kernel(x, w_in, conv_w, mix_norm_g, w_out, ln1_g, ln1_b, w_gate_up, w_down, ln2_g, ln2_b):
    bsz, seq, _ = x.shape
    depth = w_in.shape[0]
    alpha = (2 * depth) ** 0.25
    topk = min(INDEX_TOPK_MAX, seq // 4)
    assert seq % SEQ_TILE == 0 and (bsz * seq) % ROW_TILE == 0 and topk <= SEQ_TILE

    inv_freq = 1.0 / (ROPE_THETA ** (jnp.arange(0, HEAD_DIM, 2, dtype=F32) / HEAD_DIM))
    ang = jnp.arange(seq, dtype=F32)[:, None] * inv_freq[None, :]
    cos, sin = jnp.cos(ang), jnp.sin(ang)
    cos_n, sin_n = jnp.tile(cos, (1, LANES // cos.shape[1])), jnp.tile(sin, (1, LANES // sin.shape[1]))
    cos_t, sin_t = cos.T, sin.T

    for layer in range(depth):
        wq, wk, wv, wqi, wki, wwi, wgb, wgc, wh = _split_cols(w_in[layer], IN_SPLIT_SIZES)
        pad_ki = jnp.zeros((D_MODEL, LANES - IDX_DIM), F32)
        wn = jnp.concatenate([wk, _rotate_half_cols(wk), wgb, wgc, wh,
                              wki, pad_ki, _rotate_half_cols(wki), pad_ki], axis=1).astype(BF16)
        pad_wi = jnp.zeros((D_MODEL, 2 * SUBLANES - IDX_HEADS), F32)
        wt = jnp.concatenate([wq, wqi, wv, wwi, pad_wi], axis=1).T.astype(BF16)
        g = mix_norm_g[layer]
        kn, kin, qt, qit, vt, w_t, cm = _proj_call(
            x, wn, wt, cos_n, sin_n, cos_t, sin_t, conv_w[layer], g[ATT_WIDTH:][None, :])
        am = _dsa_call(kn, kin, vt, qt, qit, w_t, g[:ATT_WIDTH][None, :], topk)
        y = _ffn_call(
            x.reshape(bsz * seq, D_MODEL), am.reshape(bsz * seq, ATT_WIDTH), cm.reshape(bsz * seq, CONV_WIDTH),
            w_out[layer].astype(BF16), ln1_g[layer][None, :], ln1_b[layer][None, :],
            w_gate_up[layer].astype(BF16), w_down[layer].astype(BF16),
            ln2_g[layer][None, :], ln2_b[layer][None, :], alpha)
        x = y.reshape(bsz, seq, D_MODEL)
    return x
```

```python
import functools

import jax
import jax.numpy as jnp
from jax import lax
from jax.experimental import pallas as pl
from jax.experimental.pallas import tpu as pltpu

D_MODEL = 1024
ATT_HEADS = 8
HEAD_DIM = 64
ATT_WIDTH = ATT_HEADS * HEAD_DIM
CONV_WIDTH = D_MODEL - ATT_WIDTH
CONV_K = 3
IDX_HEADS = 8
IDX_DIM = 64
INDEX_TOPK_MAX = 256
ROPE_THETA = 10000.0
NORM_EPS = 1e-5
IN_SPLIT_SIZES = (ATT_WIDTH, ATT_WIDTH, ATT_WIDTH, IDX_HEADS * IDX_DIM, IDX_DIM, IDX_HEADS,
                  CONV_WIDTH, CONV_WIDTH, CONV_WIDTH)

LANES = 128
SUBLANES = 8
SEQ_TILE = 256
ROW_TILE = 256
COUNT_UNROLL = 4
COUNT_ACCS = 4
V_ROWS = 80
LOG2E = 1.4426950408889634
VMEM_LIMIT = 56 * 1024 * 1024

F32 = jnp.float32
BF16 = jnp.bfloat16
INT_MIN = -2 ** 31
KEY_NEG_INF = (0xFF800000 ^ 0x7FFFFFFF) - 2 ** 32
KEY_POS_INF = 0x7F800000


def _proj_kernel(x_ref, wn_ref, wt_ref, cosn_ref, sinn_ref, cost_ref, sint_ref, convw_ref, gconv_ref,
                 kn_ref, kin_ref, qt_ref, qit_ref, vt_ref, wt_out_ref, cm_ref, u_ref, *, w_scale):
    t = pl.program_id(1)
    tm = x_ref.shape[1]
    xb = x_ref[0].astype(BF16)
    nat = jnp.dot(xb, wn_ref[...], preferred_element_type=F32)
    tr = lax.dot_general(wt_ref[...], xb, (((1,), (1,)), ((), ())),
                         preferred_element_type=F32)

    cos_n = cosn_ref[...]
    sin_n = sinn_ref[...]
    cos4 = jnp.concatenate([cos_n] * (ATT_WIDTH // LANES), axis=1)
    sin4 = jnp.concatenate([sin_n] * (ATT_WIDTH // LANES), axis=1)
    k_rope = nat[:, 0:ATT_WIDTH] * cos4 + nat[:, ATT_WIDTH:2 * ATT_WIDTH] * sin4
    kn_ref[0] = k_rope.astype(BF16)
    o_ki = 2 * ATT_WIDTH + 3 * CONV_WIDTH
    ki_rope = nat[:, o_ki:o_ki + LANES] * cos_n + nat[:, o_ki + LANES:o_ki + 2 * LANES] * sin_n
    kin_ref[0] = ki_rope[:, :IDX_DIM].astype(BF16)

    cos_t = cost_ref[...]
    sin_t = sint_ref[...]
    half = HEAD_DIM // 2
    q_scale = (HEAD_DIM ** -0.5) * LOG2E
    for h in range(ATT_HEADS):
        r0 = h * HEAD_DIM
        x1 = tr[r0:r0 + half]
        x2 = tr[r0 + half:r0 + HEAD_DIM]
        qt_ref[0, 0, r0:r0 + half, :] = ((x1 * cos_t - x2 * sin_t) * q_scale).astype(BF16)
        qt_ref[0, 0, r0 + half:r0 + HEAD_DIM, :] = ((x2 * cos_t + x1 * sin_t) * q_scale).astype(BF16)
    for h in range(IDX_HEADS):
        r0 = ATT_WIDTH + h * IDX_DIM
        x1 = tr[r0:r0 + half]
        x2 = tr[r0 + half:r0 + IDX_DIM]
        qit_ref[0, 0, h * IDX_DIM:h * IDX_DIM + half, :] = (x1 * cos_t - x2 * sin_t).astype(BF16)
        qit_ref[0, 0, h * IDX_DIM + half:(h + 1) * IDX_DIM, :] = (x2 * cos_t + x1 * sin_t).astype(BF16)
    o_v = ATT_WIDTH + IDX_HEADS * IDX_DIM
    pad_row = lax.broadcasted_iota(jnp.int32, (V_ROWS - HEAD_DIM, tm), 0)
    ones_pad = jnp.where(pad_row == 0, 1.0, 0.0).astype(BF16)
    for h in range(ATT_HEADS):
        vt_ref[0, 0, h * V_ROWS:h * V_ROWS + HEAD_DIM, :] = (
            tr[o_v + h * HEAD_DIM:o_v + (h + 1) * HEAD_DIM].astype(BF16))
        vt_ref[0, 0, h * V_ROWS + HEAD_DIM:(h + 1) * V_ROWS, :] = ones_pad
    wt_out_ref[0, 0] = tr[o_v + ATT_WIDTH:o_v + ATT_WIDTH + IDX_HEADS] * w_scale

    o_c = 2 * ATT_WIDTH
    gate_b = nat[:, o_c:o_c + CONV_WIDTH]
    gate_c = nat[:, o_c + CONV_WIDTH:o_c + 2 * CONV_WIDTH]
    hh = nat[:, o_c + 2 * CONV_WIDTH:o_c + 3 * CONV_WIDTH]
    u = gate_c * hh

    @pl.when(t == 0)
    def _():
        u_ref[0:SUBLANES, :] = jnp.zeros((SUBLANES, CONV_WIDTH), F32)

    u_ref[SUBLANES:SUBLANES + tm, :] = u
    um1 = u_ref[SUBLANES - 1:SUBLANES - 1 + tm, :]
    um2 = u_ref[SUBLANES - 2:SUBLANES - 2 + tm, :]
    cw = convw_ref[...]
    y = cw[0:1, :] * um2 + cw[1:2, :] * um1 + cw[2:3, :] * u
    u_ref[0:SUBLANES, :] = u[tm - SUBLANES:tm, :]
    conv_out = gate_b * y
    ms = jnp.mean(conv_out * conv_out, axis=-1, keepdims=True)
    cm_ref[0] = (conv_out * lax.rsqrt(ms + NORM_EPS) * gconv_ref[...]).astype(BF16)


def _proj_call(x, wn, wt, cos_n, sin_n, cos_t, sin_t, conv_w, g_conv):
    bsz, seq, _ = x.shape
    tm = SEQ_TILE
    nt = seq // tm
    n_nat = wn.shape[1]
    n_tr = wt.shape[0]
    w_scale = (IDX_DIM ** -0.5) * (IDX_HEADS ** -0.5)
    const2 = lambda b, t: (0, 0)
    out_shapes = (
        jax.ShapeDtypeStruct((bsz, seq, ATT_WIDTH), BF16),
        jax.ShapeDtypeStruct((bsz, seq, IDX_DIM), BF16),
        jax.ShapeDtypeStruct((bsz, nt, ATT_WIDTH, tm), BF16),
        jax.ShapeDtypeStruct((bsz, nt, IDX_HEADS * IDX_DIM, tm), BF16),
        jax.ShapeDtypeStruct((bsz, nt, ATT_HEADS * V_ROWS, tm), BF16),
        jax.ShapeDtypeStruct((bsz, nt, IDX_HEADS, tm), F32),
        jax.ShapeDtypeStruct((bsz, seq, CONV_WIDTH), BF16),
    )
    tile4 = lambda b, t: (b, t, 0, 0)
    return pl.pallas_call(
        functools.partial(_proj_kernel, w_scale=w_scale),
        grid=(bsz, nt),
        in_specs=[
            pl.BlockSpec((1, tm, D_MODEL), lambda b, t: (b, t, 0)),
            pl.BlockSpec((D_MODEL, n_nat), const2),
            pl.BlockSpec((n_tr, D_MODEL), const2),
            pl.BlockSpec((tm, LANES), lambda b, t: (t, 0)),
            pl.BlockSpec((tm, LANES), lambda b, t: (t, 0)),
            pl.BlockSpec((HEAD_DIM // 2, tm), lambda b, t: (0, t)),
            pl.BlockSpec((HEAD_DIM // 2, tm), lambda b, t: (0, t)),
            pl.BlockSpec((CONV_K, CONV_WIDTH), const2),
            pl.BlockSpec((1, CONV_WIDTH), const2),
        ],
        out_specs=(
            pl.BlockSpec((1, tm, ATT_WIDTH), lambda b, t: (b, t, 0)),
            pl.BlockSpec((1, tm, IDX_DIM), lambda b, t: (b, t, 0)),
            pl.BlockSpec((1, 1, ATT_WIDTH, tm), tile4),
            pl.BlockSpec((1, 1, IDX_HEADS * IDX_DIM, tm), tile4),
            pl.BlockSpec((1, 1, ATT_HEADS * V_ROWS, tm), tile4),
            pl.BlockSpec((1, 1, IDX_HEADS, tm), tile4),
            pl.BlockSpec((1, tm, CONV_WIDTH), lambda b, t: (b, t, 0)),
        ),
        out_shape=out_shapes,
        scratch_shapes=[pltpu.VMEM((tm + SUBLANES, CONV_WIDTH), F32)],
        compiler_params=pltpu.CompilerParams(
            dimension_semantics=("arbitrary", "arbitrary"), vmem_limit_bytes=VMEM_LIMIT),
        name="proj_rope_conv",
    )(x, wn, wt, cos_n, sin_n, cos_t, sin_t, conv_w, g_conv)


def _key_to_float(key):
    key = jnp.clip(key, KEY_NEG_INF, KEY_POS_INF)
    bits = jnp.where(key >= 0, key, key ^ 0x7FFFFFFF)
    return lax.bitcast_convert_type(bits, F32)


def _fold_rows(a):
    return a.reshape(a.shape[0] // SUBLANES, SUBLANES, a.shape[1])


def _dsa_kernel(kn_ref, kin_ref, vt_ref, qt_ref, qit_ref, w_ref, g_ref, out_ref,
                score_ref, wbd_ref, acc_ref, at_ref, *, topk, seq):
    i = pl.program_id(1)
    tq = qt_ref.shape[3]
    tk = tq
    nchunks = i + 1
    neg_inf = jnp.float32(-jnp.inf)

    row_iota = lax.broadcasted_iota(jnp.int32, (tk, tq), 0)
    qpos = i * tq + lax.broadcasted_iota(jnp.int32, (tk, tq), 1)

    def chunk_off(c):
        return pl.multiple_of(c * tk, tk)

    qi_all = qit_ref[0, 0]
    w_rows = w_ref[0, 0]

    def score_body(c, carry):
        off = chunk_off(c)
        kic = kin_ref[0, pl.ds(off, tk), :]
        sc = jnp.zeros((tk, tq), F32)
        for h in range(IDX_HEADS):
            d = jnp.dot(kic, qi_all[h * IDX_DIM:(h + 1) * IDX_DIM, :], preferred_element_type=F32)
            sc = sc + jnp.maximum(d, 0.0) * w_rows[h:h + 1, :]
        sc = jnp.where(off + row_iota <= qpos, sc, neg_inf)
        score_ref[pl.ds(off, tk), :] = sc
        return carry

    def score_body2(c2, carry):
        return score_body(2 * c2 + 1, score_body(2 * c2, carry))

    lax.fori_loop(0, nchunks // 2, score_body2, 0)
    lax.fori_loop(2 * (nchunks // 2), nchunks, score_body, 0)

    row_iota3 = (lax.broadcasted_iota(jnp.int32, (tk // SUBLANES, SUBLANES, tq), 0) * SUBLANES
                 + lax.broadcasted_iota(jnp.int32, (tk // SUBLANES, SUBLANES, tq), 1))
    qpos3 = i * tq + lax.broadcasted_iota(jnp.int32, (tk // SUBLANES, SUBLANES, tq), 2)

    def count_rows(pred_fn):
        def count_chunk(c, accs):
            off = chunk_off(c)
            m = pred_fn(_fold_rows(score_ref[pl.ds(off, tk), :]), off).astype(jnp.int32)
            per = m.shape[0] // COUNT_ACCS
            return tuple(acc + jnp.sum(m[g * per:(g + 1) * per], axis=0) for g, acc in enumerate(accs))

        def unrolled(cu, accs):
            for k in range(COUNT_UNROLL):
                accs = count_chunk(COUNT_UNROLL * cu + k, accs)
            return accs

        n_main = nchunks // COUNT_UNROLL
        accs = lax.fori_loop(0, n_main, unrolled, (jnp.zeros((SUBLANES, tq), jnp.int32),) * COUNT_ACCS)
        accs = lax.fori_loop(n_main * COUNT_UNROLL, nchunks, count_chunk, accs)
        acc = (accs[0] + accs[1]) + (accs[2] + accs[3])
        for shift in (4, 2, 1):
            acc = acc + pltpu.roll(acc, shift, axis=0)
        return acc

    def search_body(p, carry):
        u, n_u = carry
        trial = u | jnp.left_shift(jnp.int32(1), 31 - p)
        cand = _key_to_float(trial ^ INT_MIN)
        cnt = count_rows(lambda sc, off: sc >= cand[None])
        take = cnt >= topk
        return jnp.where(take, trial, u), jnp.where(take, cnt, n_u)

    u, n_ge = lax.fori_loop(0, 32, search_body,
                            (jnp.zeros((SUBLANES, tq), jnp.int32),
                             jnp.zeros((SUBLANES, tq), jnp.int32) + nchunks * tk))
    thr = _key_to_float(u ^ INT_MIN)

    n_gt = count_rows(lambda sc, off: sc > thr[None])
    want = topk - n_gt
    has_excess = jnp.max(n_ge - n_gt - want) > 0

    def tie_search():
        def body(p, jp):
            trial = jp + jnp.left_shift(jnp.int32(1), (seq - 1).bit_length() - 1 - p)
            cnt = count_rows(lambda sc, off: (sc == thr[None]) & (off + row_iota3 <= trial[None]))
            return jnp.where(cnt < want, trial, jp)
        jp = lax.fori_loop(0, (seq - 1).bit_length(), body, jnp.full((SUBLANES, tq), -1, jnp.int32))
        return jp + 1

    jlim = lax.cond(has_excess, tie_search, lambda: jnp.full((SUBLANES, tq), seq, jnp.int32))

    def bias_body(c, carry):
        off = chunk_off(c)
        sc = _fold_rows(score_ref[pl.ds(off, tk), :])
        kpos = off + row_iota3
        sel = ((sc > thr[None]) | ((sc == thr[None]) & (kpos <= jlim[None]))) & (kpos <= qpos3)
        score_ref[pl.ds(off, tk), :] = jnp.where(sel, 0.0, neg_inf).reshape(tk, tq)
        return carry

    lax.fori_loop(0, nchunks, bias_body, 0)

    q_all = qt_ref[0, 0]
    zeros_q = jnp.zeros((HEAD_DIM, tq), BF16)
    n_pairs = ATT_HEADS // 2
    for j in range(n_pairs):
        r0 = 2 * j * HEAD_DIM
        wbd_ref[j] = jnp.concatenate([
            jnp.concatenate([q_all[r0:r0 + HEAD_DIM], zeros_q], axis=1),
            jnp.concatenate([zeros_q, q_all[r0 + HEAD_DIM:r0 + 2 * HEAD_DIM]], axis=1)], axis=0)
    acc_ref[...] = jnp.zeros(acc_ref.shape, F32)

    def attn_body(c, m_all):
        off = chunk_off(c)
        bias = score_ref[pl.ds(off, tk), :]
        bias2 = jnp.concatenate([bias, bias], axis=1)
        m_next = []
        updates = []

        def pair_logits(j):
            kc = kn_ref[0, pl.ds(off, tk), 2 * j * HEAD_DIM:2 * (j + 1) * HEAD_DIM]
            return jnp.dot(kc, wbd_ref[j], preferred_element_type=F32)

        lg_ahead = pair_logits(0)
        for j in range(n_pairs):
            lg = lg_ahead + bias2
            m_old = m_all[:, 2 * j * tq:2 * (j + 1) * tq]
            if j + 1 < n_pairs:
                lg_ahead = pair_logits(j + 1)
                m_old = jnp.maximum(m_old, jnp.minimum(lg_ahead[0:1, :], neg_inf))
            m_new = jnp.maximum(m_old, jnp.max(jnp.max(_fold_rows(lg), axis=0), axis=0, keepdims=True))
            m_use = jnp.where(m_new == neg_inf, 0.0, m_new)
            alpha = jnp.exp2(m_old - m_use)
            p = jnp.exp2(lg - m_use).astype(BF16)
            for hh in range(2):
                h = 2 * j + hh
                pv = jnp.dot(vt_ref[0, c, h * V_ROWS:(h + 1) * V_ROWS, :], p[:, hh * tq:(hh + 1) * tq],
                             preferred_element_type=F32)
                updates.append((h, alpha[:, hh * tq:(hh + 1) * tq], pv))
            m_next.append(m_new)
        for h, a, pv in updates:
            acc_ref[h] = acc_ref[h] * a + pv
        return jnp.concatenate(m_next, axis=1)

    lax.fori_loop(0, nchunks, attn_body, jnp.full((1, ATT_HEADS * tq), neg_inf, F32))
    for h in range(ATT_HEADS):
        o = acc_ref[h]
        at_ref[h * HEAD_DIM:(h + 1) * HEAD_DIM, :] = o[:HEAD_DIM] / o[HEAD_DIM:HEAD_DIM + 1]

    a_t = at_ref[...]
    ms = jnp.mean(a_t * a_t, axis=0, keepdims=True)
    y = (a_t * lax.rsqrt(ms + NORM_EPS)).T
    out_ref[0] = (y * g_ref[...]).astype(BF16)


def _dsa_call(kn, kin, vt, qt, qit, w_t, g_attn, topk):
    bsz, seq, _ = kn.shape
    nt, tq = qt.shape[1], qt.shape[3]
    tile4 = lambda b, i: (b, i, 0, 0)
    return pl.pallas_call(
        functools.partial(_dsa_kernel, topk=topk, seq=seq),
        grid=(bsz, nt),
        in_specs=[
            pl.BlockSpec((1, seq, ATT_WIDTH), lambda b, i: (b, 0, 0)),
            pl.BlockSpec((1, seq, IDX_DIM), lambda b, i: (b, 0, 0)),
            pl.BlockSpec((1, nt, ATT_HEADS * V_ROWS, tq), lambda b, i: (b, 0, 0, 0)),
            pl.BlockSpec((1, 1, ATT_WIDTH, tq), tile4),
            pl.BlockSpec((1, 1, IDX_HEADS * IDX_DIM, tq), tile4),
            pl.BlockSpec((1, 1, IDX_HEADS, tq), tile4),
            pl.BlockSpec((1, ATT_WIDTH), lambda b, i: (0, 0)),
        ],
        out_specs=pl.BlockSpec((1, tq, ATT_WIDTH), lambda b, i: (b, i, 0)),
        out_shape=jax.ShapeDtypeStruct((bsz, seq, ATT_WIDTH), BF16),
        scratch_shapes=[
            pltpu.VMEM((seq, tq), F32),
            pltpu.VMEM((ATT_HEADS // 2, 2 * HEAD_DIM, 2 * tq), BF16),
            pltpu.VMEM((ATT_HEADS, V_ROWS, tq), F32),
            pltpu.VMEM((ATT_WIDTH, tq), F32),
        ],
        compiler_params=pltpu.CompilerParams(
            dimension_semantics=("arbitrary", "arbitrary"), vmem_limit_bytes=VMEM_LIMIT),
        name="dsa_attention",
    )(kn, kin, vt, qt, qit, w_t, g_attn)


def _layer_norm(y, g, b):
    mu = jnp.mean(y, axis=-1, keepdims=True)
    d = y - mu
    var = jnp.mean(d * d, axis=-1, keepdims=True)
    return d * lax.rsqrt(var + NORM_EPS) * g + b


def _ffn_kernel(x_ref, am_ref, cm_ref, wout_ref, g1_ref, b1_ref, wgu_ref, wd_ref, g2_ref, b2_ref,
                o_ref, *, alpha, d_ff):
    x = x_ref[...]
    mix = (jnp.dot(am_ref[...], wout_ref[0:ATT_WIDTH, :], preferred_element_type=F32)
           + jnp.dot(cm_ref[...], wout_ref[ATT_WIDTH:D_MODEL, :], preferred_element_type=F32))
    x1 = _layer_norm(alpha * x + mix, g1_ref[...], b1_ref[...])
    gu = jnp.dot(x1.astype(BF16), wgu_ref[...], preferred_element_type=F32)
    gate = gu[:, :d_ff]
    up = gu[:, d_ff:]
    hidden = (gate * jax.nn.sigmoid(gate) * up).astype(BF16)
    ffn = jnp.dot(hidden, wd_ref[...], preferred_element_type=F32)
    o_ref[...] = _layer_norm(alpha * x1 + ffn, g2_ref[...], b2_ref[...])


def _ffn_call(x2d, am, cm, wout, g1, b1, wgu, wd, g2, b2, alpha):
    rows = x2d.shape[0]
    tm = ROW_TILE
    d_ff = wd.shape[0]
    row = lambda r: (r, 0)
    const = lambda r: (0, 0)
    resident = functools.partial(pl.BlockSpec, index_map=const, pipeline_mode=pl.Buffered(1))
    return pl.pallas_call(
        functools.partial(_ffn_kernel, alpha=alpha, d_ff=d_ff),
        grid=(rows // tm,),
        in_specs=[
            pl.BlockSpec((tm, D_MODEL), row),
            pl.BlockSpec((tm, ATT_WIDTH), row),
            pl.BlockSpec((tm, CONV_WIDTH), row),
            resident((D_MODEL, D_MODEL)),
            resident((1, D_MODEL)),
            resident((1, D_MODEL)),
            resident((D_MODEL, 2 * d_ff)),
            resident((d_ff, D_MODEL)),
            resident((1, D_MODEL)),
            resident((1, D_MODEL)),
        ],
        out_specs=pl.BlockSpec((tm, D_MODEL), row),
        out_shape=jax.ShapeDtypeStruct((rows, D_MODEL), F32),
        compiler_params=pltpu.CompilerParams(
            dimension_semantics=("arbitrary",), vmem_limit_bytes=VMEM_LIMIT),
        name="outproj_ffn",
    )(x2d, am, cm, wout, g1, b1, wgu, wd, g2, b2)


def _rotate_half_cols(w):
    d_in, n = w.shape
    w4 = w.reshape(d_in, n // HEAD_DIM, 2, HEAD_DIM // 2)
    return jnp.concatenate([-w4[:, :, 1:2, :], w4[:, :, 0:1, :]], axis=2).reshape(d_in, n)


def _split_cols(w, sizes):
    out, o = [], 0
    for s in sizes:
        out.append(w[:, o:o + s])
        o += s
    return out


def kernel(x, w_in, conv_w, mix_norm_g, w_out, ln1_g, ln1_b, w_gate_up, w_down, ln2_g, ln2_b):
    bsz, seq, _ = x.shape
    depth = w_in.shape[0]
    alpha = (2 * depth) ** 0.25
    topk = min(INDEX_TOPK_MAX, seq // 4)
    assert seq % SEQ_TILE == 0 and (bsz * seq) % ROW_TILE == 0 and topk <= SEQ_TILE

    inv_freq = 1.0 / (ROPE_THETA ** (jnp.arange(0, HEAD_DIM, 2, dtype=F32) / HEAD_DIM))
    ang = jnp.arange(seq, dtype=F32)[:, None] * inv_freq[None, :]
    cos, sin = jnp.cos(ang), jnp.sin(ang)
    cos_n, sin_n = jnp.tile(cos, (1, LANES // cos.shape[1])), jnp.tile(sin, (1, LANES // sin.shape[1]))
    cos_t, sin_t = cos.T, sin.T

    for layer in range(depth):
        wq, wk, wv, wqi, wki, wwi, wgb, wgc, wh = _split_cols(w_in[layer], IN_SPLIT_SIZES)
        pad_ki = jnp.zeros((D_MODEL, LANES - IDX_DIM), F32)
        wn = jnp.concatenate([wk, _rotate_half_cols(wk), wgb, wgc, wh,
                              wki, pad_ki, _rotate_half_cols(wki), pad_ki], axis=1).astype(BF16)
        pad_wi = jnp.zeros((D_MODEL, 2 * SUBLANES - IDX_HEADS), F32)
        wt = jnp.concatenate([wq, wqi, wv, wwi, pad_wi], axis=1).T.astype(BF16)
        g = mix_norm_g[layer]
        kn, kin, qt, qit, vt, w_t, cm = _proj_call(
            x, wn, wt, cos_n, sin_n, cos_t, sin_t, conv_w[layer], g[ATT_WIDTH:][None, :])
        am = _dsa_call(kn, kin, vt, qt, qit, w_t, g[:ATT_WIDTH][None, :], topk)
        y = _ffn_call(
            x.reshape(bsz * seq, D_MODEL), am.reshape(bsz * seq, ATT_WIDTH), cm.reshape(bsz * seq, CONV_WIDTH),
            w_out[layer].astype(BF16), ln1_g[layer][None, :], ln1_b[layer][None, :],
            w_gate_up[layer].astype(BF16), w_down[layer].astype(BF16),
            ln2_g[layer][None, :], ln2_b[layer][None, :], alpha)
        x = y.reshape(bsz, seq, D_MODEL)
    return x
```

```python
import functools

import jax
import jax.numpy as jnp
from jax import lax
from jax.experimental import pallas as pl
from jax.experimental.pallas import tpu as pltpu

D_MODEL = 1024
ATT_HEADS = 8
HEAD_DIM = 64
ATT_WIDTH = ATT_HEADS * HEAD_DIM
CONV_WIDTH = D_MODEL - ATT_WIDTH
CONV_K = 3
IDX_HEADS = 8
IDX_DIM = 64
INDEX_TOPK_MAX = 256
ROPE_THETA = 10000.0
NORM_EPS = 1e-5
IN_SPLIT_SIZES = (ATT_WIDTH, ATT_WIDTH, ATT_WIDTH, IDX_HEADS * IDX_DIM, IDX_DIM, IDX_HEADS,
                  CONV_WIDTH, CONV_WIDTH, CONV_WIDTH)

LANES = 128
SUBLANES = 8
SEQ_TILE = 256
ROW_TILE = 256
COUNT_UNROLL = 4
COUNT_ACCS = 4
V_ROWS = 80
LOG2E = 1.4426950408889634
VMEM_LIMIT = 56 * 1024 * 1024

F32 = jnp.float32
BF16 = jnp.bfloat16
INT_MIN = -2 ** 31
KEY_NEG_INF = (0xFF800000 ^ 0x7FFFFFFF) - 2 ** 32
KEY_POS_INF = 0x7F800000


def _proj_kernel(x_ref, wn_ref, wt_ref, cosn_ref, sinn_ref, cost_ref, sint_ref, convw_ref, gconv_ref,
                 kn_ref, kin_ref, qt_ref, qit_ref, vt_ref, wt_out_ref, cm_ref, u_ref, *, w_scale):
    t = pl.program_id(1)
    tm = x_ref.shape[1]
    xb = x_ref[0].astype(BF16)
    nat = jnp.dot(xb, wn_ref[...], preferred_element_type=F32)
    tr = lax.dot_general(wt_ref[...], xb, (((1,), (1,)), ((), ())),
                         preferred_element_type=F32)

    cos_n = cosn_ref[...]
    sin_n = sinn_ref[...]
    cos4 = jnp.concatenate([cos_n] * (ATT_WIDTH // LANES), axis=1)
    sin4 = jnp.concatenate([sin_n] * (ATT_WIDTH // LANES), axis=1)
    k_rope = nat[:, 0:ATT_WIDTH] * cos4 + nat[:, ATT_WIDTH:2 * ATT_WIDTH] * sin4
    kn_ref[0] = k_rope.astype(BF16)
    o_ki = 2 * ATT_WIDTH + 3 * CONV_WIDTH
    ki_rope = nat[:, o_ki:o_ki + LANES] * cos_n + nat[:, o_ki + LANES:o_ki + 2 * LANES] * sin_n
    kin_ref[0] = ki_rope[:, :IDX_DIM].astype(BF16)

    cos_t = cost_ref[...]
    sin_t = sint_ref[...]
    half = HEAD_DIM // 2
    q_scale = (HEAD_DIM ** -0.5) * LOG2E
    for h in range(ATT_HEADS):
        r0 = h * HEAD_DIM
        x1 = tr[r0:r0 + half]
        x2 = tr[r0 + half:r0 + HEAD_DIM]
        qt_ref[0, 0, r0:r0 + half, :] = ((x1 * cos_t - x2 * sin_t) * q_scale).astype(BF16)
        qt_ref[0, 0, r0 + half:r0 + HEAD_DIM, :] = ((x2 * cos_t + x1 * sin_t) * q_scale).astype(BF16)
    for h in range(IDX_HEADS):
        r0 = ATT_WIDTH + h * IDX_DIM
        x1 = tr[r0:r0 + half]
        x2 = tr[r0 + half:r0 + IDX_DIM]
        qit_ref[0, 0, h * IDX_DIM:h * IDX_DIM + half, :] = (x1 * cos_t - x2 * sin_t).astype(BF16)
        qit_ref[0, 0, h * IDX_DIM + half:(h + 1) * IDX_DIM, :] = (x2 * cos_t + x1 * sin_t).astype(BF16)
    o_v = ATT_WIDTH + IDX_HEADS * IDX_DIM
    pad_row = lax.broadcasted_iota(jnp.int32, (V_ROWS - HEAD_DIM, tm), 0)
    ones_pad = jnp.where(pad_row == 0, 1.0, 0.0).astype(BF16)
    for h in range(ATT_HEADS):
        vt_ref[0, 0, h * V_ROWS:h * V_ROWS + HEAD_DIM, :] = (
            tr[o_v + h * HEAD_DIM:o_v + (h + 1) * HEAD_DIM].astype(BF16))
        vt_ref[0, 0, h * V_ROWS + HEAD_DIM:(h + 1) * V_ROWS, :] = ones_pad
    wt_out_ref[0, 0] = tr[o_v + ATT_WIDTH:o_v + ATT_WIDTH + IDX_HEADS] * w_scale

    o_c = 2 * ATT_WIDTH
    gate_b = nat[:, o_c:o_c + CONV_WIDTH]
    gate_c = nat[:, o_c + CONV_WIDTH:o_c + 2 * CONV_WIDTH]
    hh = nat[:, o_c + 2 * CONV_WIDTH:o_c + 3 * CONV_WIDTH]
    u = gate_c * hh

    @pl.when(t == 0)
    def _():
        u_ref[0:SUBLANES, :] = jnp.zeros((SUBLANES, CONV_WIDTH), F32)

    u_ref[SUBLANES:SUBLANES + tm, :] = u
    um1 = u_ref[SUBLANES - 1:SUBLANES - 1 + tm, :]
    um2 = u_ref[SUBLANES - 2:SUBLANES - 2 + tm, :]
    cw = convw_ref[...]
    y = cw[0:1, :] * um2 + cw[1:2, :] * um1 + cw[2:3, :] * u
    u_ref[0:SUBLANES, :] = u[tm - SUBLANES:tm, :]
    conv_out = gate_b * y
    ms = jnp.mean(conv_out * conv_out, axis=-1, keepdims=True)
    cm_ref[0] = (conv_out * lax.rsqrt(ms + NORM_EPS) * gconv_ref[...]).astype(BF16)


def _proj_call(x, wn, wt, cos_n, sin_n, cos_t, sin_t, conv_w, g_conv):
    bsz, seq, _ = x.shape
    tm = SEQ_TILE
    nt = seq // tm
    n_nat = wn.shape[1]
    n_tr = wt.shape[0]
    w_scale = (IDX_DIM ** -0.5) * (IDX_HEADS ** -0.5)
    const2 = lambda b, t: (0, 0)
    out_shapes = (
        jax.ShapeDtypeStruct((bsz, seq, ATT_WIDTH), BF16),
        jax.ShapeDtypeStruct((bsz, seq, IDX_DIM), BF16),
        jax.ShapeDtypeStruct((bsz, nt, ATT_WIDTH, tm), BF16),
        jax.ShapeDtypeStruct((bsz, nt, IDX_HEADS * IDX_DIM, tm), BF16),
        jax.ShapeDtypeStruct((bsz, nt, ATT_HEADS * V_ROWS, tm), BF16),
        jax.ShapeDtypeStruct((bsz, nt, IDX_HEADS, tm), F32),
        jax.ShapeDtypeStruct((bsz, seq, CONV_WIDTH), BF16),
    )
    tile4 = lambda b, t: (b, t, 0, 0)
    return pl.pallas_call(
        functools.partial(_proj_kernel, w_scale=w_scale),
        grid=(bsz, nt),
        in_specs=[
            pl.BlockSpec((1, tm, D_MODEL), lambda b, t: (b, t, 0)),
            pl.BlockSpec((D_MODEL, n_nat), const2),
            pl.BlockSpec((n_tr, D_MODEL), const2),
            pl.BlockSpec((tm, LANES), lambda b, t: (t, 0)),
            pl.BlockSpec((tm, LANES), lambda b, t: (t, 0)),
            pl.BlockSpec((HEAD_DIM // 2, tm), lambda b, t: (0, t)),
            pl.BlockSpec((HEAD_DIM // 2, tm), lambda b, t: (0, t)),
            pl.BlockSpec((CONV_K, CONV_WIDTH), const2),
            pl.BlockSpec((1, CONV_WIDTH), const2),
        ],
        out_specs=(
            pl.BlockSpec((1, tm, ATT_WIDTH), lambda b, t: (b, t, 0)),
            pl.BlockSpec((1, tm, IDX_DIM), lambda b, t: (b, t, 0)),
            pl.BlockSpec((1, 1, ATT_WIDTH, tm), tile4),
            pl.BlockSpec((1, 1, IDX_HEADS * IDX_DIM, tm), tile4),
            pl.BlockSpec((1, 1, ATT_HEADS * V_ROWS, tm), tile4),
            pl.BlockSpec((1, 1, IDX_HEADS, tm), tile4),
            pl.BlockSpec((1, tm, CONV_WIDTH), lambda b, t: (b, t, 0)),
        ),
        out_shape=out_shapes,
        scratch_shapes=[pltpu.VMEM((tm + SUBLANES, CONV_WIDTH), F32)],
        compiler_params=pltpu.CompilerParams(
            dimension_semantics=("arbitrary", "arbitrary"), vmem_limit_bytes=VMEM_LIMIT),
        name="proj_rope_conv",
    )(x, wn, wt, cos_n, sin_n, cos_t, sin_t, conv_w, g_conv)


def _key_to_float(key):
    key = jnp.clip(key, KEY_NEG_INF, KEY_POS_INF)
    bits = jnp.where(key >= 0, key, key ^ 0x7FFFFFFF)
    return lax.bitcast_convert_type(bits, F32)


def _fold_rows(a):
    return a.reshape(a.shape[0] // SUBLANES, SUBLANES, a.shape[1])


def _dsa_kernel(kn_ref, kin_ref, vt_ref, qt_ref, qit_ref, w_ref, g_ref, out_ref,
                score_ref, wbd_ref, acc_ref, at_ref, *, topk, seq):
    i = pl.program_id(1)
    tq = qt_ref.shape[3]
    tk = tq
    nchunks = i + 1
    neg_inf = jnp.float32(-jnp.inf)

    row_iota = lax.broadcasted_iota(jnp.int32, (tk, tq), 0)
    qpos = i * tq + lax.broadcasted_iota(jnp.int32, (tk, tq), 1)

    def chunk_off(c):
        return pl.multiple_of(c * tk, tk)

    qi_all = qit_ref[0, 0]
    w_rows = w_ref[0, 0]

    def score_body(c, carry):
        off = chunk_off(c)
        kic = kin_ref[0, pl.ds(off, tk), :]
        sc = jnp.zeros((tk, tq), F32)
        for h in range(IDX_HEADS):
            d = jnp.dot(kic, qi_all[h * IDX_DIM:(h + 1) * IDX_DIM, :], preferred_element_type=F32)
            sc = sc + jnp.maximum(d, 0.0) * w_rows[h:h + 1, :]
        sc = jnp.where(off + row_iota <= qpos, sc, neg_inf)
        score_ref[pl.ds(off, tk), :] = sc
        return carry

    def score_body2(c2, carry):
        return score_body(2 * c2 + 1, score_body(2 * c2, carry))

    lax.fori_loop(0, nchunks // 2, score_body2, 0)
    lax.fori_loop(2 * (nchunks // 2), nchunks, score_body, 0)

    row_iota3 = (lax.broadcasted_iota(jnp.int32, (tk // SUBLANES, SUBLANES, tq), 0) * SUBLANES
                 + lax.broadcasted_iota(jnp.int32, (tk // SUBLANES, SUBLANES, tq), 1))
    qpos3 = i * tq + lax.broadcasted_iota(jnp.int32, (tk // SUBLANES, SUBLANES, tq), 2)

    def count_rows(pred_fn):
        def count_chunk(c, accs):
            off = chunk_off(c)
            m = pred_fn(_fold_rows(score_ref[pl.ds(off, tk), :]), off).astype(jnp.int32)
            per = m.shape[0] // COUNT_ACCS
            return tuple(acc + jnp.sum(m[g * per:(g + 1) * per], axis=0) for g, acc in enumerate(accs))

        def unrolled(cu, accs):
            for k in range(COUNT_UNROLL):
                accs = count_chunk(COUNT_UNROLL * cu + k, accs)
            return accs

        n_main = nchunks // COUNT_UNROLL
        accs = lax.fori_loop(0, n_main, unrolled, (jnp.zeros((SUBLANES, tq), jnp.int32),) * COUNT_ACCS)
        accs = lax.fori_loop(n_main * COUNT_UNROLL, nchunks, count_chunk, accs)
        acc = (accs[0] + accs[1]) + (accs[2] + accs[3])
        for shift in (4, 2, 1):
            acc = acc + pltpu.roll(acc, shift, axis=0)
        return acc

    def search_body(p, carry):
        u, n_u = carry
        trial = u | jnp.left_shift(jnp.int32(1), 31 - p)
        cand = _key_to_float(trial ^ INT_MIN)
        cnt = count_rows(lambda sc, off: sc >= cand[None])
        take = cnt >= topk
        return jnp.where(take, trial, u), jnp.where(take, cnt, n_u)

    u, n_ge = lax.fori_loop(0, 32, search_body,
                            (jnp.zeros((SUBLANES, tq), jnp.int32),
                             jnp.zeros((SUBLANES, tq), jnp.int32) + nchunks * tk))
    thr = _key_to_float(u ^ INT_MIN)

    n_gt = count_rows(lambda sc, off: sc > thr[None])
    want = topk - n_gt
    has_excess = jnp.max(n_ge - n_gt - want) > 0

    def tie_search():
        def body(p, jp):
            trial = jp + jnp.left_shift(jnp.int32(1), (seq - 1).bit_length() - 1 - p)
            cnt = count_rows(lambda sc, off: (sc == thr[None]) & (off + row_iota3 <= trial[None]))
            return jnp.where(cnt < want, trial, jp)
        jp = lax.fori_loop(0, (seq - 1).bit_length(), body, jnp.full((SUBLANES, tq), -1, jnp.int32))
        return jp + 1

    jlim = lax.cond(has_excess, tie_search, lambda: jnp.full((SUBLANES, tq), seq, jnp.int32))

    def bias_body(c, carry):
        off = chunk_off(c)
        sc = _fold_rows(score_ref[pl.ds(off, tk), :])
        kpos = off + row_iota3
        sel = ((sc > thr[None]) | ((sc == thr[None]) & (kpos <= jlim[None]))) & (kpos <= qpos3)
        score_ref[pl.ds(off, tk), :] = jnp.where(sel, 0.0, neg_inf).reshape(tk, tq)
        return carry

    lax.fori_loop(0, nchunks, bias_body, 0)

    q_all = qt_ref[0, 0]
    zeros_q = jnp.zeros((HEAD_DIM, tq), BF16)
    n_pairs = ATT_HEADS // 2
    for j in range(n_pairs):
        r0 = 2 * j * HEAD_DIM
        wbd_ref[j] = jnp.concatenate([
            jnp.concatenate([q_all[r0:r0 + HEAD_DIM], zeros_q], axis=1),
            jnp.concatenate([zeros_q, q_all[r0 + HEAD_DIM:r0 + 2 * HEAD_DIM]], axis=1)], axis=0)
    acc_ref[...] = jnp.zeros(acc_ref.shape, F32)

    def after(value, anchor):
        return jnp.maximum(value, jnp.minimum(anchor, neg_inf))

    def attn_chunks(chunks, m_all):
        units = [(k, j) for k in range(len(chunks)) for j in range(n_pairs)]
        bias2 = []
        for c in chunks:
            bias = score_ref[pl.ds(chunk_off(c), tk), :]
            bias2.append(jnp.concatenate([bias, bias], axis=1))

        def unit_logits(u):
            k, j = units[u]
            kc = kn_ref[0, pl.ds(chunk_off(chunks[k]), tk), 2 * j * HEAD_DIM:2 * (j + 1) * HEAD_DIM]
            return jnp.dot(kc, wbd_ref[j], preferred_element_type=F32)

        m_run = [m_all[:, 2 * j * tq:2 * (j + 1) * tq] for j in range(n_pairs)]
        lgs_raw = [unit_logits(0)]
        stage_a = []
        for u, (k, j) in enumerate(units):
            lg = lgs_raw[u] + bias2[k]
            m_old = m_run[j]
            if u + 1 < len(units):
                lgs_raw.append(unit_logits(u + 1))
                m_old = after(m_old, lgs_raw[u + 1][0:1, :])
            m_new = jnp.maximum(m_old, jnp.max(jnp.max(_fold_rows(lg), axis=0), axis=0, keepdims=True))
            m_run[j] = m_new
            stage_a.append((lg, m_old, m_new))
        updates = []
        for u, (k, j) in enumerate(units):
            lg, m_old, m_new = stage_a[u]
            m_use = jnp.where(m_new == neg_inf, 0.0, m_new)
            if u + 1 < len(units):
                m_use = after(m_use, stage_a[u + 1][2])
            alpha = jnp.exp2(m_old - m_use)
            p = jnp.exp2(lg - m_use).astype(BF16)
            for hh in range(2):
                h = 2 * j + hh
                pv = jnp.dot(vt_ref[0, chunks[k], h * V_ROWS:(h + 1) * V_ROWS, :], p[:, hh * tq:(hh + 1) * tq],
                             preferred_element_type=F32)
                updates.append((h, alpha[:, hh * tq:(hh + 1) * tq], pv))
        for h, a, pv in updates:
            acc_ref[h] = acc_ref[h] * a + pv
        return jnp.concatenate(m_run, axis=1)

    m_all = lax.fori_loop(0, nchunks // 2, lambda c2, m: attn_chunks([2 * c2, 2 * c2 + 1], m),
                          jnp.full((1, ATT_HEADS * tq), neg_inf, F32))
    lax.fori_loop(2 * (nchunks // 2), nchunks, lambda c, m: attn_chunks([c], m), m_all)
    for h in range(ATT_HEADS):
        o = acc_ref[h]
        at_ref[h * HEAD_DIM:(h + 1) * HEAD_DIM, :] = o[:HEAD_DIM] / o[HEAD_DIM:HEAD_DIM + 1]

    a_t = at_ref[...]
    ms = jnp.mean(a_t * a_t, axis=0, keepdims=True)
    y = (a_t * lax.rsqrt(ms + NORM_EPS)).T
    out_ref[0] = (y * g_ref[...]).astype(BF16)


def _dsa_call(kn, kin, vt, qt, qit, w_t, g_attn, topk):
    bsz, seq, _ = kn.shape
    nt, tq = qt.shape[1], qt.shape[3]
    tile4 = lambda b, i: (b, i, 0, 0)
    return pl.pallas_call(
        functools.partial(_dsa_kernel, topk=topk, seq=seq),
        grid=(bsz, nt),
        in_specs=[
            pl.BlockSpec((1, seq, ATT_WIDTH), lambda b, i: (b, 0, 0)),
            pl.BlockSpec((1, seq, IDX_DIM), lambda b, i: (b, 0, 0)),
            pl.BlockSpec((1, nt, ATT_HEADS * V_ROWS, tq), lambda b, i: (b, 0, 0, 0)),
            pl.BlockSpec((1, 1, ATT_WIDTH, tq), tile4),
            pl.BlockSpec((1, 1, IDX_HEADS * IDX_DIM, tq), tile4),
            pl.BlockSpec((1, 1, IDX_HEADS, tq), tile4),
            pl.BlockSpec((1, ATT_WIDTH), lambda b, i: (0, 0)),
        ],
        out_specs=pl.BlockSpec((1, tq, ATT_WIDTH), lambda b, i: (b, i, 0)),
        out_shape=jax.ShapeDtypeStruct((bsz, seq, ATT_WIDTH), BF16),
        scratch_shapes=[
            pltpu.VMEM((seq, tq), F32),
            pltpu.VMEM((ATT_HEADS // 2, 2 * HEAD_DIM, 2 * tq), BF16),
            pltpu.VMEM((ATT_HEADS, V_ROWS, tq), F32),
            pltpu.VMEM((ATT_WIDTH, tq), F32),
        ],
        compiler_params=pltpu.CompilerParams(
            dimension_semantics=("arbitrary", "arbitrary"), vmem_limit_bytes=VMEM_LIMIT),
        name="dsa_attention",
    )(kn, kin, vt, qt, qit, w_t, g_attn)


def _layer_norm(y, g, b):
    mu = jnp.mean(y, axis=-1, keepdims=True)
    d = y - mu
    var = jnp.mean(d * d, axis=-1, keepdims=True)
    return d * lax.rsqrt(var + NORM_EPS) * g + b


def _ffn_kernel(x_ref, am_ref, cm_ref, wout_ref, g1_ref, b1_ref, wgu_ref, wd_ref, g2_ref, b2_ref,
                o_ref, *, alpha, d_ff):
    x = x_ref[...]
    mix = (jnp.dot(am_ref[...], wout_ref[0:ATT_WIDTH, :], preferred_element_type=F32)
           + jnp.dot(cm_ref[...], wout_ref[ATT_WIDTH:D_MODEL, :], preferred_element_type=F32))
    x1 = _layer_norm(alpha * x + mix, g1_ref[...], b1_ref[...])
    gu = jnp.dot(x1.astype(BF16), wgu_ref[...], preferred_element_type=F32)
    gate = gu[:, :d_ff]
    up = gu[:, d_ff:]
    hidden = (gate * jax.nn.sigmoid(gate) * up).astype(BF16)
    ffn = jnp.dot(hidden, wd_ref[...], preferred_element_type=F32)
    o_ref[...] = _layer_norm(alpha * x1 + ffn, g2_ref[...], b2_ref[...])


def _ffn_call(x2d, am, cm, wout, g1, b1, wgu, wd, g2, b2, alpha):
    rows = x2d.shape[0]
    tm = ROW_TILE
    d_ff = wd.shape[0]
    row = lambda r: (r, 0)
    const = lambda r: (0, 0)
    resident = functools.partial(pl.BlockSpec, index_map=const, pipeline_mode=pl.Buffered(1))
    return pl.pallas_call(
        functools.partial(_ffn_kernel, alpha=alpha, d_ff=d_ff),
        grid=(rows // tm,),
        in_specs=[
            pl.BlockSpec((tm, D_MODEL), row),
            pl.BlockSpec((tm, ATT_WIDTH), row),
            pl.BlockSpec((tm, CONV_WIDTH), row),
            resident((D_MODEL, D_MODEL)),
            resident((1, D_MODEL)),
            resident((1, D_MODEL)),
            resident((D_MODEL, 2 * d_ff)),
            resident((d_ff, D_MODEL)),
            resident((1, D_MODEL)),
            resident((1, D_MODEL)),
        ],
        out_specs=pl.BlockSpec((tm, D_MODEL), row),
        out_shape=jax.ShapeDtypeStruct((rows, D_MODEL), F32),
        compiler_params=pltpu.CompilerParams(
            dimension_semantics=("arbitrary",), vmem_limit_bytes=VMEM_LIMIT),
        name="outproj_ffn",
    )(x2d, am, cm, wout, g1, b1, wgu, wd, g2, b2)


def _rotate_half_cols(w):
    d_in, n = w.shape
    w4 = w.reshape(d_in, n // HEAD_DIM, 2, HEAD_DIM // 2)
    return jnp.concatenate([-w4[:, :, 1:2, :], w4[:, :, 0:1, :]], axis=2).reshape(d_in, n)


def _split_cols(w, sizes):
    out, o = [], 0
    for s in sizes:
        out.append(w[:, o:o + s])
        o += s
    return out


def kernel(x, w_in, conv_w, mix_norm_g, w_out, ln1_g, ln1_b, w_gate_up, w_down, ln2_g, ln2_b):
    bsz, seq, _ = x.shape
    depth = w_in.shape[0]
    alpha = (2 * depth) ** 0.25
    topk = min(INDEX_TOPK_MAX, seq // 4)
    assert seq % SEQ_TILE == 0 and (bsz * seq) % ROW_TILE == 0 and topk <= SEQ_TILE

    inv_freq = 1.0 / (ROPE_THETA ** (jnp.arange(0, HEAD_DIM, 2, dtype=F32) / HEAD_DIM))
    ang = jnp.arange(seq, dtype=F32)[:, None] * inv_freq[None, :]
    cos, sin = jnp.cos(ang), jnp.sin(ang)
    cos_n, sin_n = jnp.tile(cos, (1, LANES // cos.shape[1])), jnp.tile(sin, (1, LANES // sin.shape[1]))
    cos_t, sin_t = cos.T, sin.T

    for layer in range(depth):
        wq, wk, wv, wqi, wki, wwi, wgb, wgc, wh = _split_cols(w_in[layer], IN_SPLIT_SIZES)
        pad_ki = jnp.zeros((D_MODEL, LANES - IDX_DIM), F32)
        wn = jnp.concatenate([wk, _rotate_half_cols(wk), wgb, wgc, wh,
                              wki, pad_ki, _rotate_half_cols(wki), pad_ki], axis=1).astype(BF16)
        pad_wi = jnp.zeros((D_MODEL, 2 * SUBLANES - IDX_HEADS), F32)
        wt = jnp.concatenate([wq, wqi, wv, wwi, pad_wi], axis=1).T.astype(BF16)
        g = mix_norm_g[layer]
        kn, kin, qt, qit, vt, w_t, cm = _proj_call(
            x, wn, wt, cos_n, sin_n, cos_t, sin_t, conv_w[layer], g[ATT_WIDTH:][None, :])
        am = _dsa_call(kn, kin, vt, qt, qit, w_t, g[:ATT_WIDTH][None, :], topk)
        y = _ffn_call(
            x.reshape(bsz * seq, D_MODEL), am.reshape(bsz * seq, ATT_WIDTH), cm.reshape(bsz * seq, CONV_WIDTH),
            w_out[layer].astype(BF16), ln1_g[layer][None, :], ln1_b[layer][None, :],
            w_gate_up[layer].astype(BF16), w_down[layer].astype(BF16),
            ln2_g[layer][None, :], ln2_b[layer][None, :], alpha)
        x = y.reshape(bsz, seq, D_MODEL)
    return x
```

```python
import functools

import jax
import jax.numpy as jnp
from jax import lax
from jax.experimental import pallas as pl
from jax.experimental.pallas import tpu as pltpu

D_MODEL = 1024
ATT_HEADS = 8
HEAD_DIM = 64
ATT_WIDTH = ATT_HEADS * HEAD_DIM
CONV_WIDTH = D_MODEL - ATT_WIDTH
CONV_K = 3
IDX_HEADS = 8
IDX_DIM = 64
INDEX_TOPK_MAX = 256
ROPE_THETA = 10000.0
NORM_EPS = 1e-5
IN_SPLIT_SIZES = (ATT_WIDTH, ATT_WIDTH, ATT_WIDTH, IDX_HEADS * IDX_DIM, IDX_DIM, IDX_HEADS,
                  CONV_WIDTH, CONV_WIDTH, CONV_WIDTH)

LANES = 128
SUBLANES = 8
SEQ_TILE = 256
PROJ_GROUPS = 2
ROW_TILE = 512
FFN_SPLIT = 2
COUNT_UNROLL = 4
COUNT_ACCS = 4
ATTN_UNROLL = 2
V_ROWS = 80
LOG2E = 1.4426950408889634
VMEM_LIMIT = 56 * 1024 * 1024

F32 = jnp.float32
BF16 = jnp.bfloat16
INT_MIN = -2 ** 31
KEY_NEG_INF = (0xFF800000 ^ 0x7FFFFFFF) - 2 ** 32
KEY_POS_INF = 0x7F800000


def _proj_kernel(x_ref, wn_ref, wt_ref, cosn_ref, sinn_ref, cost_ref, sint_ref, convw_ref, gconv_ref,
                 kn_ref, kin_ref, qt_ref, qit_ref, vt_ref, wt_out_ref, cm_ref, u_ref, *, w_scale):
    t = pl.program_id(1)
    tm = qt_ref.shape[3]
    half = HEAD_DIM // 2
    q_scale = (HEAD_DIM ** -0.5) * LOG2E
    o_ki = 2 * ATT_WIDTH + 3 * CONV_WIDTH
    o_v = ATT_WIDTH + IDX_HEADS * IDX_DIM
    o_c = 2 * ATT_WIDTH
    pad_row = lax.broadcasted_iota(jnp.int32, (V_ROWS - HEAD_DIM, tm), 0)
    ones_pad = jnp.where(pad_row == 0, 1.0, 0.0).astype(BF16)
    cw = convw_ref[...]

    @pl.when(t == 0)
    def _():
        u_ref[0:SUBLANES, :] = jnp.zeros((SUBLANES, CONV_WIDTH), F32)

    xbs = [x_ref[0, g * tm:(g + 1) * tm, :].astype(BF16) for g in range(PROJ_GROUPS)]
    nats = [jnp.dot(xb, wn_ref[...], preferred_element_type=F32) for xb in xbs]
    trs = [lax.dot_general(wt_ref[...], xb, (((1,), (1,)), ((), ())), preferred_element_type=F32)
           for xb in xbs]

    for g in range(PROJ_GROUPS):
        nat, tr = nats[g], trs[g]
        rows = slice(g * tm, (g + 1) * tm)

        cos_n = cosn_ref[rows, :]
        sin_n = sinn_ref[rows, :]
        cos4 = jnp.concatenate([cos_n] * (ATT_WIDTH // LANES), axis=1)
        sin4 = jnp.concatenate([sin_n] * (ATT_WIDTH // LANES), axis=1)
        k_rope = nat[:, 0:ATT_WIDTH] * cos4 + nat[:, ATT_WIDTH:2 * ATT_WIDTH] * sin4
        kn_ref[0, rows, :] = k_rope.astype(BF16)
        ki_rope = nat[:, o_ki:o_ki + LANES] * cos_n + nat[:, o_ki + LANES:o_ki + 2 * LANES] * sin_n
        kin_ref[0, rows, :] = ki_rope[:, :IDX_DIM].astype(BF16)

        cos_t = cost_ref[:, rows]
        sin_t = sint_ref[:, rows]
        for h in range(ATT_HEADS):
            r0 = h * HEAD_DIM
            x1 = tr[r0:r0 + half]
            x2 = tr[r0 + half:r0 + HEAD_DIM]
            qt_ref[0, g, r0:r0 + half, :] = ((x1 * cos_t - x2 * sin_t) * q_scale).astype(BF16)
            qt_ref[0, g, r0 + half:r0 + HEAD_DIM, :] = ((x2 * cos_t + x1 * sin_t) * q_scale).astype(BF16)
        for h in range(IDX_HEADS):
            r0 = ATT_WIDTH + h * IDX_DIM
            x1 = tr[r0:r0 + half]
            x2 = tr[r0 + half:r0 + IDX_DIM]
            qit_ref[0, g, h * IDX_DIM:h * IDX_DIM + half, :] = (x1 * cos_t - x2 * sin_t).astype(BF16)
            qit_ref[0, g, h * IDX_DIM + half:(h + 1) * IDX_DIM, :] = (x2 * cos_t + x1 * sin_t).astype(BF16)
        for h in range(ATT_HEADS):
            vt_ref[0, g, h * V_ROWS:h * V_ROWS + HEAD_DIM, :] = (
                tr[o_v + h * HEAD_DIM:o_v + (h + 1) * HEAD_DIM].astype(BF16))
            vt_ref[0, g, h * V_ROWS + HEAD_DIM:(h + 1) * V_ROWS, :] = ones_pad
        wt_out_ref[0, g] = tr[o_v + ATT_WIDTH:o_v + ATT_WIDTH + IDX_HEADS] * w_scale

        gate_b = nat[:, o_c:o_c + CONV_WIDTH]
        u = nat[:, o_c + CONV_WIDTH:o_c + 2 * CONV_WIDTH] * nat[:, o_c + 2 * CONV_WIDTH:o_c + 3 * CONV_WIDTH]
        base = SUBLANES + g * tm
        u_ref[base:base + tm, :] = u
        um1 = u_ref[base - 1:base - 1 + tm, :]
        um2 = u_ref[base - 2:base - 2 + tm, :]
        conv_out = gate_b * (cw[0:1, :] * um2 + cw[1:2, :] * um1 + cw[2:3, :] * u)
        ms = jnp.mean(conv_out * conv_out, axis=-1, keepdims=True)
        cm_ref[0, rows, :] = (conv_out * lax.rsqrt(ms + NORM_EPS) * gconv_ref[...]).astype(BF16)

    tail = SUBLANES + PROJ_GROUPS * tm
    u_ref[0:SUBLANES, :] = u_ref[tail - SUBLANES:tail, :]


def _proj_call(x, wn, wt, cos_n, sin_n, cos_t, sin_t, conv_w, g_conv):
    bsz, seq, _ = x.shape
    tm = SEQ_TILE
    nt = seq // tm
    gm = PROJ_GROUPS * tm
    n_nat = wn.shape[1]
    n_tr = wt.shape[0]
    w_scale = (IDX_DIM ** -0.5) * (IDX_HEADS ** -0.5)
    const2 = lambda b, t: (0, 0)
    out_shapes = (
        jax.ShapeDtypeStruct((bsz, seq, ATT_WIDTH), BF16),
        jax.ShapeDtypeStruct((bsz, seq, IDX_DIM), BF16),
        jax.ShapeDtypeStruct((bsz, nt, ATT_WIDTH, tm), BF16),
        jax.ShapeDtypeStruct((bsz, nt, IDX_HEADS * IDX_DIM, tm), BF16),
        jax.ShapeDtypeStruct((bsz, nt, ATT_HEADS * V_ROWS, tm), BF16),
        jax.ShapeDtypeStruct((bsz, nt, IDX_HEADS, tm), F32),
        jax.ShapeDtypeStruct((bsz, seq, CONV_WIDTH), BF16),
    )
    tile4 = lambda b, t: (b, t, 0, 0)
    return pl.pallas_call(
        functools.partial(_proj_kernel, w_scale=w_scale),
        grid=(bsz, nt // PROJ_GROUPS),
        in_specs=[
            pl.BlockSpec((1, gm, D_MODEL), lambda b, t: (b, t, 0)),
            pl.BlockSpec((D_MODEL, n_nat), const2),
            pl.BlockSpec((n_tr, D_MODEL), const2),
            pl.BlockSpec((gm, LANES), lambda b, t: (t, 0)),
            pl.BlockSpec((gm, LANES), lambda b, t: (t, 0)),
            pl.BlockSpec((HEAD_DIM // 2, gm), lambda b, t: (0, t)),
            pl.BlockSpec((HEAD_DIM // 2, gm), lambda b, t: (0, t)),
            pl.BlockSpec((CONV_K, CONV_WIDTH), const2),
            pl.BlockSpec((1, CONV_WIDTH), const2),
        ],
        out_specs=(
            pl.BlockSpec((1, gm, ATT_WIDTH), lambda b, t: (b, t, 0)),
            pl.BlockSpec((1, gm, IDX_DIM), lambda b, t: (b, t, 0)),
            pl.BlockSpec((1, PROJ_GROUPS, ATT_WIDTH, tm), tile4),
            pl.BlockSpec((1, PROJ_GROUPS, IDX_HEADS * IDX_DIM, tm), tile4),
            pl.BlockSpec((1, PROJ_GROUPS, ATT_HEADS * V_ROWS, tm), tile4),
            pl.BlockSpec((1, PROJ_GROUPS, IDX_HEADS, tm), tile4),
            pl.BlockSpec((1, gm, CONV_WIDTH), lambda b, t: (b, t, 0)),
        ),
        out_shape=out_shapes,
        scratch_shapes=[pltpu.VMEM((gm + SUBLANES, CONV_WIDTH), F32)],
        compiler_params=pltpu.CompilerParams(
            dimension_semantics=("arbitrary", "arbitrary"), vmem_limit_bytes=VMEM_LIMIT),
        name="proj_rope_conv",
    )(x, wn, wt, cos_n, sin_n, cos_t, sin_t, conv_w, g_conv)


def _key_to_float(key):
    key = jnp.clip(key, KEY_NEG_INF, KEY_POS_INF)
    bits = jnp.where(key >= 0, key, key ^ 0x7FFFFFFF)
    return lax.bitcast_convert_type(bits, F32)


def _after(value, anchor):
    return jnp.maximum(value, jnp.minimum(anchor, -jnp.inf))


def _fold_rows(a):
    return a.reshape(a.shape[0] // SUBLANES, SUBLANES, a.shape[1])


def _dsa_kernel(kn_ref, kin_ref, vt_ref, qt_ref, qit_ref, w_ref, g_ref, out_ref,
                score_ref, wbd_ref, acc_ref, at_ref, lg_ref, *, topk, seq):
    i = pl.program_id(1)
    tq = qt_ref.shape[3]
    tk = tq
    nchunks = i + 1
    neg_inf = jnp.float32(-jnp.inf)

    row_iota = lax.broadcasted_iota(jnp.int32, (tk, tq), 0)
    qpos = i * tq + lax.broadcasted_iota(jnp.int32, (tk, tq), 1)

    def chunk_off(c):
        return pl.multiple_of(c * tk, tk)

    qi_all = qit_ref[0, 0]
    w_rows = w_ref[0, 0]

    def score_body(c, carry):
        off = chunk_off(c)
        kic = kin_ref[0, pl.ds(off, tk), :]
        sc = jnp.zeros((tk, tq), F32)
        for h in range(IDX_HEADS):
            d = jnp.dot(kic, qi_all[h * IDX_DIM:(h + 1) * IDX_DIM, :], preferred_element_type=F32)
            sc = sc + jnp.maximum(d, 0.0) * w_rows[h:h + 1, :]
        sc = jnp.where(off + row_iota <= qpos, sc, neg_inf)
        score_ref[pl.ds(off, tk), :] = sc
        return carry

    def score_body2(c2, carry):
        return score_body(2 * c2 + 1, score_body(2 * c2, carry))

    lax.fori_loop(0, nchunks // 2, score_body2, 0)
    lax.fori_loop(2 * (nchunks // 2), nchunks, score_body, 0)

    row_iota3 = (lax.broadcasted_iota(jnp.int32, (tk // SUBLANES, SUBLANES, tq), 0) * SUBLANES
                 + lax.broadcasted_iota(jnp.int32, (tk // SUBLANES, SUBLANES, tq), 1))
    qpos3 = i * tq + lax.broadcasted_iota(jnp.int32, (tk // SUBLANES, SUBLANES, tq), 2)

    def count_rows(pred_fn):
        def count_chunk(c, accs):
            off = chunk_off(c)
            m = pred_fn(_fold_rows(score_ref[pl.ds(off, tk), :]), off).astype(jnp.int32)
            per = m.shape[0] // COUNT_ACCS
            return tuple(acc + jnp.sum(m[g * per:(g + 1) * per], axis=0) for g, acc in enumerate(accs))

        def unrolled(cu, accs):
            for k in range(COUNT_UNROLL):
                accs = count_chunk(COUNT_UNROLL * cu + k, accs)
            return accs

        n_main = nchunks // COUNT_UNROLL
        accs = lax.fori_loop(0, n_main, unrolled, (jnp.zeros((SUBLANES, tq), jnp.int32),) * COUNT_ACCS)
        accs = lax.fori_loop(n_main * COUNT_UNROLL, nchunks, count_chunk, accs)
        acc = (accs[0] + accs[1]) + (accs[2] + accs[3])
        for shift in (4, 2, 1):
            acc = acc + pltpu.roll(acc, shift, axis=0)
        return acc

    def search_body(p, carry):
        u, n_u = carry
        trial = u | jnp.left_shift(jnp.int32(1), 31 - p)
        cand = _key_to_float(trial ^ INT_MIN)
        cnt = count_rows(lambda sc, off: sc >= cand[None])
        take = cnt >= topk
        return jnp.where(take, trial, u), jnp.where(take, cnt, n_u)

    u, n_ge = lax.fori_loop(0, 32, search_body,
                            (jnp.zeros((SUBLANES, tq), jnp.int32),
                             jnp.zeros((SUBLANES, tq), jnp.int32) + nchunks * tk))
    thr = _key_to_float(u ^ INT_MIN)

    n_gt = count_rows(lambda sc, off: sc > thr[None])
    want = topk - n_gt
    has_excess = jnp.max(n_ge - n_gt - want) > 0

    def resolve_ties():
        def body(p, jp):
            trial = jp + jnp.left_shift(jnp.int32(1), (seq - 1).bit_length() - 1 - p)
            cnt = count_rows(lambda sc, off: (sc == thr[None]) & (off + row_iota3 <= trial[None]))
            return jnp.where(cnt < want, trial, jp)
        jlim = lax.fori_loop(0, (seq - 1).bit_length(), body, jnp.full((SUBLANES, tq), -1, jnp.int32)) + 1

        def rewrite(c, carry):
            off = chunk_off(c)
            sc = _fold_rows(score_ref[pl.ds(off, tk), :])
            sel = (sc > thr[None]) | ((sc == thr[None]) & (off + row_iota3 <= jlim[None]))
            score_ref[pl.ds(off, tk), :] = jnp.where(sel, 1.0, -1.0).reshape(tk, tq)
            return carry

        lax.fori_loop(0, nchunks, rewrite, 0)
        return jnp.zeros((SUBLANES, tq), F32)

    thr_sel = lax.cond(has_excess, resolve_ties, lambda: thr)

    q_all = qt_ref[0, 0]
    zeros_q = jnp.zeros((HEAD_DIM, tq), BF16)
    n_pairs = ATT_HEADS // 2
    for j in range(n_pairs):
        r0 = 2 * j * HEAD_DIM
        wbd_ref[j] = jnp.concatenate([
            jnp.concatenate([q_all[r0:r0 + HEAD_DIM], zeros_q], axis=1),
            jnp.concatenate([zeros_q, q_all[r0 + HEAD_DIM:r0 + 2 * HEAD_DIM]], axis=1)], axis=0)
    acc_ref[...] = jnp.zeros(acc_ref.shape, F32)

    def attn_chunks(chunks, m_all):
        units = [(k, j) for k in range(len(chunks)) for j in range(n_pairs)]
        bias2 = []
        for c in chunks:
            off = chunk_off(c)
            sel = (_fold_rows(score_ref[pl.ds(off, tk), :]) >= thr_sel[None]) & (off + row_iota3 <= qpos3)
            bias = jnp.where(sel, 0.0, neg_inf).reshape(tk, tq)
            bias2.append(jnp.concatenate([bias, bias], axis=1))

        def unit_logits(u):
            k, j = units[u]
            kc = kn_ref[0, pl.ds(chunk_off(chunks[k]), tk), 2 * j * HEAD_DIM:2 * (j + 1) * HEAD_DIM]
            return jnp.dot(kc, wbd_ref[j], preferred_element_type=F32)

        m_run = [m_all[:, 2 * j * tq:2 * (j + 1) * tq] for j in range(n_pairs)]
        lgs_raw = [unit_logits(0)]
        stage_a = []
        for u, (k, j) in enumerate(units):
            lg = lgs_raw[u] + bias2[k]
            m_old = m_run[j]
            if u + 1 < len(units):
                lgs_raw.append(unit_logits(u + 1))
                m_old = _after(m_old, lgs_raw[u + 1][0:1, :])
            m_new = jnp.maximum(m_old, jnp.max(jnp.max(_fold_rows(lg), axis=0), axis=0, keepdims=True))
            m_run[j] = m_new
            lg_ref[u] = lg
            stage_a.append((m_old, m_new))
        updates = []
        for u, (k, j) in enumerate(units):
            m_old, m_new = stage_a[u]
            m_use = jnp.where(m_new == neg_inf, 0.0, m_new)
            if u + 1 < len(units):
                m_use = _after(m_use, stage_a[u + 1][1])
            alpha = jnp.exp2(m_old - m_use)
            p = jnp.exp2(lg_ref[u] - m_use).astype(BF16)
            for hh in range(2):
                h = 2 * j + hh
                pv = jnp.dot(vt_ref[0, chunks[k], h * V_ROWS:(h + 1) * V_ROWS, :], p[:, hh * tq:(hh + 1) * tq],
                             preferred_element_type=F32)
                updates.append((h, alpha[:, hh * tq:(hh + 1) * tq], pv))
        for h, a, pv in updates:
            acc_ref[h] = acc_ref[h] * a + pv
        return jnp.concatenate(m_run, axis=1)

    n_main = nchunks // ATTN_UNROLL
    m_all = lax.fori_loop(0, n_main,
                          lambda cu, m: attn_chunks([ATTN_UNROLL * cu + k for k in range(ATTN_UNROLL)], m),
                          jnp.full((1, ATT_HEADS * tq), neg_inf, F32))
    lax.fori_loop(ATTN_UNROLL * n_main, nchunks, lambda c, m: attn_chunks([c], m), m_all)
    for h in range(ATT_HEADS):
        o = acc_ref[h]
        at_ref[h * HEAD_DIM:(h + 1) * HEAD_DIM, :] = o[:HEAD_DIM] / o[HEAD_DIM:HEAD_DIM + 1]

    a_t = at_ref[...]
    ms = jnp.mean(a_t * a_t, axis=0, keepdims=True)
    y = (a_t * lax.rsqrt(ms + NORM_EPS)).T
    out_ref[0] = (y * g_ref[...]).astype(BF16)


def _dsa_call(kn, kin, vt, qt, qit, w_t, g_attn, topk):
    bsz, seq, _ = kn.shape
    nt, tq = qt.shape[1], qt.shape[3]
    tile4 = lambda b, i: (b, i, 0, 0)
    return pl.pallas_call(
        functools.partial(_dsa_kernel, topk=topk, seq=seq),
        grid=(bsz, nt),
        in_specs=[
            pl.BlockSpec((1, seq, ATT_WIDTH), lambda b, i: (b, 0, 0)),
            pl.BlockSpec((1, seq, IDX_DIM), lambda b, i: (b, 0, 0)),
            pl.BlockSpec((1, nt, ATT_HEADS * V_ROWS, tq), lambda b, i: (b, 0, 0, 0)),
            pl.BlockSpec((1, 1, ATT_WIDTH, tq), tile4),
            pl.BlockSpec((1, 1, IDX_HEADS * IDX_DIM, tq), tile4),
            pl.BlockSpec((1, 1, IDX_HEADS, tq), tile4),
            pl.BlockSpec((1, ATT_WIDTH), lambda b, i: (0, 0)),
        ],
        out_specs=pl.BlockSpec((1, tq, ATT_WIDTH), lambda b, i: (b, i, 0)),
        out_shape=jax.ShapeDtypeStruct((bsz, seq, ATT_WIDTH), BF16),
        scratch_shapes=[
            pltpu.VMEM((seq, tq), F32),
            pltpu.VMEM((ATT_HEADS // 2, 2 * HEAD_DIM, 2 * tq), BF16),
            pltpu.VMEM((ATT_HEADS, V_ROWS, tq), F32),
            pltpu.VMEM((ATT_WIDTH, tq), F32),
            pltpu.VMEM((ATTN_UNROLL * (ATT_HEADS // 2), tq, 2 * tq), F32),
        ],
        compiler_params=pltpu.CompilerParams(
            dimension_semantics=("arbitrary", "arbitrary"), vmem_limit_bytes=VMEM_LIMIT),
        name="dsa_attention",
    )(kn, kin, vt, qt, qit, w_t, g_attn)


def _layer_norm(y, g, b):
    mu = jnp.mean(y, axis=-1, keepdims=True)
    d = y - mu
    var = jnp.mean(d * d, axis=-1, keepdims=True)
    return d * lax.rsqrt(var + NORM_EPS) * g + b


def _ffn_kernel(x_ref, am_ref, cm_ref, wout_ref, g1_ref, b1_ref, wgu_ref, wd_ref, g2_ref, b2_ref,
                o_ref, *, alpha, d_ff):
    n = FFN_SPLIT
    group = x_ref.shape[0] // n
    rows = [slice(r * group, (r + 1) * group) for r in range(n)]
    mix = [jnp.dot(am_ref[rw, :], wout_ref[0:ATT_WIDTH, :], preferred_element_type=F32)
           + jnp.dot(cm_ref[rw, :], wout_ref[ATT_WIDTH:D_MODEL, :], preferred_element_type=F32) for rw in rows]
    x1 = []
    for r in range(n):
        b1 = b1_ref[...] if r + 1 == n else _after(b1_ref[...], mix[r + 1][0:1, :])
        x1.append(_layer_norm(alpha * x_ref[rows[r], :] + mix[r], g1_ref[...], b1))
    gu = [jnp.dot(x1[r].astype(BF16), wgu_ref[...], preferred_element_type=F32) for r in range(n)]
    ffn = []
    for r in range(n):
        gate = gu[r][:, :d_ff]
        hidden = (gate * jax.nn.sigmoid(gate) * gu[r][:, d_ff:]).astype(BF16)
        ffn.append(jnp.dot(hidden, wd_ref[...], preferred_element_type=F32))
    for r in range(n):
        b2 = b2_ref[...] if r + 1 == n else _after(b2_ref[...], ffn[r + 1][0:1, :])
        o_ref[rows[r], :] = _layer_norm(alpha * x1[r] + ffn[r], g2_ref[...], b2)


def _ffn_call(x2d, am, cm, wout, g1, b1, wgu, wd, g2, b2, alpha):
    rows = x2d.shape[0]
    tm = ROW_TILE
    d_ff = wd.shape[0]
    row = lambda r: (r, 0)
    const = lambda r: (0, 0)
    resident = functools.partial(pl.BlockSpec, index_map=const, pipeline_mode=pl.Buffered(1))
    return pl.pallas_call(
        functools.partial(_ffn_kernel, alpha=alpha, d_ff=d_ff),
        grid=(rows // tm,),
        in_specs=[
            pl.BlockSpec((tm, D_MODEL), row),
            pl.BlockSpec((tm, ATT_WIDTH), row),
            pl.BlockSpec((tm, CONV_WIDTH), row),
            resident((D_MODEL, D_MODEL)),
            resident((1, D_MODEL)),
            resident((1, D_MODEL)),
            resident((D_MODEL, 2 * d_ff)),
            resident((d_ff, D_MODEL)),
            resident((1, D_MODEL)),
            resident((1, D_MODEL)),
        ],
        out_specs=pl.BlockSpec((tm, D_MODEL), row),
        out_shape=jax.ShapeDtypeStruct((rows, D_MODEL), F32),
        compiler_params=pltpu.CompilerParams(
            dimension_semantics=("arbitrary",), vmem_limit_bytes=VMEM_LIMIT),
        name="outproj_ffn",
    )(x2d, am, cm, wout, g1, b1, wgu, wd, g2, b2)


def _rotate_half_cols(w):
    d_in, n = w.shape
    w4 = w.reshape(d_in, n // HEAD_DIM, 2, HEAD_DIM // 2)
    return jnp.concatenate([-w4[:, :, 1:2, :], w4[:, :, 0:1, :]], axis=2).reshape(d_in, n)


def _split_cols(w, sizes):
    out, o = [], 0
    for s in sizes:
        out.append(w[:, o:o + s])
        o += s
    return out


def kernel(x, w_in, conv_w, mix_norm_g, w_out, ln1_g, ln1_b, w_gate_up, w_down, ln2_g, ln2_b):
    bsz, seq, _ = x.shape
    depth = w_in.shape[0]
    alpha = (2 * depth) ** 0.25
    topk = min(INDEX_TOPK_MAX, seq // 4)
    assert seq % (SEQ_TILE * PROJ_GROUPS) == 0 and (bsz * seq) % ROW_TILE == 0 and topk <= SEQ_TILE

    inv_freq = 1.0 / (ROPE_THETA ** (jnp.arange(0, HEAD_DIM, 2, dtype=F32) / HEAD_DIM))
    ang = jnp.arange(seq, dtype=F32)[:, None] * inv_freq[None, :]
    cos, sin = jnp.cos(ang), jnp.sin(ang)
    cos_n, sin_n = jnp.tile(cos, (1, LANES // cos.shape[1])), jnp.tile(sin, (1, LANES // sin.shape[1]))
    cos_t, sin_t = cos.T, sin.T

    for layer in range(depth):
        wq, wk, wv, wqi, wki, wwi, wgb, wgc, wh = _split_cols(w_in[layer], IN_SPLIT_SIZES)
        pad_ki = jnp.zeros((D_MODEL, LANES - IDX_DIM), F32)
        wn = jnp.concatenate([wk, _rotate_half_cols(wk), wgb, wgc, wh,
                              wki, pad_ki, _rotate_half_cols(wki), pad_ki], axis=1).astype(BF16)
        pad_wi = jnp.zeros((D_MODEL, 2 * SUBLANES - IDX_HEADS), F32)
        wt = jnp.concatenate([wq, wqi, wv, wwi, pad_wi], axis=1).T.astype(BF16)
        g = mix_norm_g[layer]
        kn, kin, qt, qit, vt, w_t, cm = _proj_call(
            x, wn, wt, cos_n, sin_n, cos_t, sin_t, conv_w[layer], g[ATT_WIDTH:][None, :])
        am = _dsa_call(kn, kin, vt, qt, qit, w_t, g[:ATT_WIDTH][None, :], topk)
        y = _ffn_call(
            x.reshape(bsz * seq, D_MODEL), am.reshape(bsz * seq, ATT_WIDTH), cm.reshape(bsz * seq, CONV_WIDTH),
            w_out[layer].astype(BF16), ln1_g[layer][None, :], ln1_b[layer][None, :],
            w_gate_up[layer].astype(BF16), w_down[layer].astype(BF16),
            ln2_g[layer][None, :], ln2_b[layer][None, :], alpha)
        x = y.reshape(bsz, seq, D_MODEL)
    return x
```

```python
import functools

import jax
import numpy as np
import jax.numpy as jnp
from jax import lax
from jax.experimental import pallas as pl
from jax.experimental.pallas import tpu as pltpu

D_MODEL = 1024
ATT_HEADS = 8
HEAD_DIM = 64
ATT_WIDTH = ATT_HEADS * HEAD_DIM
CONV_WIDTH = D_MODEL - ATT_WIDTH
CONV_K = 3
IDX_HEADS = 8
IDX_DIM = 64
INDEX_TOPK_MAX = 256
ROPE_THETA = 10000.0
NORM_EPS = 1e-5
IN_SPLIT_SIZES = (ATT_WIDTH, ATT_WIDTH, ATT_WIDTH, IDX_HEADS * IDX_DIM, IDX_DIM, IDX_HEADS,
                  CONV_WIDTH, CONV_WIDTH, CONV_WIDTH)

LANES = 128
SUBLANES = 8
SEQ_TILE = 256
PROJ_GROUPS = 2
ROW_TILE = 512
FFN_SPLIT = 2
COUNT_UNROLL = 4
COUNT_ACCS = 4
ATTN_UNROLLS = (4, 2, 1)
V_ROWS = 80
LOG2E = 1.4426950408889634
VMEM_LIMIT = 56 * 1024 * 1024

F32 = jnp.float32
BF16 = jnp.bfloat16
INT_MIN = -2 ** 31
KEY_NEG_INF = (0xFF800000 ^ 0x7FFFFFFF) - 2 ** 32
KEY_POS_INF = 0x7F800000


def _proj_kernel(x_ref, wn_ref, wt_ref, cosn_ref, sinn_ref, cost_ref, sint_ref, convw_ref, gconv_ref,
                 kn_ref, kin_ref, qt_ref, qit_ref, vt_ref, wt_out_ref, cm_ref, u_ref, *, w_scale):
    t = pl.program_id(1)
    tm = qt_ref.shape[3]
    half = HEAD_DIM // 2
    q_scale = (HEAD_DIM ** -0.5) * LOG2E
    o_ki = 2 * ATT_WIDTH + 3 * CONV_WIDTH
    o_v = ATT_WIDTH + IDX_HEADS * IDX_DIM
    o_c = 2 * ATT_WIDTH
    pad_row = lax.broadcasted_iota(jnp.int32, (V_ROWS - HEAD_DIM, tm), 0)
    ones_pad = jnp.where(pad_row == 0, 1.0, 0.0).astype(BF16)
    cw = convw_ref[...]

    @pl.when(t == 0)
    def _():
        u_ref[0:SUBLANES, :] = jnp.zeros((SUBLANES, CONV_WIDTH), F32)

    xbs = [x_ref[0, g * tm:(g + 1) * tm, :].astype(BF16) for g in range(PROJ_GROUPS)]
    nats = [jnp.dot(xb, wn_ref[...], preferred_element_type=F32) for xb in xbs]
    trs = [lax.dot_general(wt_ref[...], xb, (((1,), (1,)), ((), ())), preferred_element_type=F32)
           for xb in xbs]

    for g in range(PROJ_GROUPS):
        nat, tr = nats[g], trs[g]
        rows = slice(g * tm, (g + 1) * tm)

        cos_n = cosn_ref[rows, :]
        sin_n = sinn_ref[rows, :]
        cos4 = jnp.concatenate([cos_n] * (ATT_WIDTH // LANES), axis=1)
        sin4 = jnp.concatenate([sin_n] * (ATT_WIDTH // LANES), axis=1)
        k_rope = nat[:, 0:ATT_WIDTH] * cos4 + nat[:, ATT_WIDTH:2 * ATT_WIDTH] * sin4
        kn_ref[0, rows, :] = k_rope.astype(BF16)
        ki_rope = nat[:, o_ki:o_ki + LANES] * cos_n + nat[:, o_ki + LANES:o_ki + 2 * LANES] * sin_n
        kin_ref[0, rows, :] = ki_rope[:, :IDX_DIM].astype(BF16)

        cos_t = cost_ref[:, rows]
        sin_t = sint_ref[:, rows]
        for h in range(ATT_HEADS):
            r0 = h * HEAD_DIM
            x1 = tr[r0:r0 + half]
            x2 = tr[r0 + half:r0 + HEAD_DIM]
            qt_ref[0, g, r0:r0 + half, :] = ((x1 * cos_t - x2 * sin_t) * q_scale).astype(BF16)
            qt_ref[0, g, r0 + half:r0 + HEAD_DIM, :] = ((x2 * cos_t + x1 * sin_t) * q_scale).astype(BF16)
        for h in range(IDX_HEADS):
            r0 = ATT_WIDTH + h * IDX_DIM
            x1 = tr[r0:r0 + half]
            x2 = tr[r0 + half:r0 + IDX_DIM]
            qit_ref[0, g, h * IDX_DIM:h * IDX_DIM + half, :] = (x1 * cos_t - x2 * sin_t).astype(BF16)
            qit_ref[0, g, h * IDX_DIM + half:(h + 1) * IDX_DIM, :] = (x2 * cos_t + x1 * sin_t).astype(BF16)
        for h in range(ATT_HEADS):
            vt_ref[0, g, h * V_ROWS:h * V_ROWS + HEAD_DIM, :] = (
                tr[o_v + h * HEAD_DIM:o_v + (h + 1) * HEAD_DIM].astype(BF16))
            vt_ref[0, g, h * V_ROWS + HEAD_DIM:(h + 1) * V_ROWS, :] = ones_pad
        wt_out_ref[0, g] = tr[o_v + ATT_WIDTH:o_v + ATT_WIDTH + IDX_HEADS] * w_scale

        gate_b = nat[:, o_c:o_c + CONV_WIDTH]
        u = nat[:, o_c + CONV_WIDTH:o_c + 2 * CONV_WIDTH] * nat[:, o_c + 2 * CONV_WIDTH:o_c + 3 * CONV_WIDTH]
        base = SUBLANES + g * tm
        u_ref[base:base + tm, :] = u
        um1 = u_ref[base - 1:base - 1 + tm, :]
        um2 = u_ref[base - 2:base - 2 + tm, :]
        conv_out = gate_b * (cw[0:1, :] * um2 + cw[1:2, :] * um1 + cw[2:3, :] * u)
        ms = jnp.mean(conv_out * conv_out, axis=-1, keepdims=True)
        cm_ref[0, rows, :] = (conv_out * lax.rsqrt(ms + NORM_EPS) * gconv_ref[...]).astype(BF16)

    tail = SUBLANES + PROJ_GROUPS * tm
    u_ref[0:SUBLANES, :] = u_ref[tail - SUBLANES:tail, :]


def _proj_call(x, wn, wt, cos_n, sin_n, cos_t, sin_t, conv_w, g_conv):
    bsz, seq, _ = x.shape
    tm = SEQ_TILE
    nt = seq // tm
    gm = PROJ_GROUPS * tm
    n_nat = wn.shape[1]
    n_tr = wt.shape[0]
    w_scale = (IDX_DIM ** -0.5) * (IDX_HEADS ** -0.5)
    const2 = lambda b, t: (0, 0)
    out_shapes = (
        jax.ShapeDtypeStruct((bsz, seq, ATT_WIDTH), BF16),
        jax.ShapeDtypeStruct((bsz, seq, IDX_DIM), BF16),
        jax.ShapeDtypeStruct((bsz, nt, ATT_WIDTH, tm), BF16),
        jax.ShapeDtypeStruct((bsz, nt, IDX_HEADS * IDX_DIM, tm), BF16),
        jax.ShapeDtypeStruct((bsz, nt, ATT_HEADS * V_ROWS, tm), BF16),
        jax.ShapeDtypeStruct((bsz, nt, IDX_HEADS, tm), F32),
        jax.ShapeDtypeStruct((bsz, seq, CONV_WIDTH), BF16),
    )
    tile4 = lambda b, t: (b, t, 0, 0)
    return pl.pallas_call(
        functools.partial(_proj_kernel, w_scale=w_scale),
        grid=(bsz, nt // PROJ_GROUPS),
        in_specs=[
            pl.BlockSpec((1, gm, D_MODEL), lambda b, t: (b, t, 0)),
            pl.BlockSpec((D_MODEL, n_nat), const2),
            pl.BlockSpec((n_tr, D_MODEL), const2),
            pl.BlockSpec((gm, LANES), lambda b, t: (t, 0)),
            pl.BlockSpec((gm, LANES), lambda b, t: (t, 0)),
            pl.BlockSpec((HEAD_DIM // 2, gm), lambda b, t: (0, t)),
            pl.BlockSpec((HEAD_DIM // 2, gm), lambda b, t: (0, t)),
            pl.BlockSpec((CONV_K, CONV_WIDTH), const2),
            pl.BlockSpec((1, CONV_WIDTH), const2),
        ],
        out_specs=(
            pl.BlockSpec((1, gm, ATT_WIDTH), lambda b, t: (b, t, 0)),
            pl.BlockSpec((1, gm, IDX_DIM), lambda b, t: (b, t, 0)),
            pl.BlockSpec((1, PROJ_GROUPS, ATT_WIDTH, tm), tile4),
            pl.BlockSpec((1, PROJ_GROUPS, IDX_HEADS * IDX_DIM, tm), tile4),
            pl.BlockSpec((1, PROJ_GROUPS, ATT_HEADS * V_ROWS, tm), tile4),
            pl.BlockSpec((1, PROJ_GROUPS, IDX_HEADS, tm), tile4),
            pl.BlockSpec((1, gm, CONV_WIDTH), lambda b, t: (b, t, 0)),
        ),
        out_shape=out_shapes,
        scratch_shapes=[pltpu.VMEM((gm + SUBLANES, CONV_WIDTH), F32)],
        compiler_params=pltpu.CompilerParams(
            dimension_semantics=("arbitrary", "arbitrary"), vmem_limit_bytes=VMEM_LIMIT),
        name="proj_rope_conv",
    )(x, wn, wt, cos_n, sin_n, cos_t, sin_t, conv_w, g_conv)


def _key_to_float(key):
    key = jnp.clip(key, KEY_NEG_INF, KEY_POS_INF)
    bits = jnp.where(key >= 0, key, key ^ 0x7FFFFFFF)
    return lax.bitcast_convert_type(bits, F32)


def _after(value, anchor):
    return jnp.maximum(value, jnp.minimum(anchor, -jnp.inf))


def _fold_rows(a):
    return a.reshape(a.shape[0] // SUBLANES, SUBLANES, a.shape[1])


def _dsa_kernel(kn_ref, kin_ref, vt_ref, qt_ref, qit_ref, w_ref, g_ref, out_ref,
                score_ref, wbd_ref, acc_ref, at_ref, lg_ref, *, topk, seq):
    i = pl.program_id(1)
    tq = qt_ref.shape[3]
    tk = tq
    nchunks = i + 1
    neg_inf = jnp.float32(-jnp.inf)

    row_iota = lax.broadcasted_iota(jnp.int32, (tk, tq), 0)
    qpos = i * tq + lax.broadcasted_iota(jnp.int32, (tk, tq), 1)

    def chunk_off(c):
        return pl.multiple_of(c * tk, tk)

    qi_all = qit_ref[0, 0]
    w_rows = w_ref[0, 0]

    def score_body(c, carry):
        off = chunk_off(c)
        kic = kin_ref[0, pl.ds(off, tk), :]
        sc = jnp.zeros((tk, tq), F32)
        for h in range(IDX_HEADS):
            d = jnp.dot(kic, qi_all[h * IDX_DIM:(h + 1) * IDX_DIM, :], preferred_element_type=F32)
            sc = sc + jnp.maximum(d, 0.0) * w_rows[h:h + 1, :]
        sc = jnp.where(off + row_iota <= qpos, sc, neg_inf)
        score_ref[pl.ds(off, tk), :] = sc
        return carry

    def score_body2(c2, carry):
        return score_body(2 * c2 + 1, score_body(2 * c2, carry))

    lax.fori_loop(0, nchunks // 2, score_body2, 0)
    lax.fori_loop(2 * (nchunks // 2), nchunks, score_body, 0)

    row_iota3 = (lax.broadcasted_iota(jnp.int32, (tk // SUBLANES, SUBLANES, tq), 0) * SUBLANES
                 + lax.broadcasted_iota(jnp.int32, (tk // SUBLANES, SUBLANES, tq), 1))
    qpos3 = i * tq + lax.broadcasted_iota(jnp.int32, (tk // SUBLANES, SUBLANES, tq), 2)

    def count_rows(pred_fn):
        def count_chunk(c, accs):
            off = chunk_off(c)
            m = pred_fn(_fold_rows(score_ref[pl.ds(off, tk), :]), off).astype(jnp.int32)
            per = m.shape[0] // COUNT_ACCS
            return tuple(acc + jnp.sum(m[g * per:(g + 1) * per], axis=0) for g, acc in enumerate(accs))

        def unrolled(cu, accs):
            for k in range(COUNT_UNROLL):
                accs = count_chunk(COUNT_UNROLL * cu + k, accs)
            return accs

        n_main = nchunks // COUNT_UNROLL
        accs = lax.fori_loop(0, n_main, unrolled, (jnp.zeros((SUBLANES, tq), jnp.int32),) * COUNT_ACCS)
        accs = lax.fori_loop(n_main * COUNT_UNROLL, nchunks, count_chunk, accs)
        acc = (accs[0] + accs[1]) + (accs[2] + accs[3])
        for shift in (4, 2, 1):
            acc = acc + pltpu.roll(acc, shift, axis=0)
        return acc

    def search_body(p, carry):
        u, n_u = carry
        trial = u | jnp.left_shift(jnp.int32(1), 31 - p)
        cand = _key_to_float(trial ^ INT_MIN)
        cnt = count_rows(lambda sc, off: sc >= cand[None])
        take = cnt >= topk
        return jnp.where(take, trial, u), jnp.where(take, cnt, n_u)

    u, n_ge = lax.fori_loop(0, 32, search_body,
                            (jnp.zeros((SUBLANES, tq), jnp.int32),
                             jnp.zeros((SUBLANES, tq), jnp.int32) + nchunks * tk))
    thr = _key_to_float(u ^ INT_MIN)

    n_gt = count_rows(lambda sc, off: sc > thr[None])
    want = topk - n_gt
    has_excess = jnp.max(n_ge - n_gt - want) > 0

    def resolve_ties():
        def body(p, jp):
            trial = jp + jnp.left_shift(jnp.int32(1), (seq - 1).bit_length() - 1 - p)
            cnt = count_rows(lambda sc, off: (sc == thr[None]) & (off + row_iota3 <= trial[None]))
            return jnp.where(cnt < want, trial, jp)
        jlim = lax.fori_loop(0, (seq - 1).bit_length(), body, jnp.full((SUBLANES, tq), -1, jnp.int32)) + 1

        def rewrite(c, carry):
            off = chunk_off(c)
            sc = _fold_rows(score_ref[pl.ds(off, tk), :])
            sel = (sc > thr[None]) | ((sc == thr[None]) & (off + row_iota3 <= jlim[None]))
            score_ref[pl.ds(off, tk), :] = jnp.where(sel, 1.0, -1.0).reshape(tk, tq)
            return carry

        lax.fori_loop(0, nchunks, rewrite, 0)
        return jnp.zeros((SUBLANES, tq), F32)

    thr_sel = lax.cond(has_excess, resolve_ties, lambda: thr)

    q_all = qt_ref[0, 0]
    zeros_q = jnp.zeros((HEAD_DIM, tq), BF16)
    n_pairs = ATT_HEADS // 2
    for j in range(n_pairs):
        r0 = 2 * j * HEAD_DIM
        wbd_ref[j] = jnp.concatenate([
            jnp.concatenate([q_all[r0:r0 + HEAD_DIM], zeros_q], axis=1),
            jnp.concatenate([zeros_q, q_all[r0 + HEAD_DIM:r0 + 2 * HEAD_DIM]], axis=1)], axis=0)
    acc_ref[...] = jnp.zeros(acc_ref.shape, F32)

    def attn_chunks(chunks, m_all):
        units = [(k, j) for k in range(len(chunks)) for j in range(n_pairs)]
        bias2 = []
        for c in chunks:
            off = chunk_off(c)
            sel = (_fold_rows(score_ref[pl.ds(off, tk), :]) >= thr_sel[None]) & (off + row_iota3 <= qpos3)
            bias = jnp.where(sel, 0.0, neg_inf).reshape(tk, tq)
            bias2.append(jnp.concatenate([bias, bias], axis=1))

        def unit_logits(u):
            k, j = units[u]
            kc = kn_ref[0, pl.ds(chunk_off(chunks[k]), tk), 2 * j * HEAD_DIM:2 * (j + 1) * HEAD_DIM]
            return jnp.dot(kc, wbd_ref[j], preferred_element_type=F32)

        m_run = [m_all[:, 2 * j * tq:2 * (j + 1) * tq] for j in range(n_pairs)]
        lgs_raw = [unit_logits(0)]
        stage_a = []
        for u, (k, j) in enumerate(units):
            lg = lgs_raw[u] + bias2[k]
            m_old = m_run[j]
            if u + 1 < len(units):
                lgs_raw.append(unit_logits(u + 1))
                m_old = _after(m_old, lgs_raw[u + 1][0:1, :])
            m_new = jnp.maximum(m_old, jnp.max(jnp.max(_fold_rows(lg), axis=0), axis=0, keepdims=True))
            m_run[j] = m_new
            lg_ref[u] = lg
            stage_a.append((m_old, m_new))
        updates = []
        for u, (k, j) in enumerate(units):
            m_old, m_new = stage_a[u]
            m_use = jnp.where(m_new == neg_inf, 0.0, m_new)
            if u + 1 < len(units):
                m_use = _after(m_use, stage_a[u + 1][1])
            alpha = jnp.exp2(m_old - m_use)
            p = jnp.exp2(lg_ref[u] - m_use).astype(BF16)
            for hh in range(2):
                h = 2 * j + hh
                pv = jnp.dot(vt_ref[0, chunks[k], h * V_ROWS:(h + 1) * V_ROWS, :], p[:, hh * tq:(hh + 1) * tq],
                             preferred_element_type=F32)
                updates.append((h, alpha[:, hh * tq:(hh + 1) * tq], pv))
        for h, a, pv in updates:
            acc_ref[h] = acc_ref[h] * a + pv
        return jnp.concatenate(m_run, axis=1)

    m_all = jnp.full((1, ATT_HEADS * tq), neg_inf, F32)
    start = 0
    for width in ATTN_UNROLLS:
        trips = (nchunks - start) // width
        m_all = lax.fori_loop(
            0, trips, lambda it, m, s=start, w=width: attn_chunks([s + w * it + k for k in range(w)], m), m_all)
        start = start + trips * width
    for h in range(ATT_HEADS):
        o = acc_ref[h]
        at_ref[h * HEAD_DIM:(h + 1) * HEAD_DIM, :] = o[:HEAD_DIM] / o[HEAD_DIM:HEAD_DIM + 1]

    a_t = at_ref[...]
    ms = jnp.mean(a_t * a_t, axis=0, keepdims=True)
    y = (a_t * lax.rsqrt(ms + NORM_EPS)).T
    out_ref[0] = (y * g_ref[...]).astype(BF16)


def _dsa_call(kn, kin, vt, qt, qit, w_t, g_attn, topk):
    bsz, seq, _ = kn.shape
    nt, tq = qt.shape[1], qt.shape[3]
    tile4 = lambda b, i: (b, i, 0, 0)
    return pl.pallas_call(
        functools.partial(_dsa_kernel, topk=topk, seq=seq),
        grid=(bsz, nt),
        in_specs=[
            pl.BlockSpec((1, seq, ATT_WIDTH), lambda b, i: (b, 0, 0)),
            pl.BlockSpec((1, seq, IDX_DIM), lambda b, i: (b, 0, 0)),
            pl.BlockSpec((1, nt, ATT_HEADS * V_ROWS, tq), lambda b, i: (b, 0, 0, 0)),
            pl.BlockSpec((1, 1, ATT_WIDTH, tq), tile4),
            pl.BlockSpec((1, 1, IDX_HEADS * IDX_DIM, tq), tile4),
            pl.BlockSpec((1, 1, IDX_HEADS, tq), tile4),
            pl.BlockSpec((1, ATT_WIDTH), lambda b, i: (0, 0)),
        ],
        out_specs=pl.BlockSpec((1, tq, ATT_WIDTH), lambda b, i: (b, i, 0)),
        out_shape=jax.ShapeDtypeStruct((bsz, seq, ATT_WIDTH), BF16),
        scratch_shapes=[
            pltpu.VMEM((seq, tq), F32),
            pltpu.VMEM((ATT_HEADS // 2, 2 * HEAD_DIM, 2 * tq), BF16),
            pltpu.VMEM((ATT_HEADS, V_ROWS, tq), F32),
            pltpu.VMEM((ATT_WIDTH, tq), F32),
            pltpu.VMEM((max(ATTN_UNROLLS) * (ATT_HEADS // 2), tq, 2 * tq), F32),
        ],
        compiler_params=pltpu.CompilerParams(
            dimension_semantics=("arbitrary", "arbitrary"), vmem_limit_bytes=VMEM_LIMIT),
        name="dsa_attention",
    )(kn, kin, vt, qt, qit, w_t, g_attn)


def _layer_norm(y, g, b):
    mu = jnp.mean(y, axis=-1, keepdims=True)
    d = y - mu
    var = jnp.mean(d * d, axis=-1, keepdims=True)
    return d * lax.rsqrt(var + NORM_EPS) * g + b


def _ffn_kernel(x_ref, am_ref, cm_ref, wout_ref, g1_ref, b1_ref, wgu_ref, wd_ref, g2_ref, b2_ref,
                o_ref, *, alpha, d_ff):
    n = FFN_SPLIT
    group = x_ref.shape[0] // n
    rows = [slice(r * group, (r + 1) * group) for r in range(n)]
    mix = [jnp.dot(am_ref[rw, :], wout_ref[0:ATT_WIDTH, :], preferred_element_type=F32)
           + jnp.dot(cm_ref[rw, :], wout_ref[ATT_WIDTH:D_MODEL, :], preferred_element_type=F32) for rw in rows]
    x1 = []
    for r in range(n):
        b1 = b1_ref[...] if r + 1 == n else _after(b1_ref[...], mix[r + 1][0:1, :])
        x1.append(_layer_norm(alpha * x_ref[rows[r], :] + mix[r], g1_ref[...], b1))
    gu = [jnp.dot(x1[r].astype(BF16), wgu_ref[...], preferred_element_type=F32) for r in range(n)]
    ffn = []
    for r in range(n):
        gate = gu[r][:, :d_ff]
        hidden = (gate * jax.nn.sigmoid(gate) * gu[r][:, d_ff:]).astype(BF16)
        ffn.append(jnp.dot(hidden, wd_ref[...], preferred_element_type=F32))
    for r in range(n):
        b2 = b2_ref[...] if r + 1 == n else _after(b2_ref[...], ffn[r + 1][0:1, :])
        o_ref[rows[r], :] = _layer_norm(alpha * x1[r] + ffn[r], g2_ref[...], b2)


def _ffn_call(x2d, am, cm, wout, g1, b1, wgu, wd, g2, b2, alpha):
    rows = x2d.shape[0]
    tm = ROW_TILE
    d_ff = wd.shape[0]
    row = lambda r: (r, 0)
    const = lambda r: (0, 0)
    resident = functools.partial(pl.BlockSpec, index_map=const, pipeline_mode=pl.Buffered(1))
    return pl.pallas_call(
        functools.partial(_ffn_kernel, alpha=alpha, d_ff=d_ff),
        grid=(rows // tm,),
        in_specs=[
            pl.BlockSpec((tm, D_MODEL), row),
            pl.BlockSpec((tm, ATT_WIDTH), row),
            pl.BlockSpec((tm, CONV_WIDTH), row),
            resident((D_MODEL, D_MODEL)),
            resident((1, D_MODEL)),
            resident((1, D_MODEL)),
            resident((D_MODEL, 2 * d_ff)),
            resident((d_ff, D_MODEL)),
            resident((1, D_MODEL)),
            resident((1, D_MODEL)),
        ],
        out_specs=pl.BlockSpec((tm, D_MODEL), row),
        out_shape=jax.ShapeDtypeStruct((rows, D_MODEL), F32),
        compiler_params=pltpu.CompilerParams(
            dimension_semantics=("arbitrary",), vmem_limit_bytes=VMEM_LIMIT),
        name="outproj_ffn",
    )(x2d, am, cm, wout, g1, b1, wgu, wd, g2, b2)


def _rotate_half_cols(w):
    d_in, n = w.shape
    w4 = w.reshape(d_in, n // HEAD_DIM, 2, HEAD_DIM // 2)
    return jnp.concatenate([-w4[:, :, 1:2, :], w4[:, :, 0:1, :]], axis=2).reshape(d_in, n)


def _split_cols(w, sizes):
    out, o = [], 0
    for s in sizes:
        out.append(w[:, o:o + s])
        o += s
    return out


def kernel(x, w_in, conv_w, mix_norm_g, w_out, ln1_g, ln1_b, w_gate_up, w_down, ln2_g, ln2_b):
    bsz, seq, _ = x.shape
    depth = w_in.shape[0]
    alpha = (2 * depth) ** 0.25
    topk = min(INDEX_TOPK_MAX, seq // 4)
    assert seq % (SEQ_TILE * PROJ_GROUPS) == 0 and (bsz * seq) % ROW_TILE == 0 and topk <= SEQ_TILE

    inv_freq = 1.0 / (ROPE_THETA ** (np.arange(0, HEAD_DIM, 2, dtype=np.float64) / HEAD_DIM))
    ang = np.arange(seq, dtype=np.float64)[:, None] * inv_freq[None, :]
    cos, sin = np.cos(ang).astype(np.float32), np.sin(ang).astype(np.float32)
    cos_n, sin_n = np.tile(cos, (1, LANES // cos.shape[1])), np.tile(sin, (1, LANES // sin.shape[1]))
    cos_t, sin_t = np.ascontiguousarray(cos.T), np.ascontiguousarray(sin.T)

    for layer in range(depth):
        wq, wk, wv, wqi, wki, wwi, wgb, wgc, wh = _split_cols(w_in[layer].astype(BF16), IN_SPLIT_SIZES)
        pad_ki = jnp.zeros((D_MODEL, LANES - IDX_DIM), BF16)
        wn = jnp.concatenate([wk, _rotate_half_cols(wk), wgb, wgc, wh,
                              wki, pad_ki, _rotate_half_cols(wki), pad_ki], axis=1)
        pad_wi = jnp.zeros((D_MODEL, 2 * SUBLANES - IDX_HEADS), BF16)
        wt = jnp.concatenate([wq, wqi, wv, wwi, pad_wi], axis=1).T
        g = mix_norm_g[layer]
        kn, kin, qt, qit, vt, w_t, cm = _proj_call(
            x, wn, wt, cos_n, sin_n, cos_t, sin_t, conv_w[layer], g[ATT_WIDTH:][None, :])
        am = _dsa_call(kn, kin, vt, qt, qit, w_t, g[:ATT_WIDTH][None, :], topk)
        y = _ffn_call(
            x.reshape(bsz * seq, D_MODEL), am.reshape(bsz * seq, ATT_WIDTH), cm.reshape(bsz * seq, CONV_WIDTH),
            w_out[layer].astype(BF16), ln1_g[layer][None, :], ln1_b[layer][None, :],
            w_gate_up[layer].astype(BF16), w_down[layer].astype(BF16),
            ln2_g[layer][None, :], ln2_b[layer][None, :], alpha)
        x = y.reshape(bsz, seq, D_MODEL)
    return x
```

```python
import functools

import jax
import numpy as np
import jax.numpy as jnp
from jax import lax
from jax.experimental import pallas as pl
from jax.experimental.pallas import tpu as pltpu

D_MODEL = 1024
ATT_HEADS = 8
HEAD_DIM = 64
ATT_WIDTH = ATT_HEADS * HEAD_DIM
CONV_WIDTH = D_MODEL - ATT_WIDTH
CONV_K = 3
IDX_HEADS = 8
IDX_DIM = 64
INDEX_TOPK_MAX = 256
ROPE_THETA = 10000.0
NORM_EPS = 1e-5
IN_SPLIT_SIZES = (ATT_WIDTH, ATT_WIDTH, ATT_WIDTH, IDX_HEADS * IDX_DIM, IDX_DIM, IDX_HEADS,
                  CONV_WIDTH, CONV_WIDTH, CONV_WIDTH)

LANES = 128
SUBLANES = 8
SEQ_TILE = 256
PROJ_GROUPS = 2
ROW_TILE = 512
FFN_SPLIT = 2
COUNT_UNROLL = 4
COUNT_ACCS = 4
SORT_GROUP = 8
ATTN_UNROLLS = (4, 2, 1)
V_ROWS = 80
LOG2E = 1.4426950408889634
VMEM_LIMIT = 56 * 1024 * 1024

F32 = jnp.float32
BF16 = jnp.bfloat16
INT_MIN = -2 ** 31
KEY_NEG_INF = (0xFF800000 ^ 0x7FFFFFFF) - 2 ** 32
KEY_POS_INF = 0x7F800000


def _proj_kernel(x_ref, wn_ref, wt_ref, cosn_ref, sinn_ref, cost_ref, sint_ref, convw_ref, gconv_ref,
                 kn_ref, kin_ref, qt_ref, qit_ref, vt_ref, wt_out_ref, cm_ref, u_ref, *, w_scale):
    t = pl.program_id(1)
    tm = qt_ref.shape[3]
    half = HEAD_DIM // 2
    q_scale = (HEAD_DIM ** -0.5) * LOG2E
    o_ki = 2 * ATT_WIDTH + 3 * CONV_WIDTH
    o_v = ATT_WIDTH + IDX_HEADS * IDX_DIM
    o_c = 2 * ATT_WIDTH
    pad_row = lax.broadcasted_iota(jnp.int32, (V_ROWS - HEAD_DIM, tm), 0)
    ones_pad = jnp.where(pad_row == 0, 1.0, 0.0).astype(BF16)
    cw = convw_ref[...]

    @pl.when(t == 0)
    def _():
        u_ref[0:SUBLANES, :] = jnp.zeros((SUBLANES, CONV_WIDTH), F32)

    xbs = [x_ref[0, g * tm:(g + 1) * tm, :].astype(BF16) for g in range(PROJ_GROUPS)]
    nats = [jnp.dot(xb, wn_ref[...], preferred_element_type=F32) for xb in xbs]
    trs = [lax.dot_general(wt_ref[...], xb, (((1,), (1,)), ((), ())), preferred_element_type=F32)
           for xb in xbs]

    for g in range(PROJ_GROUPS):
        nat, tr = nats[g], trs[g]
        rows = slice(g * tm, (g + 1) * tm)

        cos_n = cosn_ref[rows, :]
        sin_n = sinn_ref[rows, :]
        cos4 = jnp.concatenate([cos_n] * (ATT_WIDTH // LANES), axis=1)
        sin4 = jnp.concatenate([sin_n] * (ATT_WIDTH // LANES), axis=1)
        k_rope = nat[:, 0:ATT_WIDTH] * cos4 + nat[:, ATT_WIDTH:2 * ATT_WIDTH] * sin4
        kn_ref[0, rows, :] = k_rope.astype(BF16)
        ki_rope = nat[:, o_ki:o_ki + LANES] * cos_n + nat[:, o_ki + LANES:o_ki + 2 * LANES] * sin_n
        kin_ref[0, rows, :] = ki_rope[:, :IDX_DIM].astype(BF16)

        cos_t = cost_ref[:, rows]
        sin_t = sint_ref[:, rows]
        for h in range(ATT_HEADS):
            r0 = h * HEAD_DIM
            x1 = tr[r0:r0 + half]
            x2 = tr[r0 + half:r0 + HEAD_DIM]
            qt_ref[0, g, r0:r0 + half, :] = ((x1 * cos_t - x2 * sin_t) * q_scale).astype(BF16)
            qt_ref[0, g, r0 + half:r0 + HEAD_DIM, :] = ((x2 * cos_t + x1 * sin_t) * q_scale).astype(BF16)
        for h in range(IDX_HEADS):
            r0 = ATT_WIDTH + h * IDX_DIM
            x1 = tr[r0:r0 + half]
            x2 = tr[r0 + half:r0 + IDX_DIM]
            qit_ref[0, g, h * IDX_DIM:h * IDX_DIM + half, :] = (x1 * cos_t - x2 * sin_t).astype(BF16)
            qit_ref[0, g, h * IDX_DIM + half:(h + 1) * IDX_DIM, :] = (x2 * cos_t + x1 * sin_t).astype(BF16)
        for h in range(ATT_HEADS):
            vt_ref[0, g, h * V_ROWS:h * V_ROWS + HEAD_DIM, :] = (
                tr[o_v + h * HEAD_DIM:o_v + (h + 1) * HEAD_DIM].astype(BF16))
            vt_ref[0, g, h * V_ROWS + HEAD_DIM:(h + 1) * V_ROWS, :] = ones_pad
        wt_out_ref[0, g] = tr[o_v + ATT_WIDTH:o_v + ATT_WIDTH + IDX_HEADS] * w_scale

        gate_b = nat[:, o_c:o_c + CONV_WIDTH]
        u = nat[:, o_c + CONV_WIDTH:o_c + 2 * CONV_WIDTH] * nat[:, o_c + 2 * CONV_WIDTH:o_c + 3 * CONV_WIDTH]
        base = SUBLANES + g * tm
        u_ref[base:base + tm, :] = u
        um1 = u_ref[base - 1:base - 1 + tm, :]
        um2 = u_ref[base - 2:base - 2 + tm, :]
        conv_out = gate_b * (cw[0:1, :] * um2 + cw[1:2, :] * um1 + cw[2:3, :] * u)
        ms = jnp.mean(conv_out * conv_out, axis=-1, keepdims=True)
        cm_ref[0, rows, :] = (conv_out * lax.rsqrt(ms + NORM_EPS) * gconv_ref[...]).astype(BF16)

    tail = SUBLANES + PROJ_GROUPS * tm
    u_ref[0:SUBLANES, :] = u_ref[tail - SUBLANES:tail, :]


def _proj_call(x, wn, wt, cos_n, sin_n, cos_t, sin_t, conv_w, g_conv):
    bsz, seq, _ = x.shape
    tm = SEQ_TILE
    nt = seq // tm
    gm = PROJ_GROUPS * tm
    n_nat = wn.shape[1]
    n_tr = wt.shape[0]
    w_scale = (IDX_DIM ** -0.5) * (IDX_HEADS ** -0.5)
    const2 = lambda b, t: (0, 0)
    out_shapes = (
        jax.ShapeDtypeStruct((bsz, seq, ATT_WIDTH), BF16),
        jax.ShapeDtypeStruct((bsz, seq, IDX_DIM), BF16),
        jax.ShapeDtypeStruct((bsz, nt, ATT_WIDTH, tm), BF16),
        jax.ShapeDtypeStruct((bsz, nt, IDX_HEADS * IDX_DIM, tm), BF16),
        jax.ShapeDtypeStruct((bsz, nt, ATT_HEADS * V_ROWS, tm), BF16),
        jax.ShapeDtypeStruct((bsz, nt, IDX_HEADS, tm), F32),
        jax.ShapeDtypeStruct((bsz, seq, CONV_WIDTH), BF16),
    )
    tile4 = lambda b, t: (b, t, 0, 0)
    return pl.pallas_call(
        functools.partial(_proj_kernel, w_scale=w_scale),
        grid=(bsz, nt // PROJ_GROUPS),
        in_specs=[
            pl.BlockSpec((1, gm, D_MODEL), lambda b, t: (b, t, 0)),
            pl.BlockSpec((D_MODEL, n_nat), const2),
            pl.BlockSpec((n_tr, D_MODEL), const2),
            pl.BlockSpec((gm, LANES), lambda b, t: (t, 0)),
            pl.BlockSpec((gm, LANES), lambda b, t: (t, 0)),
            pl.BlockSpec((HEAD_DIM // 2, gm), lambda b, t: (0, t)),
            pl.BlockSpec((HEAD_DIM // 2, gm), lambda b, t: (0, t)),
            pl.BlockSpec((CONV_K, CONV_WIDTH), const2),
            pl.BlockSpec((1, CONV_WIDTH), const2),
        ],
        out_specs=(
            pl.BlockSpec((1, gm, ATT_WIDTH), lambda b, t: (b, t, 0)),
            pl.BlockSpec((1, gm, IDX_DIM), lambda b, t: (b, t, 0)),
            pl.BlockSpec((1, PROJ_GROUPS, ATT_WIDTH, tm), tile4),
            pl.BlockSpec((1, PROJ_GROUPS, IDX_HEADS * IDX_DIM, tm), tile4),
            pl.BlockSpec((1, PROJ_GROUPS, ATT_HEADS * V_ROWS, tm), tile4),
            pl.BlockSpec((1, PROJ_GROUPS, IDX_HEADS, tm), tile4),
            pl.BlockSpec((1, gm, CONV_WIDTH), lambda b, t: (b, t, 0)),
        ),
        out_shape=out_shapes,
        scratch_shapes=[pltpu.VMEM((gm + SUBLANES, CONV_WIDTH), F32)],
        compiler_params=pltpu.CompilerParams(
            dimension_semantics=("arbitrary", "arbitrary"), vmem_limit_bytes=VMEM_LIMIT),
        name="proj_rope_conv",
    )(x, wn, wt, cos_n, sin_n, cos_t, sin_t, conv_w, g_conv)


def _key_to_float(key):
    key = jnp.clip(key, KEY_NEG_INF, KEY_POS_INF)
    bits = jnp.where(key >= 0, key, key ^ 0x7FFFFFFF)
    return lax.bitcast_convert_type(bits, F32)


def _after(value, anchor):
    return jnp.maximum(value, jnp.minimum(anchor, -jnp.inf))


def _sorting_network(n):
    pairs = []

    def merge(lo, size, r):
        step = 2 * r
        if step < size:
            merge(lo, size, step)
            merge(lo + r, size, step)
            pairs.extend((i, i + r) for i in range(lo + r, lo + size - r, step))
        else:
            pairs.append((lo, lo + r))

    def sort(lo, size):
        if size > 1:
            sort(lo, size // 2)
            sort(lo + size // 2, size // 2)
            merge(lo, size, 1)

    sort(0, n)
    return pairs


def _sorted_group_count(grp, pred):
    size = len(grp)
    levels = size.bit_length() - 1
    masks = []

    def pick(cands, decisions):
        if len(cands) == 1:
            return cands[0]
        mid = len(cands) // 2
        return jnp.where(decisions[0], pick(cands[mid:], decisions[1:]), pick(cands[:mid], decisions[1:]))

    for level in range(1, levels + 1):
        step = size >> level
        cands = [grp[(2 * t + 1) * step - 1] for t in range(1 << (level - 1))]
        masks.append(pred(pick(cands, masks)))
    every = pred(grp[size - 1])
    count = jnp.where(every, 2, jnp.where(masks[levels - 1], 1, 0))
    for level in range(1, levels):
        count = count + jnp.where(masks[level - 1], size >> level, 0)
    return count


def _fold_rows(a):
    return a.reshape(a.shape[0] // SUBLANES, SUBLANES, a.shape[1])


def _dsa_kernel(kn_ref, kin_ref, vt_ref, qt_ref, qit_ref, w_ref, g_ref, out_ref,
                score_ref, sorted_ref, wbd_ref, acc_ref, at_ref, lg_ref, *, topk, seq):
    i = pl.program_id(1)
    tq = qt_ref.shape[3]
    tk = tq
    nchunks = i + 1
    neg_inf = jnp.float32(-jnp.inf)

    row_iota = lax.broadcasted_iota(jnp.int32, (tk, tq), 0)
    qpos = i * tq + lax.broadcasted_iota(jnp.int32, (tk, tq), 1)

    def chunk_off(c):
        return pl.multiple_of(c * tk, tk)

    qi_all = qit_ref[0, 0]
    w_rows = w_ref[0, 0]

    def score_body(c, carry):
        off = chunk_off(c)
        kic = kin_ref[0, pl.ds(off, tk), :]
        sc = jnp.zeros((tk, tq), F32)
        for h in range(IDX_HEADS):
            d = jnp.dot(kic, qi_all[h * IDX_DIM:(h + 1) * IDX_DIM, :], preferred_element_type=F32)
            sc = sc + jnp.maximum(d, 0.0) * w_rows[h:h + 1, :]
        sc = jnp.where(off + row_iota <= qpos, sc, neg_inf)
        score_ref[pl.ds(off, tk), :] = sc
        rows = _fold_rows(sc)
        sorted_rows = []
        for g in range(tk // SUBLANES // SORT_GROUP):
            grp = [rows[SORT_GROUP * g + r] for r in range(SORT_GROUP)]
            for a, b in _sorting_network(SORT_GROUP):
                grp[a], grp[b] = jnp.maximum(grp[a], grp[b]), jnp.minimum(grp[a], grp[b])
            sorted_rows.extend(grp)
        sorted_ref[pl.ds(off, tk), :] = jnp.stack(sorted_rows).reshape(tk, tq)
        return carry

    def score_body2(c2, carry):
        return score_body(2 * c2 + 1, score_body(2 * c2, carry))

    lax.fori_loop(0, nchunks // 2, score_body2, 0)
    lax.fori_loop(2 * (nchunks // 2), nchunks, score_body, 0)

    row_iota3 = (lax.broadcasted_iota(jnp.int32, (tk // SUBLANES, SUBLANES, tq), 0) * SUBLANES
                 + lax.broadcasted_iota(jnp.int32, (tk // SUBLANES, SUBLANES, tq), 1))
    qpos3 = i * tq + lax.broadcasted_iota(jnp.int32, (tk // SUBLANES, SUBLANES, tq), 2)

    def count_total(chunk_partials, n_parts):
        def count_chunk(c, accs):
            return tuple(acc + part for acc, part in zip(accs, chunk_partials(c)))

        def unrolled(cu, accs):
            for k in range(COUNT_UNROLL):
                accs = count_chunk(COUNT_UNROLL * cu + k, accs)
            return accs

        n_main = nchunks // COUNT_UNROLL
        accs = lax.fori_loop(0, n_main, unrolled, (jnp.zeros((SUBLANES, tq), jnp.int32),) * n_parts)
        accs = lax.fori_loop(n_main * COUNT_UNROLL, nchunks, count_chunk, accs)
        acc = functools.reduce(lambda x, y: x + y, accs)
        for shift in (4, 2, 1):
            acc = acc + pltpu.roll(acc, shift, axis=0)
        return acc

    def count_rows(pred_fn):
        def partials(c):
            off = chunk_off(c)
            m = pred_fn(_fold_rows(score_ref[pl.ds(off, tk), :]), off).astype(jnp.int32)
            per = m.shape[0] // COUNT_ACCS
            return [jnp.sum(m[g * per:(g + 1) * per], axis=0) for g in range(COUNT_ACCS)]
        return count_total(partials, COUNT_ACCS)

    blocks_per_chunk = tk // SUBLANES // SORT_GROUP

    def count_sorted(pred):
        def partials(c):
            rows = _fold_rows(sorted_ref[pl.ds(chunk_off(c), tk), :])
            return [_sorted_group_count([rows[SORT_GROUP * g + r] for r in range(SORT_GROUP)], pred)
                    for g in range(blocks_per_chunk)]
        return count_total(partials, blocks_per_chunk)

    def search_body(p, carry):
        u, n_u = carry
        trial = u | jnp.left_shift(jnp.int32(1), 31 - p)
        cand = _key_to_float(trial ^ INT_MIN)
        cnt = count_sorted(lambda sc: sc >= cand)
        take = cnt >= topk
        return jnp.where(take, trial, u), jnp.where(take, cnt, n_u)

    u, n_ge = lax.fori_loop(0, 32, search_body,
                            (jnp.zeros((SUBLANES, tq), jnp.int32),
                             jnp.zeros((SUBLANES, tq), jnp.int32) + nchunks * tk))
    thr = _key_to_float(u ^ INT_MIN)

    n_gt = count_sorted(lambda sc: sc > thr)
    want = topk - n_gt
    has_excess = jnp.max(n_ge - n_gt - want) > 0

    def resolve_ties():
        def body(p, jp):
            trial = jp + jnp.left_shift(jnp.int32(1), (seq - 1).bit_length() - 1 - p)
            cnt = count_rows(lambda sc, off: (sc == thr[None]) & (off + row_iota3 <= trial[None]))
            return jnp.where(cnt < want, trial, jp)
        jlim = lax.fori_loop(0, (seq - 1).bit_length(), body, jnp.full((SUBLANES, tq), -1, jnp.int32)) + 1

        def rewrite(c, carry):
            off = chunk_off(c)
            sc = _fold_rows(score_ref[pl.ds(off, tk), :])
            sel = (sc > thr[None]) | ((sc == thr[None]) & (off + row_iota3 <= jlim[None]))
            score_ref[pl.ds(off, tk), :] = jnp.where(sel, 1.0, -1.0).reshape(tk, tq)
            return carry

        lax.fori_loop(0, nchunks, rewrite, 0)
        return jnp.zeros((SUBLANES, tq), F32)

    thr_sel = lax.cond(has_excess, resolve_ties, lambda: thr)

    q_all = qt_ref[0, 0]
    zeros_q = jnp.zeros((HEAD_DIM, tq), BF16)
    n_pairs = ATT_HEADS // 2
    for j in range(n_pairs):
        r0 = 2 * j * HEAD_DIM
        wbd_ref[j] = jnp.concatenate([
            jnp.concatenate([q_all[r0:r0 + HEAD_DIM], zeros_q], axis=1),
            jnp.concatenate([zeros_q, q_all[r0 + HEAD_DIM:r0 + 2 * HEAD_DIM]], axis=1)], axis=0)
    acc_ref[...] = jnp.zeros(acc_ref.shape, F32)

    def attn_chunks(chunks, m_all):
        units = [(k, j) for k in range(len(chunks)) for j in range(n_pairs)]
        bias2 = []
        for c in chunks:
            off = chunk_off(c)
            sel = (_fold_rows(score_ref[pl.ds(off, tk), :]) >= thr_sel[None]) & (off + row_iota3 <= qpos3)
            bias = jnp.where(sel, 0.0, neg_inf).reshape(tk, tq)
            bias2.append(jnp.concatenate([bias, bias], axis=1))

        def unit_logits(u):
            k, j = units[u]
            kc = kn_ref[0, pl.ds(chunk_off(chunks[k]), tk), 2 * j * HEAD_DIM:2 * (j + 1) * HEAD_DIM]
            return jnp.dot(kc, wbd_ref[j], preferred_element_type=F32)

        m_run = [m_all[:, 2 * j * tq:2 * (j + 1) * tq] for j in range(n_pairs)]
        lgs_raw = [unit_logits(0)]
        stage_a = []
        for u, (k, j) in enumerate(units):
            lg = lgs_raw[u] + bias2[k]
            m_old = m_run[j]
            if u + 1 < len(units):
                lgs_raw.append(unit_logits(u + 1))
                m_old = _after(m_old, lgs_raw[u + 1][0:1, :])
            m_new = jnp.maximum(m_old, jnp.max(jnp.max(_fold_rows(lg), axis=0), axis=0, keepdims=True))
            m_run[j] = m_new
            lg_ref[u] = lg
            stage_a.append((m_old, m_new))
        updates = []
        for u, (k, j) in enumerate(units):
            m_old, m_new = stage_a[u]
            m_use = jnp.where(m_new == neg_inf, 0.0, m_new)
            if u + 1 < len(units):
                m_use = _after(m_use, stage_a[u + 1][1])
            alpha = jnp.exp2(m_old - m_use)
            p = jnp.exp2(lg_ref[u] - m_use).astype(BF16)
            for hh in range(2):
                h = 2 * j + hh
                pv = jnp.dot(vt_ref[0, chunks[k], h * V_ROWS:(h + 1) * V_ROWS, :], p[:, hh * tq:(hh + 1) * tq],
                             preferred_element_type=F32)
                updates.append((h, alpha[:, hh * tq:(hh + 1) * tq], pv))
        for h, a, pv in updates:
            acc_ref[h] = acc_ref[h] * a + pv
        return jnp.concatenate(m_run, axis=1)

    m_all = jnp.full((1, ATT_HEADS * tq), neg_inf, F32)
    start = 0
    for width in ATTN_UNROLLS:
        trips = (nchunks - start) // width
        m_all = lax.fori_loop(
            0, trips, lambda it, m, s=start, w=width: attn_chunks([s + w * it + k for k in range(w)], m), m_all)
        start = start + trips * width
    for h in range(ATT_HEADS):
        o = acc_ref[h]
        at_ref[h * HEAD_DIM:(h + 1) * HEAD_DIM, :] = o[:HEAD_DIM] / o[HEAD_DIM:HEAD_DIM + 1]

    a_t = at_ref[...]
    ms = jnp.mean(a_t * a_t, axis=0, keepdims=True)
    y = (a_t * lax.rsqrt(ms + NORM_EPS)).T
    out_ref[0] = (y * g_ref[...]).astype(BF16)


def _dsa_call(kn, kin, vt, qt, qit, w_t, g_attn, topk):
    bsz, seq, _ = kn.shape
    nt, tq = qt.shape[1], qt.shape[3]
    tile4 = lambda b, i: (b, i, 0, 0)
    return pl.pallas_call(
        functools.partial(_dsa_kernel, topk=topk, seq=seq),
        grid=(bsz, nt),
        in_specs=[
            pl.BlockSpec((1, seq, ATT_WIDTH), lambda b, i: (b, 0, 0)),
            pl.BlockSpec((1, seq, IDX_DIM), lambda b, i: (b, 0, 0)),
            pl.BlockSpec((1, nt, ATT_HEADS * V_ROWS, tq), lambda b, i: (b, 0, 0, 0)),
            pl.BlockSpec((1, 1, ATT_WIDTH, tq), tile4),
            pl.BlockSpec((1, 1, IDX_HEADS * IDX_DIM, tq), tile4),
            pl.BlockSpec((1, 1, IDX_HEADS, tq), tile4),
            pl.BlockSpec((1, ATT_WIDTH), lambda b, i: (0, 0)),
        ],
        out_specs=pl.BlockSpec((1, tq, ATT_WIDTH), lambda b, i: (b, i, 0)),
        out_shape=jax.ShapeDtypeStruct((bsz, seq, ATT_WIDTH), BF16),
        scratch_shapes=[
            pltpu.VMEM((seq, tq), F32),
            pltpu.VMEM((seq, tq), F32),
            pltpu.VMEM((ATT_HEADS // 2, 2 * HEAD_DIM, 2 * tq), BF16),
            pltpu.VMEM((ATT_HEADS, V_ROWS, tq), F32),
            pltpu.VMEM((ATT_WIDTH, tq), F32),
            pltpu.VMEM((max(ATTN_UNROLLS) * (ATT_HEADS // 2), tq, 2 * tq), F32),
        ],
        compiler_params=pltpu.CompilerParams(
            dimension_semantics=("arbitrary", "arbitrary"), vmem_limit_bytes=VMEM_LIMIT),
        name="dsa_attention",
    )(kn, kin, vt, qt, qit, w_t, g_attn)


def _layer_norm(y, g, b):
    mu = jnp.mean(y, axis=-1, keepdims=True)
    d = y - mu
    var = jnp.mean(d * d, axis=-1, keepdims=True)
    return d * lax.rsqrt(var + NORM_EPS) * g + b


def _ffn_kernel(x_ref, am_ref, cm_ref, wout_ref, g1_ref, b1_ref, wgu_ref, wd_ref, g2_ref, b2_ref,
                o_ref, *, alpha, d_ff):
    n = FFN_SPLIT
    group = x_ref.shape[0] // n
    rows = [slice(r * group, (r + 1) * group) for r in range(n)]
    mix = [jnp.dot(am_ref[rw, :], wout_ref[0:ATT_WIDTH, :], preferred_element_type=F32)
           + jnp.dot(cm_ref[rw, :], wout_ref[ATT_WIDTH:D_MODEL, :], preferred_element_type=F32) for rw in rows]
    x1 = []
    for r in range(n):
        b1 = b1_ref[...] if r + 1 == n else _after(b1_ref[...], mix[r + 1][0:1, :])
        x1.append(_layer_norm(alpha * x_ref[rows[r], :] + mix[r], g1_ref[...], b1))
    gu = [jnp.dot(x1[r].astype(BF16), wgu_ref[...], preferred_element_type=F32) for r in range(n)]
    ffn = []
    for r in range(n):
        gate = gu[r][:, :d_ff]
        hidden = (gate * jax.nn.sigmoid(gate) * gu[r][:, d_ff:]).astype(BF16)
        ffn.append(jnp.dot(hidden, wd_ref[...], preferred_element_type=F32))
    for r in range(n):
        b2 = b2_ref[...] if r + 1 == n else _after(b2_ref[...], ffn[r + 1][0:1, :])
        o_ref[rows[r], :] = _layer_norm(alpha * x1[r] + ffn[r], g2_ref[...], b2)


def _ffn_call(x2d, am, cm, wout, g1, b1, wgu, wd, g2, b2, alpha):
    rows = x2d.shape[0]
    tm = ROW_TILE
    d_ff = wd.shape[0]
    row = lambda r: (r, 0)
    const = lambda r: (0, 0)
    resident = functools.partial(pl.BlockSpec, index_map=const, pipeline_mode=pl.Buffered(1))
    return pl.pallas_call(
        functools.partial(_ffn_kernel, alpha=alpha, d_ff=d_ff),
        grid=(rows // tm,),
        in_specs=[
            pl.BlockSpec((tm, D_MODEL), row),
            pl.BlockSpec((tm, ATT_WIDTH), row),
            pl.BlockSpec((tm, CONV_WIDTH), row),
            resident((D_MODEL, D_MODEL)),
            resident((1, D_MODEL)),
            resident((1, D_MODEL)),
            resident((D_MODEL, 2 * d_ff)),
            resident((d_ff, D_MODEL)),
            resident((1, D_MODEL)),
            resident((1, D_MODEL)),
        ],
        out_specs=pl.BlockSpec((tm, D_MODEL), row),
        out_shape=jax.ShapeDtypeStruct((rows, D_MODEL), F32),
        compiler_params=pltpu.CompilerParams(
            dimension_semantics=("arbitrary",), vmem_limit_bytes=VMEM_LIMIT),
        name="outproj_ffn",
    )(x2d, am, cm, wout, g1, b1, wgu, wd, g2, b2)


def _rotate_half_cols(w):
    d_in, n = w.shape
    w4 = w.reshape(d_in, n // HEAD_DIM, 2, HEAD_DIM // 2)
    return jnp.concatenate([-w4[:, :, 1:2, :], w4[:, :, 0:1, :]], axis=2).reshape(d_in, n)


def _split_cols(w, sizes):
    out, o = [], 0
    for s in sizes:
        out.append(w[:, o:o + s])
        o += s
    return out


def kernel(x, w_in, conv_w, mix_norm_g, w_out, ln1_g, ln1_b, w_gate_up, w_down, ln2_g, ln2_b):
    bsz, seq, _ = x.shape
    depth = w_in.shape[0]
    alpha = (2 * depth) ** 0.25
    topk = min(INDEX_TOPK_MAX, seq // 4)
    assert seq % (SEQ_TILE * PROJ_GROUPS) == 0 and (bsz * seq) % ROW_TILE == 0 and topk <= SEQ_TILE

    inv_freq = 1.0 / (ROPE_THETA ** (np.arange(0, HEAD_DIM, 2, dtype=np.float64) / HEAD_DIM))
    ang = np.arange(seq, dtype=np.float64)[:, None] * inv_freq[None, :]
    cos, sin = np.cos(ang).astype(np.float32), np.sin(ang).astype(np.float32)
    cos_n, sin_n = np.tile(cos, (1, LANES // cos.shape[1])), np.tile(sin, (1, LANES // sin.shape[1]))
    cos_t, sin_t = np.ascontiguousarray(cos.T), np.ascontiguousarray(sin.T)

    for layer in range(depth):
        wq, wk, wv, wqi, wki, wwi, wgb, wgc, wh = _split_cols(w_in[layer].astype(BF16), IN_SPLIT_SIZES)
        pad_ki = jnp.zeros((D_MODEL, LANES - IDX_DIM), BF16)
        wn = jnp.concatenate([wk, _rotate_half_cols(wk), wgb, wgc, wh,
                              wki, pad_ki, _rotate_half_cols(wki), pad_ki], axis=1)
        pad_wi = jnp.zeros((D_MODEL, 2 * SUBLANES - IDX_HEADS), BF16)
        wt = jnp.concatenate([wq, wqi, wv, wwi, pad_wi], axis=1).T
        g = mix_norm_g[layer]
        kn, kin, qt, qit, vt, w_t, cm = _proj_call(
            x, wn, wt, cos_n, sin_n, cos_t, sin_t, conv_w[layer], g[ATT_WIDTH:][None, :])
        am = _dsa_call(kn, kin, vt, qt, qit, w_t, g[:ATT_WIDTH][None, :], topk)
        y = _ffn_call(
            x.reshape(bsz * seq, D_MODEL), am.reshape(bsz * seq, ATT_WIDTH), cm.reshape(bsz * seq, CONV_WIDTH),
            w_out[layer].astype(BF16), ln1_g[layer][None, :], ln1_b[layer][None, :],
            w_gate_up[layer].astype(BF16), w_down[layer].astype(BF16),
            ln2_g[layer][None, :], ln2_b[layer][None, :], alpha)
        x = y.reshape(bsz, seq, D_MODEL)
    return x
```

```python
import functools

import jax
import numpy as np
import jax.numpy as jnp
from jax import lax
from jax.experimental import pallas as pl
from jax.experimental.pallas import tpu as pltpu

D_MODEL = 1024
ATT_HEADS = 8
HEAD_DIM = 64
ATT_WIDTH = ATT_HEADS * HEAD_DIM
CONV_WIDTH = D_MODEL - ATT_WIDTH
CONV_K = 3
IDX_HEADS = 8
IDX_DIM = 64
INDEX_TOPK_MAX = 256
ROPE_THETA = 10000.0
NORM_EPS = 1e-5
IN_SPLIT_SIZES = (ATT_WIDTH, ATT_WIDTH, ATT_WIDTH, IDX_HEADS * IDX_DIM, IDX_DIM, IDX_HEADS,
                  CONV_WIDTH, CONV_WIDTH, CONV_WIDTH)

LANES = 128
SUBLANES = 8
SEQ_TILE = 256
PROJ_GROUPS = 2
ROW_TILE = 512
FFN_SPLIT = 2
SCORE_UNROLL = 4
COUNT_UNROLL = 4
COUNT_ACCS = 4
SORT_GROUP = 16
ATTN_UNROLLS = (4, 2, 1)
V_ROWS = 80
LOG2E = 1.4426950408889634
VMEM_LIMIT = 56 * 1024 * 1024

F32 = jnp.float32
BF16 = jnp.bfloat16
INT_MIN = -2 ** 31
KEY_NEG_INF = (0xFF800000 ^ 0x7FFFFFFF) - 2 ** 32
KEY_POS_INF = 0x7F800000


def _proj_kernel(x_ref, wn_ref, wt_ref, cosn_ref, sinn_ref, cost_ref, sint_ref, convw_ref, gconv_ref,
                 kn_ref, kin_ref, qt_ref, qit_ref, vt_ref, wt_out_ref, cm_ref, u_ref, *, w_scale):
    t = pl.program_id(1)
    tm = qt_ref.shape[3]
    half = HEAD_DIM // 2
    q_scale = (HEAD_DIM ** -0.5) * LOG2E
    o_ki = 2 * ATT_WIDTH + 3 * CONV_WIDTH
    o_v = ATT_WIDTH + IDX_HEADS * IDX_DIM
    o_c = 2 * ATT_WIDTH
    pad_row = lax.broadcasted_iota(jnp.int32, (V_ROWS - HEAD_DIM, tm), 0)
    ones_pad = jnp.where(pad_row == 0, 1.0, 0.0).astype(BF16)
    cw = convw_ref[...]

    @pl.when(t == 0)
    def _():
        u_ref[0:SUBLANES, :] = jnp.zeros((SUBLANES, CONV_WIDTH), F32)

    xbs = [x_ref[0, g * tm:(g + 1) * tm, :].astype(BF16) for g in range(PROJ_GROUPS)]
    nats = [jnp.dot(xb, wn_ref[...], preferred_element_type=F32) for xb in xbs]
    trs = [lax.dot_general(wt_ref[...], xb, (((1,), (1,)), ((), ())), preferred_element_type=F32)
           for xb in xbs]

    for g in range(PROJ_GROUPS):
        nat, tr = nats[g], trs[g]
        rows = slice(g * tm, (g + 1) * tm)

        cos_n = cosn_ref[rows, :]
        sin_n = sinn_ref[rows, :]
        cos4 = jnp.concatenate([cos_n] * (ATT_WIDTH // LANES), axis=1)
        sin4 = jnp.concatenate([sin_n] * (ATT_WIDTH // LANES), axis=1)
        k_rope = nat[:, 0:ATT_WIDTH] * cos4 + nat[:, ATT_WIDTH:2 * ATT_WIDTH] * sin4
        kn_ref[0, rows, :] = k_rope.astype(BF16)
        ki_rope = nat[:, o_ki:o_ki + LANES] * cos_n + nat[:, o_ki + LANES:o_ki + 2 * LANES] * sin_n
        kin_ref[0, rows, :] = ki_rope[:, :IDX_DIM].astype(BF16)

        cos_t = cost_ref[:, rows]
        sin_t = sint_ref[:, rows]
        for h in range(ATT_HEADS):
            r0 = h * HEAD_DIM
            x1 = tr[r0:r0 + half]
            x2 = tr[r0 + half:r0 + HEAD_DIM]
            qt_ref[0, g, r0:r0 + half, :] = ((x1 * cos_t - x2 * sin_t) * q_scale).astype(BF16)
            qt_ref[0, g, r0 + half:r0 + HEAD_DIM, :] = ((x2 * cos_t + x1 * sin_t) * q_scale).astype(BF16)
        for h in range(IDX_HEADS):
            r0 = ATT_WIDTH + h * IDX_DIM
            x1 = tr[r0:r0 + half]
            x2 = tr[r0 + half:r0 + IDX_DIM]
            qit_ref[0, g, h * IDX_DIM:h * IDX_DIM + half, :] = (x1 * cos_t - x2 * sin_t).astype(BF16)
            qit_ref[0, g, h * IDX_DIM + half:(h + 1) * IDX_DIM, :] = (x2 * cos_t + x1 * sin_t).astype(BF16)
        for h in range(ATT_HEADS):
            vt_ref[0, g, h * V_ROWS:h * V_ROWS + HEAD_DIM, :] = (
                tr[o_v + h * HEAD_DIM:o_v + (h + 1) * HEAD_DIM].astype(BF16))
            vt_ref[0, g, h * V_ROWS + HEAD_DIM:(h + 1) * V_ROWS, :] = ones_pad
        wt_out_ref[0, g] = tr[o_v + ATT_WIDTH:o_v + ATT_WIDTH + IDX_HEADS] * w_scale

        gate_b = nat[:, o_c:o_c + CONV_WIDTH]
        u = nat[:, o_c + CONV_WIDTH:o_c + 2 * CONV_WIDTH] * nat[:, o_c + 2 * CONV_WIDTH:o_c + 3 * CONV_WIDTH]
        base = SUBLANES + g * tm
        u_ref[base:base + tm, :] = u
        um1 = u_ref[base - 1:base - 1 + tm, :]
        um2 = u_ref[base - 2:base - 2 + tm, :]
        conv_out = gate_b * (cw[0:1, :] * um2 + cw[1:2, :] * um1 + cw[2:3, :] * u)
        ms = jnp.mean(conv_out * conv_out, axis=-1, keepdims=True)
        cm_ref[0, rows, :] = (conv_out * lax.rsqrt(ms + NORM_EPS) * gconv_ref[...]).astype(BF16)

    tail = SUBLANES + PROJ_GROUPS * tm
    u_ref[0:SUBLANES, :] = u_ref[tail - SUBLANES:tail, :]


def _proj_call(x, wn, wt, cos_n, sin_n, cos_t, sin_t, conv_w, g_conv):
    bsz, seq, _ = x.shape
    tm = SEQ_TILE
    nt = seq // tm
    gm = PROJ_GROUPS * tm
    n_nat = wn.shape[1]
    n_tr = wt.shape[0]
    w_scale = (IDX_DIM ** -0.5) * (IDX_HEADS ** -0.5)
    const2 = lambda b, t: (0, 0)
    out_shapes = (
        jax.ShapeDtypeStruct((bsz, seq, ATT_WIDTH), BF16),
        jax.ShapeDtypeStruct((bsz, seq, IDX_DIM), BF16),
        jax.ShapeDtypeStruct((bsz, nt, ATT_WIDTH, tm), BF16),
        jax.ShapeDtypeStruct((bsz, nt, IDX_HEADS * IDX_DIM, tm), BF16),
        jax.ShapeDtypeStruct((bsz, nt, ATT_HEADS * V_ROWS, tm), BF16),
        jax.ShapeDtypeStruct((bsz, nt, IDX_HEADS, tm), F32),
        jax.ShapeDtypeStruct((bsz, seq, CONV_WIDTH), BF16),
    )
    tile4 = lambda b, t: (b, t, 0, 0)
    return pl.pallas_call(
        functools.partial(_proj_kernel, w_scale=w_scale),
        grid=(bsz, nt // PROJ_GROUPS),
        in_specs=[
            pl.BlockSpec((1, gm, D_MODEL), lambda b, t: (b, t, 0)),
            pl.BlockSpec((D_MODEL, n_nat), const2),
            pl.BlockSpec((n_tr, D_MODEL), const2),
            pl.BlockSpec((gm, LANES), lambda b, t: (t, 0)),
            pl.BlockSpec((gm, LANES), lambda b, t: (t, 0)),
            pl.BlockSpec((HEAD_DIM // 2, gm), lambda b, t: (0, t)),
            pl.BlockSpec((HEAD_DIM // 2, gm), lambda b, t: (0, t)),
            pl.BlockSpec((CONV_K, CONV_WIDTH), const2),
            pl.BlockSpec((1, CONV_WIDTH), const2),
        ],
        out_specs=(
            pl.BlockSpec((1, gm, ATT_WIDTH), lambda b, t: (b, t, 0)),
            pl.BlockSpec((1, gm, IDX_DIM), lambda b, t: (b, t, 0)),
            pl.BlockSpec((1, PROJ_GROUPS, ATT_WIDTH, tm), tile4),
            pl.BlockSpec((1, PROJ_GROUPS, IDX_HEADS * IDX_DIM, tm), tile4),
            pl.BlockSpec((1, PROJ_GROUPS, ATT_HEADS * V_ROWS, tm), tile4),
            pl.BlockSpec((1, PROJ_GROUPS, IDX_HEADS, tm), tile4),
            pl.BlockSpec((1, gm, CONV_WIDTH), lambda b, t: (b, t, 0)),
        ),
        out_shape=out_shapes,
        scratch_shapes=[pltpu.VMEM((gm + SUBLANES, CONV_WIDTH), F32)],
        compiler_params=pltpu.CompilerParams(
            dimension_semantics=("arbitrary", "arbitrary"), vmem_limit_bytes=VMEM_LIMIT),
        name="proj_rope_conv",
    )(x, wn, wt, cos_n, sin_n, cos_t, sin_t, conv_w, g_conv)


def _key_to_float(key):
    key = jnp.clip(key, KEY_NEG_INF, KEY_POS_INF)
    bits = jnp.where(key >= 0, key, key ^ 0x7FFFFFFF)
    return lax.bitcast_convert_type(bits, F32)


def _after(value, anchor):
    return jnp.maximum(value, jnp.minimum(anchor, -jnp.inf))


def _sorting_network(n):
    pairs = []

    def merge(lo, size, r):
        step = 2 * r
        if step < size:
            merge(lo, size, step)
            merge(lo + r, size, step)
            pairs.extend((i, i + r) for i in range(lo + r, lo + size - r, step))
        else:
            pairs.append((lo, lo + r))

    def sort(lo, size):
        if size > 1:
            sort(lo, size // 2)
            sort(lo + size // 2, size // 2)
            merge(lo, size, 1)

    sort(0, n)
    return pairs


def _sorted_group_count(grp, pred):
    size = len(grp)
    levels = size.bit_length() - 1
    masks = []

    def pick(cands, decisions):
        if len(cands) == 1:
            return cands[0]
        mid = len(cands) // 2
        return jnp.where(decisions[0], pick(cands[mid:], decisions[1:]), pick(cands[:mid], decisions[1:]))

    for level in range(1, levels + 1):
        step = size >> level
        cands = [grp[(2 * t + 1) * step - 1] for t in range(1 << (level - 1))]
        masks.append(pred(pick(cands, masks)))
    every = pred(grp[size - 1])
    count = jnp.where(every, 2, jnp.where(masks[levels - 1], 1, 0))
    for level in range(1, levels):
        count = count + jnp.where(masks[level - 1], size >> level, 0)
    return count


def _fold_rows(a):
    return a.reshape(a.shape[0] // SUBLANES, SUBLANES, a.shape[1])


def _dsa_kernel(kn_ref, kin_ref, vt_ref, qt_ref, qit_ref, w_ref, g_ref, out_ref,
                score_ref, sorted_ref, wbd_ref, acc_ref, at_ref, lg_ref, *, topk, seq):
    i = pl.program_id(1)
    tq = qt_ref.shape[3]
    tk = tq
    nchunks = i + 1
    neg_inf = jnp.float32(-jnp.inf)

    row_iota = lax.broadcasted_iota(jnp.int32, (tk, tq), 0)
    qpos = i * tq + lax.broadcasted_iota(jnp.int32, (tk, tq), 1)

    def chunk_off(c):
        return pl.multiple_of(c * tk, tk)

    qi_all = qit_ref[0, 0]
    w_rows = w_ref[0, 0]

    def score_body(c, carry):
        off = chunk_off(c)
        kic = kin_ref[0, pl.ds(off, tk), :]
        sc = jnp.zeros((tk, tq), F32)
        for h in range(IDX_HEADS):
            d = jnp.dot(kic, qi_all[h * IDX_DIM:(h + 1) * IDX_DIM, :], preferred_element_type=F32)
            sc = sc + jnp.maximum(d, 0.0) * w_rows[h:h + 1, :]
        sc = jnp.where(off + row_iota <= qpos, sc, neg_inf)
        score_ref[pl.ds(off, tk), :] = sc
        rows = _fold_rows(sc)
        sorted_rows = []
        for g in range(tk // SUBLANES // SORT_GROUP):
            grp = [rows[SORT_GROUP * g + r] for r in range(SORT_GROUP)]
            for a, b in _sorting_network(SORT_GROUP):
                grp[a], grp[b] = jnp.maximum(grp[a], grp[b]), jnp.minimum(grp[a], grp[b])
            sorted_rows.extend(grp)
        sorted_ref[pl.ds(off, tk), :] = jnp.stack(sorted_rows).reshape(tk, tq)
        return carry

    def score_unrolled(cu, carry):
        for k in range(SCORE_UNROLL):
            carry = score_body(SCORE_UNROLL * cu + k, carry)
        return carry

    lax.fori_loop(0, nchunks // SCORE_UNROLL, score_unrolled, 0)
    lax.fori_loop(SCORE_UNROLL * (nchunks // SCORE_UNROLL), nchunks, score_body, 0)

    row_iota3 = (lax.broadcasted_iota(jnp.int32, (tk // SUBLANES, SUBLANES, tq), 0) * SUBLANES
                 + lax.broadcasted_iota(jnp.int32, (tk // SUBLANES, SUBLANES, tq), 1))
    qpos3 = i * tq + lax.broadcasted_iota(jnp.int32, (tk // SUBLANES, SUBLANES, tq), 2)

    def count_total(chunk_partials, n_parts):
        def count_chunk(c, accs):
            return tuple(acc + part for acc, part in zip(accs, chunk_partials(c)))

        def unrolled(cu, accs):
            for k in range(COUNT_UNROLL):
                accs = count_chunk(COUNT_UNROLL * cu + k, accs)
            return accs

        n_main = nchunks // COUNT_UNROLL
        accs = lax.fori_loop(0, n_main, unrolled, (jnp.zeros((SUBLANES, tq), jnp.int32),) * n_parts)
        accs = lax.fori_loop(n_main * COUNT_UNROLL, nchunks, count_chunk, accs)
        acc = functools.reduce(lambda x, y: x + y, accs)
        for shift in (4, 2, 1):
            acc = acc + pltpu.roll(acc, shift, axis=0)
        return acc

    def count_rows(pred_fn):
        def partials(c):
            off = chunk_off(c)
            m = pred_fn(_fold_rows(score_ref[pl.ds(off, tk), :]), off).astype(jnp.int32)
            per = m.shape[0] // COUNT_ACCS
            return [jnp.sum(m[g * per:(g + 1) * per], axis=0) for g in range(COUNT_ACCS)]
        return count_total(partials, COUNT_ACCS)

    blocks_per_chunk = tk // SUBLANES // SORT_GROUP

    def count_sorted(pred):
        def partials(c):
            rows = _fold_rows(sorted_ref[pl.ds(chunk_off(c), tk), :])
            return [_sorted_group_count([rows[SORT_GROUP * g + r] for r in range(SORT_GROUP)], pred)
                    for g in range(blocks_per_chunk)]
        return count_total(partials, blocks_per_chunk)

    def search_body(p, carry):
        u, n_u = carry
        trial = u | jnp.left_shift(jnp.int32(1), 31 - p)
        cand = _key_to_float(trial ^ INT_MIN)
        cnt = count_sorted(lambda sc: sc >= cand)
        take = cnt >= topk
        return jnp.where(take, trial, u), jnp.where(take, cnt, n_u)

    u, n_ge = lax.fori_loop(0, 32, search_body,
                            (jnp.zeros((SUBLANES, tq), jnp.int32),
                             jnp.zeros((SUBLANES, tq), jnp.int32) + nchunks * tk))
    thr = _key_to_float(u ^ INT_MIN)

    n_gt = count_sorted(lambda sc: sc > thr)
    want = topk - n_gt
    has_excess = jnp.max(n_ge - n_gt - want) > 0

    def resolve_ties():
        def body(p, jp):
            trial = jp + jnp.left_shift(jnp.int32(1), (seq - 1).bit_length() - 1 - p)
            cnt = count_rows(lambda sc, off: (sc == thr[None]) & (off + row_iota3 <= trial[None]))
            return jnp.where(cnt < want, trial, jp)
        jlim = lax.fori_loop(0, (seq - 1).bit_length(), body, jnp.full((SUBLANES, tq), -1, jnp.int32)) + 1

        def rewrite(c, carry):
            off = chunk_off(c)
            sc = _fold_rows(score_ref[pl.ds(off, tk), :])
            sel = (sc > thr[None]) | ((sc == thr[None]) & (off + row_iota3 <= jlim[None]))
            score_ref[pl.ds(off, tk), :] = jnp.where(sel, 1.0, -1.0).reshape(tk, tq)
            return carry

        lax.fori_loop(0, nchunks, rewrite, 0)
        return jnp.zeros((SUBLANES, tq), F32)

    thr_sel = lax.cond(has_excess, resolve_ties, lambda: thr)

    q_all = qt_ref[0, 0]
    zeros_q = jnp.zeros((HEAD_DIM, tq), BF16)
    n_pairs = ATT_HEADS // 2
    for j in range(n_pairs):
        r0 = 2 * j * HEAD_DIM
        wbd_ref[j] = jnp.concatenate([
            jnp.concatenate([q_all[r0:r0 + HEAD_DIM], zeros_q], axis=1),
            jnp.concatenate([zeros_q, q_all[r0 + HEAD_DIM:r0 + 2 * HEAD_DIM]], axis=1)], axis=0)
    acc_ref[...] = jnp.zeros(acc_ref.shape, F32)

    def attn_chunks(chunks, m_all):
        units = [(k, j) for k in range(len(chunks)) for j in range(n_pairs)]
        bias2 = []
        for c in chunks:
            off = chunk_off(c)
            sel = (_fold_rows(score_ref[pl.ds(off, tk), :]) >= thr_sel[None]) & (off + row_iota3 <= qpos3)
            bias = jnp.where(sel, 0.0, neg_inf).reshape(tk, tq)
            bias2.append(jnp.concatenate([bias, bias], axis=1))

        def unit_logits(u):
            k, j = units[u]
            kc = kn_ref[0, pl.ds(chunk_off(chunks[k]), tk), 2 * j * HEAD_DIM:2 * (j + 1) * HEAD_DIM]
            return jnp.dot(kc, wbd_ref[j], preferred_element_type=F32)

        m_run = [m_all[:, 2 * j * tq:2 * (j + 1) * tq] for j in range(n_pairs)]
        lgs_raw = [unit_logits(0)]
        stage_a = []
        for u, (k, j) in enumerate(units):
            lg = lgs_raw[u] + bias2[k]
            m_old = m_run[j]
            if u + 1 < len(units):
                lgs_raw.append(unit_logits(u + 1))
                m_old = _after(m_old, lgs_raw[u + 1][0:1, :])
            m_new = jnp.maximum(m_old, jnp.max(jnp.max(_fold_rows(lg), axis=0), axis=0, keepdims=True))
            m_run[j] = m_new
            lg_ref[u] = lg
            stage_a.append((m_old, m_new))
        updates = []
        for u, (k, j) in enumerate(units):
            m_old, m_new = stage_a[u]
            m_use = jnp.where(m_new == neg_inf, 0.0, m_new)
            if u + 1 < len(units):
                m_use = _after(m_use, stage_a[u + 1][1])
            alpha = jnp.exp2(m_old - m_use)
            p = jnp.exp2(lg_ref[u] - m_use).astype(BF16)
            for hh in range(2):
                h = 2 * j + hh
                pv = jnp.dot(vt_ref[0, chunks[k], h * V_ROWS:(h + 1) * V_ROWS, :], p[:, hh * tq:(hh + 1) * tq],
                             preferred_element_type=F32)
                updates.append((h, alpha[:, hh * tq:(hh + 1) * tq], pv))
        for h, a, pv in updates:
            acc_ref[h] = acc_ref[h] * a + pv
        return jnp.concatenate(m_run, axis=1)

    m_all = jnp.full((1, ATT_HEADS * tq), neg_inf, F32)
    start = 0
    for width in ATTN_UNROLLS:
        trips = (nchunks - start) // width
        m_all = lax.fori_loop(
            0, trips, lambda it, m, s=start, w=width: attn_chunks([s + w * it + k for k in range(w)], m), m_all)
        start = start + trips * width
    for h in range(ATT_HEADS):
        o = acc_ref[h]
        at_ref[h * HEAD_DIM:(h + 1) * HEAD_DIM, :] = o[:HEAD_DIM] / o[HEAD_DIM:HEAD_DIM + 1]

    a_t = at_ref[...]
    ms = jnp.mean(a_t * a_t, axis=0, keepdims=True)
    y = (a_t * lax.rsqrt(ms + NORM_EPS)).T
    out_ref[0] = (y * g_ref[...]).astype(BF16)


def _dsa_call(kn, kin, vt, qt, qit, w_t, g_attn, topk):
    bsz, seq, _ = kn.shape
    nt, tq = qt.shape[1], qt.shape[3]
    tile4 = lambda b, i: (b, i, 0, 0)
    return pl.pallas_call(
        functools.partial(_dsa_kernel, topk=topk, seq=seq),
        grid=(bsz, nt),
        in_specs=[
            pl.BlockSpec((1, seq, ATT_WIDTH), lambda b, i: (b, 0, 0)),
            pl.BlockSpec((1, seq, IDX_DIM), lambda b, i: (b, 0, 0)),
            pl.BlockSpec((1, nt, ATT_HEADS * V_ROWS, tq), lambda b, i: (b, 0, 0, 0)),
            pl.BlockSpec((1, 1, ATT_WIDTH, tq), tile4),
            pl.BlockSpec((1, 1, IDX_HEADS * IDX_DIM, tq), tile4),
            pl.BlockSpec((1, 1, IDX_HEADS, tq), tile4),
            pl.BlockSpec((1, ATT_WIDTH), lambda b, i: (0, 0)),
        ],
        out_specs=pl.BlockSpec((1, tq, ATT_WIDTH), lambda b, i: (b, i, 0)),
        out_shape=jax.ShapeDtypeStruct((bsz, seq, ATT_WIDTH), BF16),
        scratch_shapes=[
            pltpu.VMEM((seq, tq), F32),
            pltpu.VMEM((seq, tq), F32),
            pltpu.VMEM((ATT_HEADS // 2, 2 * HEAD_DIM, 2 * tq), BF16),
            pltpu.VMEM((ATT_HEADS, V_ROWS, tq), F32),
            pltpu.VMEM((ATT_WIDTH, tq), F32),
            pltpu.VMEM((max(ATTN_UNROLLS) * (ATT_HEADS // 2), tq, 2 * tq), F32),
        ],
        compiler_params=pltpu.CompilerParams(
            dimension_semantics=("arbitrary", "arbitrary"), vmem_limit_bytes=VMEM_LIMIT),
        name="dsa_attention",
    )(kn, kin, vt, qt, qit, w_t, g_attn)


def _layer_norm(y, g, b):
    mu = jnp.mean(y, axis=-1, keepdims=True)
    d = y - mu
    var = jnp.mean(d * d, axis=-1, keepdims=True)
    return d * lax.rsqrt(var + NORM_EPS) * g + b


def _ffn_kernel(x_ref, am_ref, cm_ref, wout_ref, g1_ref, b1_ref, wgu_ref, wd_ref, g2_ref, b2_ref,
                o_ref, *, alpha, d_ff):
    n = FFN_SPLIT
    group = x_ref.shape[0] // n
    rows = [slice(r * group, (r + 1) * group) for r in range(n)]
    mix = [jnp.dot(am_ref[rw, :], wout_ref[0:ATT_WIDTH, :], preferred_element_type=F32)
           + jnp.dot(cm_ref[rw, :], wout_ref[ATT_WIDTH:D_MODEL, :], preferred_element_type=F32) for rw in rows]
    x1 = []
    for r in range(n):
        b1 = b1_ref[...] if r + 1 == n else _after(b1_ref[...], mix[r + 1][0:1, :])
        x1.append(_layer_norm(alpha * x_ref[rows[r], :] + mix[r], g1_ref[...], b1))
    gu = [jnp.dot(x1[r].astype(BF16), wgu_ref[...], preferred_element_type=F32) for r in range(n)]
    ffn = []
    for r in range(n):
        gate = gu[r][:, :d_ff]
        hidden = (gate * jax.nn.sigmoid(gate) * gu[r][:, d_ff:]).astype(BF16)
        ffn.append(jnp.dot(hidden, wd_ref[...], preferred_element_type=F32))
    for r in range(n):
        b2 = b2_ref[...] if r + 1 == n else _after(b2_ref[...], ffn[r + 1][0:1, :])
        o_ref[rows[r], :] = _layer_norm(alpha * x1[r] + ffn[r], g2_ref[...], b2)


def _ffn_call(x2d, am, cm, wout, g1, b1, wgu, wd, g2, b2, alpha):
    rows = x2d.shape[0]
    tm = ROW_TILE
    d_ff = wd.shape[0]
    row = lambda r: (r, 0)
    const = lambda r: (0, 0)
    resident = functools.partial(pl.BlockSpec, index_map=const, pipeline_mode=pl.Buffered(1))
    return pl.pallas_call(
        functools.partial(_ffn_kernel, alpha=alpha, d_ff=d_ff),
        grid=(rows // tm,),
        in_specs=[
            pl.BlockSpec((tm, D_MODEL), row),
            pl.BlockSpec((tm, ATT_WIDTH), row),
            pl.BlockSpec((tm, CONV_WIDTH), row),
            resident((D_MODEL, D_MODEL)),
            resident((1, D_MODEL)),
            resident((1, D_MODEL)),
            resident((D_MODEL, 2 * d_ff)),
            resident((d_ff, D_MODEL)),
            resident((1, D_MODEL)),
            resident((1, D_MODEL)),
        ],
        out_specs=pl.BlockSpec((tm, D_MODEL), row),
        out_shape=jax.ShapeDtypeStruct((rows, D_MODEL), F32),
        compiler_params=pltpu.CompilerParams(
            dimension_semantics=("arbitrary",), vmem_limit_bytes=VMEM_LIMIT),
        name="outproj_ffn",
    )(x2d, am, cm, wout, g1, b1, wgu, wd, g2, b2)


def _rotate_half_cols(w):
    d_in, n = w.shape
    w4 = w.reshape(d_in, n // HEAD_DIM, 2, HEAD_DIM // 2)
    return jnp.concatenate([-w4[:, :, 1:2, :], w4[:, :, 0:1, :]], axis=2).reshape(d_in, n)


def _split_cols(w, sizes):
    out, o = [], 0
    for s in sizes:
        out.append(w[:, o:o + s])
        o += s
    return out


def kernel(x, w_in, conv_w, mix_norm_g, w_out, ln1_g, ln1_b, w_gate_up, w_down, ln2_g, ln2_b):
    bsz, seq, _ = x.shape
    depth = w_in.shape[0]
    alpha = (2 * depth) ** 0.25
    topk = min(INDEX_TOPK_MAX, seq // 4)
    assert seq % (SEQ_TILE * PROJ_GROUPS) == 0 and (bsz * seq) % ROW_TILE == 0 and topk <= SEQ_TILE

    inv_freq = 1.0 / (ROPE_THETA ** (np.arange(0, HEAD_DIM, 2, dtype=np.float64) / HEAD_DIM))
    ang = np.arange(seq, dtype=np.float64)[:, None] * inv_freq[None, :]
    cos, sin = np.cos(ang).astype(np.float32), np.sin(ang).astype(np.float32)
    cos_n, sin_n = np.tile(cos, (1, LANES // cos.shape[1])), np.tile(sin, (1, LANES // sin.shape[1]))
    cos_t, sin_t = np.ascontiguousarray(cos.T), np.ascontiguousarray(sin.T)

    for layer in range(depth):
        wq, wk, wv, wqi, wki, wwi, wgb, wgc, wh = _split_cols(w_in[layer].astype(BF16), IN_SPLIT_SIZES)
        pad_ki = jnp.zeros((D_MODEL, LANES - IDX_DIM), BF16)
        wn = jnp.concatenate([wk, _rotate_half_cols(wk), wgb, wgc, wh,
                              wki, pad_ki, _rotate_half_cols(wki), pad_ki], axis=1)
        pad_wi = jnp.zeros((D_MODEL, 2 * SUBLANES - IDX_HEADS), BF16)
        wt = jnp.concatenate([wq, wqi, wv, wwi, pad_wi], axis=1).T
        g = mix_norm_g[layer]
        kn, kin, qt, qit, vt, w_t, cm = _proj_call(
            x, wn, wt, cos_n, sin_n, cos_t, sin_t, conv_w[layer], g[ATT_WIDTH:][None, :])
        am = _dsa_call(kn, kin, vt, qt, qit, w_t, g[:ATT_WIDTH][None, :], topk)
        y = _ffn_call(
            x.reshape(bsz * seq, D_MODEL), am.reshape(bsz * seq, ATT_WIDTH), cm.reshape(bsz * seq, CONV_WIDTH),
            w_out[layer].astype(BF16), ln1_g[layer][None, :], ln1_b[layer][None, :],
            w_gate_up[layer].astype(BF16), w_down[layer].astype(BF16),
            ln2_g[layer][None, :], ln2_b[layer][None, :], alpha)
        x = y.reshape(bsz, seq, D_MODEL)
    return x
```

```python
import functools

import jax
import numpy as np
import jax.numpy as jnp
from jax import lax
from jax.experimental import pallas as pl
from jax.experimental.pallas import tpu as pltpu

D_MODEL = 1024
ATT_HEADS = 8
HEAD_DIM = 64
ATT_WIDTH = ATT_HEADS * HEAD_DIM
CONV_WIDTH = D_MODEL - ATT_WIDTH
CONV_K = 3
IDX_HEADS = 8
IDX_DIM = 64
INDEX_TOPK_MAX = 256
ROPE_THETA = 10000.0
NORM_EPS = 1e-5
IN_SPLIT_SIZES = (ATT_WIDTH, ATT_WIDTH, ATT_WIDTH, IDX_HEADS * IDX_DIM, IDX_DIM, IDX_HEADS,
                  CONV_WIDTH, CONV_WIDTH, CONV_WIDTH)

LANES = 128
SUBLANES = 8
SEQ_TILE = 256
PROJ_GROUPS = 2
ROW_TILE = 512
FFN_SPLIT = 2
SCORE_UNROLL = 4
COUNT_UNROLL = 4
COUNT_ACCS = 4
SORT_GROUP = 16
ATTN_UNROLLS = (4, 2, 1)
HEAD_GROUP = 4
V_ROWS = 80
LOG2E = 1.4426950408889634
VMEM_LIMIT = 56 * 1024 * 1024

F32 = jnp.float32
BF16 = jnp.bfloat16
INT_MIN = -2 ** 31
KEY_NEG_INF = (0xFF800000 ^ 0x7FFFFFFF) - 2 ** 32
KEY_POS_INF = 0x7F800000


def _proj_kernel(x_ref, wn_ref, wt_ref, cosn_ref, sinn_ref, cost_ref, sint_ref, convw_ref, gconv_ref,
                 kn_ref, kin_ref, qt_ref, qit_ref, vt_ref, wt_out_ref, cm_ref, u_ref, *, w_scale):
    t = pl.program_id(1)
    tm = qt_ref.shape[3]
    half = HEAD_DIM // 2
    q_scale = (HEAD_DIM ** -0.5) * LOG2E
    o_ki = ATT_WIDTH + 3 * CONV_WIDTH
    o_v = ATT_WIDTH + IDX_HEADS * IDX_DIM
    o_c = ATT_WIDTH
    pad_row = lax.broadcasted_iota(jnp.int32, (V_ROWS - HEAD_DIM, tm), 0)
    ones_pad = jnp.where(pad_row == 0, 1.0, 0.0).astype(BF16)
    cw = convw_ref[...]

    @pl.when(t == 0)
    def _():
        u_ref[0:SUBLANES, :] = jnp.zeros((SUBLANES, CONV_WIDTH), F32)

    xbs = [x_ref[0, g * tm:(g + 1) * tm, :].astype(BF16) for g in range(PROJ_GROUPS)]
    nats = [jnp.dot(xb, wn_ref[...], preferred_element_type=F32) for xb in xbs]
    trs = [lax.dot_general(wt_ref[...], xb, (((1,), (1,)), ((), ())), preferred_element_type=F32)
           for xb in xbs]

    for g in range(PROJ_GROUPS):
        nat, tr = nats[g], trs[g]
        rows = slice(g * tm, (g + 1) * tm)

        cos_n = cosn_ref[rows, :]
        sin_n = sinn_ref[rows, :]
        for grp in range(ATT_HEADS // HEAD_GROUP):
            c0 = grp * 2 * LANES
            x1 = nat[:, c0:c0 + LANES]
            x2 = nat[:, c0 + LANES:c0 + 2 * LANES]
            kn_ref[0, rows, c0:c0 + LANES] = (x1 * cos_n - x2 * sin_n).astype(BF16)
            kn_ref[0, rows, c0 + LANES:c0 + 2 * LANES] = (x2 * cos_n + x1 * sin_n).astype(BF16)
        ki_rope = nat[:, o_ki:o_ki + LANES] * cos_n + nat[:, o_ki + LANES:o_ki + 2 * LANES] * sin_n
        kin_ref[0, rows, :] = ki_rope[:, :IDX_DIM].astype(BF16)

        cos_t = cost_ref[:, rows]
        sin_t = sint_ref[:, rows]
        for h in range(ATT_HEADS):
            r0 = h * HEAD_DIM
            x1 = tr[r0:r0 + half]
            x2 = tr[r0 + half:r0 + HEAD_DIM]
            qt_ref[0, g, r0:r0 + half, :] = ((x1 * cos_t - x2 * sin_t) * q_scale).astype(BF16)
            qt_ref[0, g, r0 + half:r0 + HEAD_DIM, :] = ((x2 * cos_t + x1 * sin_t) * q_scale).astype(BF16)
        for h in range(IDX_HEADS):
            r0 = ATT_WIDTH + h * IDX_DIM
            x1 = tr[r0:r0 + half]
            x2 = tr[r0 + half:r0 + IDX_DIM]
            qit_ref[0, g, h * IDX_DIM:h * IDX_DIM + half, :] = (x1 * cos_t - x2 * sin_t).astype(BF16)
            qit_ref[0, g, h * IDX_DIM + half:(h + 1) * IDX_DIM, :] = (x2 * cos_t + x1 * sin_t).astype(BF16)
        for h in range(ATT_HEADS):
            vt_ref[0, g, h * V_ROWS:h * V_ROWS + HEAD_DIM, :] = (
                tr[o_v + h * HEAD_DIM:o_v + (h + 1) * HEAD_DIM].astype(BF16))
            vt_ref[0, g, h * V_ROWS + HEAD_DIM:(h + 1) * V_ROWS, :] = ones_pad
        wt_out_ref[0, g] = tr[o_v + ATT_WIDTH:o_v + ATT_WIDTH + IDX_HEADS] * w_scale

        gate_b = nat[:, o_c:o_c + CONV_WIDTH]
        u = nat[:, o_c + CONV_WIDTH:o_c + 2 * CONV_WIDTH] * nat[:, o_c + 2 * CONV_WIDTH:o_c + 3 * CONV_WIDTH]
        base = SUBLANES + g * tm
        u_ref[base:base + tm, :] = u
        um1 = u_ref[base - 1:base - 1 + tm, :]
        um2 = u_ref[base - 2:base - 2 + tm, :]
        conv_out = gate_b * (cw[0:1, :] * um2 + cw[1:2, :] * um1 + cw[2:3, :] * u)
        ms = jnp.mean(conv_out * conv_out, axis=-1, keepdims=True)
        cm_ref[0, rows, :] = (conv_out * lax.rsqrt(ms + NORM_EPS) * gconv_ref[...]).astype(BF16)

    tail = SUBLANES + PROJ_GROUPS * tm
    u_ref[0:SUBLANES, :] = u_ref[tail - SUBLANES:tail, :]


def _proj_call(x, wn, wt, cos_n, sin_n, cos_t, sin_t, conv_w, g_conv):
    bsz, seq, _ = x.shape
    tm = SEQ_TILE
    nt = seq // tm
    gm = PROJ_GROUPS * tm
    n_nat = wn.shape[1]
    n_tr = wt.shape[0]
    w_scale = (IDX_DIM ** -0.5) * (IDX_HEADS ** -0.5)
    const2 = lambda b, t: (0, 0)
    out_shapes = (
        jax.ShapeDtypeStruct((bsz, seq, ATT_WIDTH), BF16),
        jax.ShapeDtypeStruct((bsz, seq, IDX_DIM), BF16),
        jax.ShapeDtypeStruct((bsz, nt, ATT_WIDTH, tm), BF16),
        jax.ShapeDtypeStruct((bsz, nt, IDX_HEADS * IDX_DIM, tm), BF16),
        jax.ShapeDtypeStruct((bsz, nt, ATT_HEADS * V_ROWS, tm), BF16),
        jax.ShapeDtypeStruct((bsz, nt, IDX_HEADS, tm), F32),
        jax.ShapeDtypeStruct((bsz, seq, CONV_WIDTH), BF16),
    )
    tile4 = lambda b, t: (b, t, 0, 0)
    return pl.pallas_call(
        functools.partial(_proj_kernel, w_scale=w_scale),
        grid=(bsz, nt // PROJ_GROUPS),
        in_specs=[
            pl.BlockSpec((1, gm, D_MODEL), lambda b, t: (b, t, 0)),
            pl.BlockSpec((D_MODEL, n_nat), const2),
            pl.BlockSpec((n_tr, D_MODEL), const2),
            pl.BlockSpec((gm, LANES), lambda b, t: (t, 0)),
            pl.BlockSpec((gm, LANES), lambda b, t: (t, 0)),
            pl.BlockSpec((HEAD_DIM // 2, gm), lambda b, t: (0, t)),
            pl.BlockSpec((HEAD_DIM // 2, gm), lambda b, t: (0, t)),
            pl.BlockSpec((CONV_K, CONV_WIDTH), const2),
            pl.BlockSpec((1, CONV_WIDTH), const2),
        ],
        out_specs=(
            pl.BlockSpec((1, gm, ATT_WIDTH), lambda b, t: (b, t, 0)),
            pl.BlockSpec((1, gm, IDX_DIM), lambda b, t: (b, t, 0)),
            pl.BlockSpec((1, PROJ_GROUPS, ATT_WIDTH, tm), tile4),
            pl.BlockSpec((1, PROJ_GROUPS, IDX_HEADS * IDX_DIM, tm), tile4),
            pl.BlockSpec((1, PROJ_GROUPS, ATT_HEADS * V_ROWS, tm), tile4),
            pl.BlockSpec((1, PROJ_GROUPS, IDX_HEADS, tm), tile4),
            pl.BlockSpec((1, gm, CONV_WIDTH), lambda b, t: (b, t, 0)),
        ),
        out_shape=out_shapes,
        scratch_shapes=[pltpu.VMEM((gm + SUBLANES, CONV_WIDTH), F32)],
        compiler_params=pltpu.CompilerParams(
            dimension_semantics=("arbitrary", "arbitrary"), vmem_limit_bytes=VMEM_LIMIT),
        name="proj_rope_conv",
    )(x, wn, wt, cos_n, sin_n, cos_t, sin_t, conv_w, g_conv)


def _key_to_float(key):
    key = jnp.clip(key, KEY_NEG_INF, KEY_POS_INF)
    bits = jnp.where(key >= 0, key, key ^ 0x7FFFFFFF)
    return lax.bitcast_convert_type(bits, F32)


def _after(value, anchor):
    return jnp.maximum(value, jnp.minimum(anchor, -jnp.inf))


def _sorting_network(n):
    pairs = []

    def merge(lo, size, r):
        step = 2 * r
        if step < size:
            merge(lo, size, step)
            merge(lo + r, size, step)
            pairs.extend((i, i + r) for i in range(lo + r, lo + size - r, step))
        else:
            pairs.append((lo, lo + r))

    def sort(lo, size):
        if size > 1:
            sort(lo, size // 2)
            sort(lo + size // 2, size // 2)
            merge(lo, size, 1)

    sort(0, n)
    return pairs


def _sorted_group_count(grp, pred):
    size = len(grp)
    levels = size.bit_length() - 1
    masks = []

    def pick(cands, decisions):
        if len(cands) == 1:
            return cands[0]
        mid = len(cands) // 2
        return jnp.where(decisions[0], pick(cands[mid:], decisions[1:]), pick(cands[:mid], decisions[1:]))

    for level in range(1, levels + 1):
        step = size >> level
        cands = [grp[(2 * t + 1) * step - 1] for t in range(1 << (level - 1))]
        masks.append(pred(pick(cands, masks)))
    every = pred(grp[size - 1])
    count = jnp.where(every, 2, jnp.where(masks[levels - 1], 1, 0))
    for level in range(1, levels):
        count = count + jnp.where(masks[level - 1], size >> level, 0)
    return count


def _fold_rows(a):
    return a.reshape(a.shape[0] // SUBLANES, SUBLANES, a.shape[1])


def _dsa_kernel(kn_ref, kin_ref, vt_ref, qt_ref, qit_ref, w_ref, g_ref, out_ref,
                score_ref, sorted_ref, wbd_ref, acc_ref, at_ref, lg_ref, *, topk, seq):
    i = pl.program_id(1)
    tq = qt_ref.shape[3]
    tk = tq
    nchunks = i + 1
    neg_inf = jnp.float32(-jnp.inf)

    row_iota = lax.broadcasted_iota(jnp.int32, (tk, tq), 0)
    qpos = i * tq + lax.broadcasted_iota(jnp.int32, (tk, tq), 1)

    def chunk_off(c):
        return pl.multiple_of(c * tk, tk)

    qi_all = qit_ref[0, 0]
    w_rows = w_ref[0, 0]

    def score_body(c, carry):
        off = chunk_off(c)
        kic = kin_ref[0, pl.ds(off, tk), :]
        sc = jnp.zeros((tk, tq), F32)
        for h in range(IDX_HEADS):
            d = jnp.dot(kic, qi_all[h * IDX_DIM:(h + 1) * IDX_DIM, :], preferred_element_type=F32)
            sc = sc + jnp.maximum(d, 0.0) * w_rows[h:h + 1, :]
        sc = jnp.where(off + row_iota <= qpos, sc, neg_inf)
        score_ref[pl.ds(off, tk), :] = sc
        rows = _fold_rows(sc)
        sorted_rows = []
        for g in range(tk // SUBLANES // SORT_GROUP):
            grp = [rows[SORT_GROUP * g + r] for r in range(SORT_GROUP)]
            for a, b in _sorting_network(SORT_GROUP):
                grp[a], grp[b] = jnp.maximum(grp[a], grp[b]), jnp.minimum(grp[a], grp[b])
            sorted_rows.extend(grp)
        sorted_ref[pl.ds(off, tk), :] = jnp.stack(sorted_rows).reshape(tk, tq)
        return carry

    def score_unrolled(cu, carry):
        for k in range(SCORE_UNROLL):
            carry = score_body(SCORE_UNROLL * cu + k, carry)
        return carry

    lax.fori_loop(0, nchunks // SCORE_UNROLL, score_unrolled, 0)
    lax.fori_loop(SCORE_UNROLL * (nchunks // SCORE_UNROLL), nchunks, score_body, 0)

    row_iota3 = (lax.broadcasted_iota(jnp.int32, (tk // SUBLANES, SUBLANES, tq), 0) * SUBLANES
                 + lax.broadcasted_iota(jnp.int32, (tk // SUBLANES, SUBLANES, tq), 1))
    qpos3 = i * tq + lax.broadcasted_iota(jnp.int32, (tk // SUBLANES, SUBLANES, tq), 2)

    def count_total(chunk_partials, n_parts):
        def count_chunk(c, accs):
            return tuple(acc + part for acc, part in zip(accs, chunk_partials(c)))

        def unrolled(cu, accs):
            for k in range(COUNT_UNROLL):
                accs = count_chunk(COUNT_UNROLL * cu + k, accs)
            return accs

        n_main = nchunks // COUNT_UNROLL
        accs = lax.fori_loop(0, n_main, unrolled, (jnp.zeros((SUBLANES, tq), jnp.int32),) * n_parts)
        accs = lax.fori_loop(n_main * COUNT_UNROLL, nchunks, count_chunk, accs)
        acc = functools.reduce(lambda x, y: x + y, accs)
        for shift in (4, 2, 1):
            acc = acc + pltpu.roll(acc, shift, axis=0)
        return acc

    def count_rows(pred_fn):
        def partials(c):
            off = chunk_off(c)
            m = pred_fn(_fold_rows(score_ref[pl.ds(off, tk), :]), off).astype(jnp.int32)
            per = m.shape[0] // COUNT_ACCS
            return [jnp.sum(m[g * per:(g + 1) * per], axis=0) for g in range(COUNT_ACCS)]
        return count_total(partials, COUNT_ACCS)

    blocks_per_chunk = tk // SUBLANES // SORT_GROUP

    def count_sorted(pred):
        def partials(c):
            rows = _fold_rows(sorted_ref[pl.ds(chunk_off(c), tk), :])
            return [_sorted_group_count([rows[SORT_GROUP * g + r] for r in range(SORT_GROUP)], pred)
                    for g in range(blocks_per_chunk)]
        return count_total(partials, blocks_per_chunk)

    def search_body(p, carry):
        u, n_u = carry
        trial = u | jnp.left_shift(jnp.int32(1), 31 - p)
        cand = _key_to_float(trial ^ INT_MIN)
        cnt = count_sorted(lambda sc: sc >= cand)
        take = cnt >= topk
        return jnp.where(take, trial, u), jnp.where(take, cnt, n_u)

    u, n_ge = lax.fori_loop(0, 32, search_body,
                            (jnp.zeros((SUBLANES, tq), jnp.int32),
                             jnp.zeros((SUBLANES, tq), jnp.int32) + nchunks * tk))
    thr = _key_to_float(u ^ INT_MIN)

    n_gt = count_sorted(lambda sc: sc > thr)
    want = topk - n_gt
    has_excess = jnp.max(n_ge - n_gt - want) > 0

    def resolve_ties():
        def body(p, jp):
            trial = jp + jnp.left_shift(jnp.int32(1), (seq - 1).bit_length() - 1 - p)
            cnt = count_rows(lambda sc, off: (sc == thr[None]) & (off + row_iota3 <= trial[None]))
            return jnp.where(cnt < want, trial, jp)
        jlim = lax.fori_loop(0, (seq - 1).bit_length(), body, jnp.full((SUBLANES, tq), -1, jnp.int32)) + 1

        def rewrite(c, carry):
            off = chunk_off(c)
            sc = _fold_rows(score_ref[pl.ds(off, tk), :])
            sel = (sc > thr[None]) | ((sc == thr[None]) & (off + row_iota3 <= jlim[None]))
            score_ref[pl.ds(off, tk), :] = jnp.where(sel, 1.0, -1.0).reshape(tk, tq)
            return carry

        lax.fori_loop(0, nchunks, rewrite, 0)
        return jnp.zeros((SUBLANES, tq), F32)

    thr_sel = lax.cond(has_excess, resolve_ties, lambda: thr)

    q_all = qt_ref[0, 0]
    half = HEAD_DIM // 2
    n_pairs = ATT_HEADS // 2
    pairs_per_group = HEAD_GROUP // 2
    for j in range(n_pairs):
        first = 2 * (j % pairs_per_group)
        row_blocks = []
        for part in range(2):
            for hh in range(HEAD_GROUP):
                r0 = ((j // pairs_per_group) * HEAD_GROUP + hh) * HEAD_DIM + part * half
                pieces = [q_all[r0:r0 + half] if hh == first + col else jnp.zeros((half, tq), BF16)
                          for col in range(2)]
                row_blocks.append(jnp.concatenate(pieces, axis=1))
        wbd_ref[j] = jnp.concatenate(row_blocks, axis=0)
    acc_ref[...] = jnp.zeros(acc_ref.shape, F32)

    def attn_chunks(chunks, m_all):
        units = [(k, j) for k in range(len(chunks)) for j in range(n_pairs)]
        bias2 = []
        for c in chunks:
            off = chunk_off(c)
            sel = (_fold_rows(score_ref[pl.ds(off, tk), :]) >= thr_sel[None]) & (off + row_iota3 <= qpos3)
            bias = jnp.where(sel, 0.0, neg_inf).reshape(tk, tq)
            bias2.append(jnp.concatenate([bias, bias], axis=1))

        def unit_logits(u):
            k, j = units[u]
            c0 = (j // pairs_per_group) * HEAD_GROUP * HEAD_DIM
            kc = kn_ref[0, pl.ds(chunk_off(chunks[k]), tk), c0:c0 + HEAD_GROUP * HEAD_DIM]
            return jnp.dot(kc, wbd_ref[j], preferred_element_type=F32)

        m_run = [m_all[:, 2 * j * tq:2 * (j + 1) * tq] for j in range(n_pairs)]
        lgs_raw = [unit_logits(0)]
        stage_a = []
        for u, (k, j) in enumerate(units):
            lg = lgs_raw[u] + bias2[k]
            m_old = m_run[j]
            if u + 1 < len(units):
                lgs_raw.append(unit_logits(u + 1))
                m_old = _after(m_old, lgs_raw[u + 1][0:1, :])
            m_new = jnp.maximum(m_old, jnp.max(jnp.max(_fold_rows(lg), axis=0), axis=0, keepdims=True))
            m_run[j] = m_new
            lg_ref[u] = lg
            stage_a.append((m_old, m_new))
        updates = []
        for u, (k, j) in enumerate(units):
            m_old, m_new = stage_a[u]
            m_use = jnp.where(m_new == neg_inf, 0.0, m_new)
            if u + 1 < len(units):
                m_use = _after(m_use, stage_a[u + 1][1])
            alpha = jnp.exp2(m_old - m_use)
            p = jnp.exp2(lg_ref[u] - m_use).astype(BF16)
            for hh in range(2):
                h = 2 * j + hh
                pv = jnp.dot(vt_ref[0, chunks[k], h * V_ROWS:(h + 1) * V_ROWS, :], p[:, hh * tq:(hh + 1) * tq],
                             preferred_element_type=F32)
                updates.append((h, alpha[:, hh * tq:(hh + 1) * tq], pv))
        for h, a, pv in updates:
            acc_ref[h] = acc_ref[h] * a + pv
        return jnp.concatenate(m_run, axis=1)

    m_all = jnp.full((1, ATT_HEADS * tq), neg_inf, F32)
    start = 0
    for width in ATTN_UNROLLS:
        trips = (nchunks - start) // width
        m_all = lax.fori_loop(
            0, trips, lambda it, m, s=start, w=width: attn_chunks([s + w * it + k for k in range(w)], m), m_all)
        start = start + trips * width
    for h in range(ATT_HEADS):
        o = acc_ref[h]
        at_ref[h * HEAD_DIM:(h + 1) * HEAD_DIM, :] = o[:HEAD_DIM] / o[HEAD_DIM:HEAD_DIM + 1]

    a_t = at_ref[...]
    ms = jnp.mean(a_t * a_t, axis=0, keepdims=True)
    y = (a_t * lax.rsqrt(ms + NORM_EPS)).T
    out_ref[0] = (y * g_ref[...]).astype(BF16)


def _dsa_call(kn, kin, vt, qt, qit, w_t, g_attn, topk):
    bsz, seq, _ = kn.shape
    nt, tq = qt.shape[1], qt.shape[3]
    tile4 = lambda b, i: (b, i, 0, 0)
    return pl.pallas_call(
        functools.partial(_dsa_kernel, topk=topk, seq=seq),
        grid=(bsz, nt),
        in_specs=[
            pl.BlockSpec((1, seq, ATT_WIDTH), lambda b, i: (b, 0, 0)),
            pl.BlockSpec((1, seq, IDX_DIM), lambda b, i: (b, 0, 0)),
            pl.BlockSpec((1, nt, ATT_HEADS * V_ROWS, tq), lambda b, i: (b, 0, 0, 0)),
            pl.BlockSpec((1, 1, ATT_WIDTH, tq), tile4),
            pl.BlockSpec((1, 1, IDX_HEADS * IDX_DIM, tq), tile4),
            pl.BlockSpec((1, 1, IDX_HEADS, tq), tile4),
            pl.BlockSpec((1, ATT_WIDTH), lambda b, i: (0, 0)),
        ],
        out_specs=pl.BlockSpec((1, tq, ATT_WIDTH), lambda b, i: (b, i, 0)),
        out_shape=jax.ShapeDtypeStruct((bsz, seq, ATT_WIDTH), BF16),
        scratch_shapes=[
            pltpu.VMEM((seq, tq), F32),
            pltpu.VMEM((seq, tq), F32),
            pltpu.VMEM((ATT_HEADS // 2, HEAD_GROUP * HEAD_DIM, 2 * tq), BF16),
            pltpu.VMEM((ATT_HEADS, V_ROWS, tq), F32),
            pltpu.VMEM((ATT_WIDTH, tq), F32),
            pltpu.VMEM((max(ATTN_UNROLLS) * (ATT_HEADS // 2), tq, 2 * tq), F32),
        ],
        compiler_params=pltpu.CompilerParams(
            dimension_semantics=("arbitrary", "arbitrary"), vmem_limit_bytes=VMEM_LIMIT),
        name="dsa_attention",
    )(kn, kin, vt, qt, qit, w_t, g_attn)


def _layer_norm(y, g, b):
    mu = jnp.mean(y, axis=-1, keepdims=True)
    d = y - mu
    var = jnp.mean(d * d, axis=-1, keepdims=True)
    return d * lax.rsqrt(var + NORM_EPS) * g + b


def _ffn_kernel(x_ref, am_ref, cm_ref, wout_ref, g1_ref, b1_ref, wgu_ref, wd_ref, g2_ref, b2_ref,
                o_ref, *, alpha, d_ff):
    n = FFN_SPLIT
    group = x_ref.shape[0] // n
    rows = [slice(r * group, (r + 1) * group) for r in range(n)]
    mix = [jnp.dot(am_ref[rw, :], wout_ref[0:ATT_WIDTH, :], preferred_element_type=F32)
           + jnp.dot(cm_ref[rw, :], wout_ref[ATT_WIDTH:D_MODEL, :], preferred_element_type=F32) for rw in rows]
    x1 = []
    for r in range(n):
        b1 = b1_ref[...] if r + 1 == n else _after(b1_ref[...], mix[r + 1][0:1, :])
        x1.append(_layer_norm(alpha * x_ref[rows[r], :] + mix[r], g1_ref[...], b1))
    gu = [jnp.dot(x1[r].astype(BF16), wgu_ref[...], preferred_element_type=F32) for r in range(n)]
    ffn = []
    for r in range(n):
        gate = gu[r][:, :d_ff]
        hidden = (gate * jax.nn.sigmoid(gate) * gu[r][:, d_ff:]).astype(BF16)
        ffn.append(jnp.dot(hidden, wd_ref[...], preferred_element_type=F32))
    for r in range(n):
        b2 = b2_ref[...] if r + 1 == n else _after(b2_ref[...], ffn[r + 1][0:1, :])
        o_ref[rows[r], :] = _layer_norm(alpha * x1[r] + ffn[r], g2_ref[...], b2)


def _ffn_call(x2d, am, cm, wout, g1, b1, wgu, wd, g2, b2, alpha):
    rows = x2d.shape[0]
    tm = ROW_TILE
    d_ff = wd.shape[0]
    row = lambda r: (r, 0)
    const = lambda r: (0, 0)
    resident = functools.partial(pl.BlockSpec, index_map=const, pipeline_mode=pl.Buffered(1))
    return pl.pallas_call(
        functools.partial(_ffn_kernel, alpha=alpha, d_ff=d_ff),
        grid=(rows // tm,),
        in_specs=[
            pl.BlockSpec((tm, D_MODEL), row),
            pl.BlockSpec((tm, ATT_WIDTH), row),
            pl.BlockSpec((tm, CONV_WIDTH), row),
            resident((D_MODEL, D_MODEL)),
            resident((1, D_MODEL)),
            resident((1, D_MODEL)),
            resident((D_MODEL, 2 * d_ff)),
            resident((d_ff, D_MODEL)),
            resident((1, D_MODEL)),
            resident((1, D_MODEL)),
        ],
        out_specs=pl.BlockSpec((tm, D_MODEL), row),
        out_shape=jax.ShapeDtypeStruct((rows, D_MODEL), F32),
        compiler_params=pltpu.CompilerParams(
            dimension_semantics=("arbitrary",), vmem_limit_bytes=VMEM_LIMIT),
        name="outproj_ffn",
    )(x2d, am, cm, wout, g1, b1, wgu, wd, g2, b2)


def _rotate_half_cols(w):
    d_in, n = w.shape
    w4 = w.reshape(d_in, n // HEAD_DIM, 2, HEAD_DIM // 2)
    return jnp.concatenate([-w4[:, :, 1:2, :], w4[:, :, 0:1, :]], axis=2).reshape(d_in, n)


def _group_halves_cols(w):
    d_in, n = w.shape
    w5 = w.reshape(d_in, n // HEAD_DIM // HEAD_GROUP, HEAD_GROUP, 2, HEAD_DIM // 2)
    return jnp.swapaxes(w5, 2, 3).reshape(d_in, n)


def _split_cols(w, sizes):
    out, o = [], 0
    for s in sizes:
        out.append(w[:, o:o + s])
        o += s
    return out


def kernel(x, w_in, conv_w, mix_norm_g, w_out, ln1_g, ln1_b, w_gate_up, w_down, ln2_g, ln2_b):
    bsz, seq, _ = x.shape
    depth = w_in.shape[0]
    alpha = (2 * depth) ** 0.25
    topk = min(INDEX_TOPK_MAX, seq // 4)
    assert seq % (SEQ_TILE * PROJ_GROUPS) == 0 and (bsz * seq) % ROW_TILE == 0 and topk <= SEQ_TILE

    inv_freq = 1.0 / (ROPE_THETA ** (np.arange(0, HEAD_DIM, 2, dtype=np.float64) / HEAD_DIM))
    ang = np.arange(seq, dtype=np.float64)[:, None] * inv_freq[None, :]
    cos, sin = np.cos(ang).astype(np.float32), np.sin(ang).astype(np.float32)
    cos_n, sin_n = np.tile(cos, (1, LANES // cos.shape[1])), np.tile(sin, (1, LANES // sin.shape[1]))
    cos_t, sin_t = np.ascontiguousarray(cos.T), np.ascontiguousarray(sin.T)

    for layer in range(depth):
        wq, wk, wv, wqi, wki, wwi, wgb, wgc, wh = _split_cols(w_in[layer].astype(BF16), IN_SPLIT_SIZES)
        pad_ki = jnp.zeros((D_MODEL, LANES - IDX_DIM), BF16)
        wn = jnp.concatenate([_group_halves_cols(wk), wgb, wgc, wh,
                              wki, pad_ki, _rotate_half_cols(wki), pad_ki], axis=1)
        pad_wi = jnp.zeros((D_MODEL, 2 * SUBLANES - IDX_HEADS), BF16)
        wt = jnp.concatenate([wq, wqi, wv, wwi, pad_wi], axis=1).T
        g = mix_norm_g[layer]
        kn, kin, qt, qit, vt, w_t, cm = _proj_call(
            x, wn, wt, cos_n, sin_n, cos_t, sin_t, conv_w[layer], g[ATT_WIDTH:][None, :])
        am = _dsa_call(kn, kin, vt, qt, qit, w_t, g[:ATT_WIDTH][None, :], topk)
        y = _ffn_call(
            x.reshape(bsz * seq, D_MODEL), am.reshape(bsz * seq, ATT_WIDTH), cm.reshape(bsz * seq, CONV_WIDTH),
            w_out[layer].astype(BF16), ln1_g[layer][None, :], ln1_b[layer][None, :],
            w_gate_up[layer].astype(BF16), w_down[layer].astype(BF16),
            ln2_g[layer][None, :], ln2_b[layer][None, :], alpha)
        x = y.reshape(bsz, seq, D_MODEL)
    return x
```

```python
import functools

import jax
import numpy as np
import jax.numpy as jnp
from jax import lax
from jax.experimental import pallas as pl
from jax.experimental.pallas import tpu as pltpu

D_MODEL = 1024
ATT_HEADS = 8
HEAD_DIM = 64
ATT_WIDTH = ATT_HEADS * HEAD_DIM
CONV_WIDTH = D_MODEL - ATT_WIDTH
CONV_K = 3
IDX_HEADS = 8
IDX_DIM = 64
INDEX_TOPK_MAX = 256
ROPE_THETA = 10000.0
NORM_EPS = 1e-5
IN_SPLIT_SIZES = (ATT_WIDTH, ATT_WIDTH, ATT_WIDTH, IDX_HEADS * IDX_DIM, IDX_DIM, IDX_HEADS,
                  CONV_WIDTH, CONV_WIDTH, CONV_WIDTH)

LANES = 128
SUBLANES = 8
SEQ_TILE = 256
PROJ_GROUPS = 2
ROW_TILE = 512
FFN_SPLIT = 2
SCORE_UNROLL = 4
COUNT_UNROLL = 4
COUNT_ACCS = 4
SORT_GROUP = 16
ATTN_UNROLLS = (4, 2, 1)
HEAD_GROUP = 4
V_ROWS = 80
LOG2E = 1.4426950408889634
VMEM_LIMIT = 56 * 1024 * 1024

F32 = jnp.float32
BF16 = jnp.bfloat16
INT_MIN = -2 ** 31
KEY_NEG_INF = (0xFF800000 ^ 0x7FFFFFFF) - 2 ** 32
KEY_POS_INF = 0x7F800000


def _proj_kernel(x_ref, wn_ref, wt_ref, cosn_ref, sinn_ref, cost_ref, sint_ref, convw_ref, gconv_ref,
                 kn_ref, kin_ref, qt_ref, qit_ref, vt_ref, wt_out_ref, cm_ref, u_ref, *, w_scale):
    t = pl.program_id(1)
    tm = qt_ref.shape[3]
    half = HEAD_DIM // 2
    q_scale = (HEAD_DIM ** -0.5) * LOG2E
    o_ki = ATT_WIDTH + 3 * CONV_WIDTH
    o_v = ATT_WIDTH + IDX_HEADS * IDX_DIM
    o_c = ATT_WIDTH
    pad_row = lax.broadcasted_iota(jnp.int32, (V_ROWS - HEAD_DIM, tm), 0)
    ones_pad = jnp.where(pad_row == 0, 1.0, 0.0).astype(BF16)
    cw = convw_ref[...]

    @pl.when(t == 0)
    def _():
        u_ref[0:SUBLANES, :] = jnp.zeros((SUBLANES, CONV_WIDTH), F32)

    xbs = [x_ref[0, g * tm:(g + 1) * tm, :].astype(BF16) for g in range(PROJ_GROUPS)]
    nats = [jnp.dot(xb, wn_ref[...], preferred_element_type=F32) for xb in xbs]
    trs = [lax.dot_general(wt_ref[...], xb, (((1,), (1,)), ((), ())), preferred_element_type=F32)
           for xb in xbs]

    for g in range(PROJ_GROUPS):
        nat, tr = nats[g], trs[g]
        rows = slice(g * tm, (g + 1) * tm)

        cos_n = cosn_ref[rows, :]
        sin_n = sinn_ref[rows, :]
        for grp in range(ATT_HEADS // HEAD_GROUP):
            c0 = grp * 2 * LANES
            x1 = nat[:, c0:c0 + LANES]
            x2 = nat[:, c0 + LANES:c0 + 2 * LANES]
            kn_ref[0, rows, c0:c0 + LANES] = (x1 * cos_n - x2 * sin_n).astype(BF16)
            kn_ref[0, rows, c0 + LANES:c0 + 2 * LANES] = (x2 * cos_n + x1 * sin_n).astype(BF16)
        ki_rope = nat[:, o_ki:o_ki + LANES] * cos_n + nat[:, o_ki + LANES:o_ki + 2 * LANES] * sin_n
        kin_ref[0, rows, :] = ki_rope[:, :IDX_DIM].astype(BF16)

        cos_t = cost_ref[:, rows]
        sin_t = sint_ref[:, rows]
        for h in range(ATT_HEADS):
            r0 = h * HEAD_DIM
            x1 = tr[r0:r0 + half]
            x2 = tr[r0 + half:r0 + HEAD_DIM]
            qt_ref[0, g, r0:r0 + half, :] = ((x1 * cos_t - x2 * sin_t) * q_scale).astype(BF16)
            qt_ref[0, g, r0 + half:r0 + HEAD_DIM, :] = ((x2 * cos_t + x1 * sin_t) * q_scale).astype(BF16)
        for h in range(IDX_HEADS):
            r0 = ATT_WIDTH + h * IDX_DIM
            x1 = tr[r0:r0 + half]
            x2 = tr[r0 + half:r0 + IDX_DIM]
            qit_ref[0, g, h * IDX_DIM:h * IDX_DIM + half, :] = (x1 * cos_t - x2 * sin_t).astype(BF16)
            qit_ref[0, g, h * IDX_DIM + half:(h + 1) * IDX_DIM, :] = (x2 * cos_t + x1 * sin_t).astype(BF16)
        for h in range(ATT_HEADS):
            vt_ref[0, g, h * V_ROWS:h * V_ROWS + HEAD_DIM, :] = (
                tr[o_v + h * HEAD_DIM:o_v + (h + 1) * HEAD_DIM].astype(BF16))
            vt_ref[0, g, h * V_ROWS + HEAD_DIM:(h + 1) * V_ROWS, :] = ones_pad
        wt_out_ref[0, g] = tr[o_v + ATT_WIDTH:o_v + ATT_WIDTH + IDX_HEADS] * w_scale

        gate_b = nat[:, o_c:o_c + CONV_WIDTH]
        u = nat[:, o_c + CONV_WIDTH:o_c + 2 * CONV_WIDTH] * nat[:, o_c + 2 * CONV_WIDTH:o_c + 3 * CONV_WIDTH]
        base = SUBLANES + g * tm
        u_ref[base:base + tm, :] = u
        um1 = u_ref[base - 1:base - 1 + tm, :]
        um2 = u_ref[base - 2:base - 2 + tm, :]
        conv_out = gate_b * (cw[0:1, :] * um2 + cw[1:2, :] * um1 + cw[2:3, :] * u)
        ms = jnp.mean(conv_out * conv_out, axis=-1, keepdims=True)
        cm_ref[0, rows, :] = (conv_out * lax.rsqrt(ms + NORM_EPS) * gconv_ref[...]).astype(BF16)

    tail = SUBLANES + PROJ_GROUPS * tm
    u_ref[0:SUBLANES, :] = u_ref[tail - SUBLANES:tail, :]


def _proj_call(x, wn, wt, cos_n, sin_n, cos_t, sin_t, conv_w, g_conv):
    bsz, seq, _ = x.shape
    tm = SEQ_TILE
    nt = seq // tm
    gm = PROJ_GROUPS * tm
    n_nat = wn.shape[1]
    n_tr = wt.shape[0]
    w_scale = (IDX_DIM ** -0.5) * (IDX_HEADS ** -0.5)
    const2 = lambda b, t: (0, 0)
    out_shapes = (
        jax.ShapeDtypeStruct((bsz, seq, ATT_WIDTH), BF16),
        jax.ShapeDtypeStruct((bsz, seq, IDX_DIM), BF16),
        jax.ShapeDtypeStruct((bsz, nt, ATT_WIDTH, tm), BF16),
        jax.ShapeDtypeStruct((bsz, nt, IDX_HEADS * IDX_DIM, tm), BF16),
        jax.ShapeDtypeStruct((bsz, nt, ATT_HEADS * V_ROWS, tm), BF16),
        jax.ShapeDtypeStruct((bsz, nt, IDX_HEADS, tm), F32),
        jax.ShapeDtypeStruct((bsz, seq, CONV_WIDTH), BF16),
    )
    tile4 = lambda b, t: (b, t, 0, 0)
    return pl.pallas_call(
        functools.partial(_proj_kernel, w_scale=w_scale),
        grid=(bsz, nt // PROJ_GROUPS),
        in_specs=[
            pl.BlockSpec((1, gm, D_MODEL), lambda b, t: (b, t, 0)),
            pl.BlockSpec((D_MODEL, n_nat), const2),
            pl.BlockSpec((n_tr, D_MODEL), const2),
            pl.BlockSpec((gm, LANES), lambda b, t: (t, 0)),
            pl.BlockSpec((gm, LANES), lambda b, t: (t, 0)),
            pl.BlockSpec((HEAD_DIM // 2, gm), lambda b, t: (0, t)),
            pl.BlockSpec((HEAD_DIM // 2, gm), lambda b, t: (0, t)),
            pl.BlockSpec((CONV_K, CONV_WIDTH), const2),
            pl.BlockSpec((1, CONV_WIDTH), const2),
        ],
        out_specs=(
            pl.BlockSpec((1, gm, ATT_WIDTH), lambda b, t: (b, t, 0)),
            pl.BlockSpec((1, gm, IDX_DIM), lambda b, t: (b, t, 0)),
            pl.BlockSpec((1, PROJ_GROUPS, ATT_WIDTH, tm), tile4),
            pl.BlockSpec((1, PROJ_GROUPS, IDX_HEADS * IDX_DIM, tm), tile4),
            pl.BlockSpec((1, PROJ_GROUPS, ATT_HEADS * V_ROWS, tm), tile4),
            pl.BlockSpec((1, PROJ_GROUPS, IDX_HEADS, tm), tile4),
            pl.BlockSpec((1, gm, CONV_WIDTH), lambda b, t: (b, t, 0)),
        ),
        out_shape=out_shapes,
        scratch_shapes=[pltpu.VMEM((gm + SUBLANES, CONV_WIDTH), F32)],
        compiler_params=pltpu.CompilerParams(
            dimension_semantics=("arbitrary", "arbitrary"), vmem_limit_bytes=VMEM_LIMIT),
        name="proj_rope_conv",
    )(x, wn, wt, cos_n, sin_n, cos_t, sin_t, conv_w, g_conv)


def _key_to_float(key):
    key = jnp.clip(key, KEY_NEG_INF, KEY_POS_INF)
    bits = jnp.where(key >= 0, key, key ^ 0x7FFFFFFF)
    return lax.bitcast_convert_type(bits, F32)


def _after(value, anchor):
    return jnp.maximum(value, jnp.minimum(anchor, -jnp.inf))


def _sorting_network(n):
    pairs = []

    def merge(lo, size, r):
        step = 2 * r
        if step < size:
            merge(lo, size, step)
            merge(lo + r, size, step)
            pairs.extend((i, i + r) for i in range(lo + r, lo + size - r, step))
        else:
            pairs.append((lo, lo + r))

    def sort(lo, size):
        if size > 1:
            sort(lo, size // 2)
            sort(lo + size // 2, size // 2)
            merge(lo, size, 1)

    sort(0, n)
    return pairs


def _sorted_group_count(grp, pred):
    size = len(grp)
    levels = size.bit_length() - 1
    masks = []

    def pick(cands, decisions):
        if len(cands) == 1:
            return cands[0]
        mid = len(cands) // 2
        return jnp.where(decisions[0], pick(cands[mid:], decisions[1:]), pick(cands[:mid], decisions[1:]))

    for level in range(1, levels + 1):
        step = size >> level
        cands = [grp[(2 * t + 1) * step - 1] for t in range(1 << (level - 1))]
        masks.append(pred(pick(cands, masks)))
    every = pred(grp[size - 1])
    count = jnp.where(every, 2, jnp.where(masks[levels - 1], 1, 0))
    for level in range(1, levels):
        count = count + jnp.where(masks[level - 1], size >> level, 0)
    return count


def _fold_rows(a):
    return a.reshape(a.shape[0] // SUBLANES, SUBLANES, a.shape[1])


def _dsa_kernel(kn_ref, kin_ref, vt_ref, qt_ref, qit_ref, w_ref, g_ref, out_ref,
                score_ref, sorted_ref, wbd_ref, acc_ref, at_ref, lg_ref, *, topk, seq):
    i = pl.program_id(1)
    tq = qt_ref.shape[3]
    tk = tq
    nchunks = i + 1
    neg_inf = jnp.float32(-jnp.inf)

    row_iota = lax.broadcasted_iota(jnp.int32, (tk, tq), 0)
    qpos = i * tq + lax.broadcasted_iota(jnp.int32, (tk, tq), 1)

    def chunk_off(c):
        return pl.multiple_of(c * tk, tk)

    qi_all = qit_ref[0, 0]
    w_rows = w_ref[0, 0]

    def score_body(c, carry):
        off = chunk_off(c)
        kic = kin_ref[0, pl.ds(off, tk), :]
        sc = jnp.zeros((tk, tq), F32)
        for h in range(IDX_HEADS):
            d = jnp.dot(kic, qi_all[h * IDX_DIM:(h + 1) * IDX_DIM, :], preferred_element_type=F32)
            sc = sc + jnp.maximum(d, 0.0) * w_rows[h:h + 1, :]
        sc = jnp.where(off + row_iota <= qpos, sc, neg_inf)
        score_ref[pl.ds(off, tk), :] = sc
        rows = _fold_rows(sc)
        sorted_rows = []
        for g in range(tk // SUBLANES // SORT_GROUP):
            grp = [rows[SORT_GROUP * g + r] for r in range(SORT_GROUP)]
            for a, b in _sorting_network(SORT_GROUP):
                grp[a], grp[b] = jnp.maximum(grp[a], grp[b]), jnp.minimum(grp[a], grp[b])
            sorted_rows.extend(grp)
        sorted_ref[pl.ds(off, tk), :] = jnp.stack(sorted_rows).reshape(tk, tq)
        return carry

    def score_unrolled(cu, carry):
        for k in range(SCORE_UNROLL):
            carry = score_body(SCORE_UNROLL * cu + k, carry)
        return carry

    lax.fori_loop(0, nchunks // SCORE_UNROLL, score_unrolled, 0)
    lax.fori_loop(SCORE_UNROLL * (nchunks // SCORE_UNROLL), nchunks, score_body, 0)

    row_iota3 = (lax.broadcasted_iota(jnp.int32, (tk // SUBLANES, SUBLANES, tq), 0) * SUBLANES
                 + lax.broadcasted_iota(jnp.int32, (tk // SUBLANES, SUBLANES, tq), 1))
    qpos3 = i * tq + lax.broadcasted_iota(jnp.int32, (tk // SUBLANES, SUBLANES, tq), 2)

    def count_total(chunk_partials, n_parts):
        def count_chunk(c, accs):
            return tuple(acc + part for acc, part in zip(accs, chunk_partials(c)))

        def unrolled(cu, accs):
            for k in range(COUNT_UNROLL):
                accs = count_chunk(COUNT_UNROLL * cu + k, accs)
            return accs

        n_main = nchunks // COUNT_UNROLL
        accs = lax.fori_loop(0, n_main, unrolled, (jnp.zeros((SUBLANES, tq), jnp.int32),) * n_parts)
        accs = lax.fori_loop(n_main * COUNT_UNROLL, nchunks, count_chunk, accs)
        acc = functools.reduce(lambda x, y: x + y, accs)
        for shift in (4, 2, 1):
            acc = acc + pltpu.roll(acc, shift, axis=0)
        return acc

    def count_rows(pred_fn):
        def partials(c):
            off = chunk_off(c)
            m = pred_fn(_fold_rows(score_ref[pl.ds(off, tk), :]), off).astype(jnp.int32)
            per = m.shape[0] // COUNT_ACCS
            return [jnp.sum(m[g * per:(g + 1) * per], axis=0) for g in range(COUNT_ACCS)]
        return count_total(partials, COUNT_ACCS)

    blocks_per_chunk = tk // SUBLANES // SORT_GROUP

    def count_sorted(pred):
        def partials(c):
            rows = _fold_rows(sorted_ref[pl.ds(chunk_off(c), tk), :])
            return [_sorted_group_count([rows[SORT_GROUP * g + r] for r in range(SORT_GROUP)], pred)
                    for g in range(blocks_per_chunk)]
        return count_total(partials, blocks_per_chunk)

    def search_body(p, carry):
        u, n_u = carry
        trial = u | jnp.left_shift(jnp.int32(1), 31 - p)
        cand = _key_to_float(trial ^ INT_MIN)
        cnt = count_sorted(lambda sc: sc >= cand)
        take = cnt >= topk
        return jnp.where(take, trial, u), jnp.where(take, cnt, n_u)

    u, n_ge = lax.fori_loop(0, 32, search_body,
                            (jnp.zeros((SUBLANES, tq), jnp.int32),
                             jnp.zeros((SUBLANES, tq), jnp.int32) + nchunks * tk))
    thr = _key_to_float(u ^ INT_MIN)

    needs_rewrite = (jnp.max(n_ge) > topk) | (i * tq < topk)

    def resolve_ties():
        want = topk - count_sorted(lambda sc: sc > thr)

        def body(p, jp):
            trial = jp + jnp.left_shift(jnp.int32(1), (seq - 1).bit_length() - 1 - p)
            cnt = count_rows(lambda sc, off: (sc == thr[None]) & (off + row_iota3 <= trial[None]))
            return jnp.where(cnt < want, trial, jp)
        jlim = lax.fori_loop(0, (seq - 1).bit_length(), body, jnp.full((SUBLANES, tq), -1, jnp.int32)) + 1

        def rewrite(c, carry):
            off = chunk_off(c)
            sc = _fold_rows(score_ref[pl.ds(off, tk), :])
            kpos = off + row_iota3
            sel = ((sc > thr[None]) | ((sc == thr[None]) & (kpos <= jlim[None]))) & (kpos <= qpos3)
            score_ref[pl.ds(off, tk), :] = jnp.where(sel, 1.0, -1.0).reshape(tk, tq)
            return carry

        lax.fori_loop(0, nchunks, rewrite, 0)
        return jnp.zeros((SUBLANES, tq), F32)

    thr_sel = lax.cond(needs_rewrite, resolve_ties, lambda: thr)

    q_all = qt_ref[0, 0]
    half = HEAD_DIM // 2
    n_pairs = ATT_HEADS // 2
    pairs_per_group = HEAD_GROUP // 2
    for j in range(n_pairs):
        first = 2 * (j % pairs_per_group)
        row_blocks = []
        for part in range(2):
            for hh in range(HEAD_GROUP):
                r0 = ((j // pairs_per_group) * HEAD_GROUP + hh) * HEAD_DIM + part * half
                pieces = [q_all[r0:r0 + half] if hh == first + col else jnp.zeros((half, tq), BF16)
                          for col in range(2)]
                row_blocks.append(jnp.concatenate(pieces, axis=1))
        wbd_ref[j] = jnp.concatenate(row_blocks, axis=0)
    acc_ref[...] = jnp.zeros(acc_ref.shape, F32)

    def attn_chunks(chunks, m_all):
        units = [(k, j) for k in range(len(chunks)) for j in range(n_pairs)]
        bias2 = []
        for c in chunks:
            sel = _fold_rows(score_ref[pl.ds(chunk_off(c), tk), :]) >= thr_sel[None]
            bias = jnp.where(sel, 0.0, neg_inf).reshape(tk, tq)
            bias2.append(jnp.concatenate([bias, bias], axis=1))

        def unit_logits(u):
            k, j = units[u]
            c0 = (j // pairs_per_group) * HEAD_GROUP * HEAD_DIM
            kc = kn_ref[0, pl.ds(chunk_off(chunks[k]), tk), c0:c0 + HEAD_GROUP * HEAD_DIM]
            return jnp.dot(kc, wbd_ref[j], preferred_element_type=F32)

        m_run = [m_all[:, 2 * j * tq:2 * (j + 1) * tq] for j in range(n_pairs)]
        lgs_raw = [unit_logits(0)]
        stage_a = []
        for u, (k, j) in enumerate(units):
            lg = lgs_raw[u] + bias2[k]
            m_old = m_run[j]
            if u + 1 < len(units):
                lgs_raw.append(unit_logits(u + 1))
                m_old = _after(m_old, lgs_raw[u + 1][0:1, :])
            m_new = jnp.maximum(m_old, jnp.max(jnp.max(_fold_rows(lg), axis=0), axis=0, keepdims=True))
            m_run[j] = m_new
            lg_ref[u] = lg
            stage_a.append((m_old, m_new))
        updates = []
        for u, (k, j) in enumerate(units):
            m_old, m_new = stage_a[u]
            m_use = jnp.where(m_new == neg_inf, 0.0, m_new)
            if u + 1 < len(units):
                m_use = _after(m_use, stage_a[u + 1][1])
            alpha = jnp.exp2(m_old - m_use)
            p = jnp.exp2(lg_ref[u] - m_use).astype(BF16)
            for hh in range(2):
                h = 2 * j + hh
                pv = jnp.dot(vt_ref[0, chunks[k], h * V_ROWS:(h + 1) * V_ROWS, :], p[:, hh * tq:(hh + 1) * tq],
                             preferred_element_type=F32)
                updates.append((h, alpha[:, hh * tq:(hh + 1) * tq], pv))
        for h, a, pv in updates:
            acc_ref[h] = acc_ref[h] * a + pv
        return jnp.concatenate(m_run, axis=1)

    m_all = jnp.full((1, ATT_HEADS * tq), neg_inf, F32)
    start = 0
    for width in ATTN_UNROLLS:
        trips = (nchunks - start) // width
        m_all = lax.fori_loop(
            0, trips, lambda it, m, s=start, w=width: attn_chunks([s + w * it + k for k in range(w)], m), m_all)
        start = start + trips * width
    for h in range(ATT_HEADS):
        o = acc_ref[h]
        at_ref[h * HEAD_DIM:(h + 1) * HEAD_DIM, :] = o[:HEAD_DIM] / o[HEAD_DIM:HEAD_DIM + 1]

    a_t = at_ref[...]
    ms = jnp.mean(a_t * a_t, axis=0, keepdims=True)
    y = (a_t * lax.rsqrt(ms + NORM_EPS)).T
    out_ref[0] = (y * g_ref[...]).astype(BF16)


def _dsa_call(kn, kin, vt, qt, qit, w_t, g_attn, topk):
    bsz, seq, _ = kn.shape
    nt, tq = qt.shape[1], qt.shape[3]
    tile4 = lambda b, i: (b, i, 0, 0)
    return pl.pallas_call(
        functools.partial(_dsa_kernel, topk=topk, seq=seq),
        grid=(bsz, nt),
        in_specs=[
            pl.BlockSpec((1, seq, ATT_WIDTH), lambda b, i: (b, 0, 0)),
            pl.BlockSpec((1, seq, IDX_DIM), lambda b, i: (b, 0, 0)),
            pl.BlockSpec((1, nt, ATT_HEADS * V_ROWS, tq), lambda b, i: (b, 0, 0, 0)),
            pl.BlockSpec((1, 1, ATT_WIDTH, tq), tile4),
            pl.BlockSpec((1, 1, IDX_HEADS * IDX_DIM, tq), tile4),
            pl.BlockSpec((1, 1, IDX_HEADS, tq), tile4),
            pl.BlockSpec((1, ATT_WIDTH), lambda b, i: (0, 0)),
        ],
        out_specs=pl.BlockSpec((1, tq, ATT_WIDTH), lambda b, i: (b, i, 0)),
        out_shape=jax.ShapeDtypeStruct((bsz, seq, ATT_WIDTH), BF16),
        scratch_shapes=[
            pltpu.VMEM((seq, tq), F32),
            pltpu.VMEM((seq, tq), F32),
            pltpu.VMEM((ATT_HEADS // 2, HEAD_GROUP * HEAD_DIM, 2 * tq), BF16),
            pltpu.VMEM((ATT_HEADS, V_ROWS, tq), F32),
            pltpu.VMEM((ATT_WIDTH, tq), F32),
            pltpu.VMEM((max(ATTN_UNROLLS) * (ATT_HEADS // 2), tq, 2 * tq), F32),
        ],
        compiler_params=pltpu.CompilerParams(
            dimension_semantics=("arbitrary", "arbitrary"), vmem_limit_bytes=VMEM_LIMIT),
        name="dsa_attention",
    )(kn, kin, vt, qt, qit, w_t, g_attn)


def _layer_norm(y, g, b):
    mu = jnp.mean(y, axis=-1, keepdims=True)
    d = y - mu
    var = jnp.mean(d * d, axis=-1, keepdims=True)
    return d * lax.rsqrt(var + NORM_EPS) * g + b


def _ffn_kernel(x_ref, am_ref, cm_ref, wout_ref, g1_ref, b1_ref, wgu_ref, wd_ref, g2_ref, b2_ref,
                o_ref, *, alpha, d_ff):
    n = FFN_SPLIT
    group = x_ref.shape[0] // n
    rows = [slice(r * group, (r + 1) * group) for r in range(n)]
    mix = [jnp.dot(am_ref[rw, :], wout_ref[0:ATT_WIDTH, :], preferred_element_type=F32)
           + jnp.dot(cm_ref[rw, :], wout_ref[ATT_WIDTH:D_MODEL, :], preferred_element_type=F32) for rw in rows]
    x1 = []
    for r in range(n):
        b1 = b1_ref[...] if r + 1 == n else _after(b1_ref[...], mix[r + 1][0:1, :])
        x1.append(_layer_norm(alpha * x_ref[rows[r], :] + mix[r], g1_ref[...], b1))
    gu = [jnp.dot(x1[r].astype(BF16), wgu_ref[...], preferred_element_type=F32) for r in range(n)]
    ffn = []
    for r in range(n):
        gate = gu[r][:, :d_ff]
        hidden = (gate * jax.nn.sigmoid(gate) * gu[r][:, d_ff:]).astype(BF16)
        ffn.append(jnp.dot(hidden, wd_ref[...], preferred_element_type=F32))
    for r in range(n):
        b2 = b2_ref[...] if r + 1 == n else _after(b2_ref[...], ffn[r + 1][0:1, :])
        o_ref[rows[r], :] = _layer_norm(alpha * x1[r] + ffn[r], g2_ref[...], b2)


def _ffn_call(x2d, am, cm, wout, g1, b1, wgu, wd, g2, b2, alpha):
    rows = x2d.shape[0]
    tm = ROW_TILE
    d_ff = wd.shape[0]
    row = lambda r: (r, 0)
    const = lambda r: (0, 0)
    resident = functools.partial(pl.BlockSpec, index_map=const, pipeline_mode=pl.Buffered(1))
    return pl.pallas_call(
        functools.partial(_ffn_kernel, alpha=alpha, d_ff=d_ff),
        grid=(rows // tm,),
        in_specs=[
            pl.BlockSpec((tm, D_MODEL), row),
            pl.BlockSpec((tm, ATT_WIDTH), row),
            pl.BlockSpec((tm, CONV_WIDTH), row),
            resident((D_MODEL, D_MODEL)),
            resident((1, D_MODEL)),
            resident((1, D_MODEL)),
            resident((D_MODEL, 2 * d_ff)),
            resident((d_ff, D_MODEL)),
            resident((1, D_MODEL)),
            resident((1, D_MODEL)),
        ],
        out_specs=pl.BlockSpec((tm, D_MODEL), row),
        out_shape=jax.ShapeDtypeStruct((rows, D_MODEL), F32),
        compiler_params=pltpu.CompilerParams(
            dimension_semantics=("arbitrary",), vmem_limit_bytes=VMEM_LIMIT),
        name="outproj_ffn",
    )(x2d, am, cm, wout, g1, b1, wgu, wd, g2, b2)


def _rotate_half_cols(w):
    d_in, n = w.shape
    w4 = w.reshape(d_in, n // HEAD_DIM, 2, HEAD_DIM // 2)
    return jnp.concatenate([-w4[:, :, 1:2, :], w4[:, :, 0:1, :]], axis=2).reshape(d_in, n)


def _group_halves_cols(w):
    d_in, n = w.shape
    w5 = w.reshape(d_in, n // HEAD_DIM // HEAD_GROUP, HEAD_GROUP, 2, HEAD_DIM // 2)
    return jnp.swapaxes(w5, 2, 3).reshape(d_in, n)


def _split_cols(w, sizes):
    out, o = [], 0
    for s in sizes:
        out.append(w[:, o:o + s])
        o += s
    return out


def kernel(x, w_in, conv_w, mix_norm_g, w_out, ln1_g, ln1_b, w_gate_up, w_down, ln2_g, ln2_b):
    bsz, seq, _ = x.shape
    depth = w_in.shape[0]
    alpha = (2 * depth) ** 0.25
    topk = min(INDEX_TOPK_MAX, seq // 4)
    assert seq % (SEQ_TILE * PROJ_GROUPS) == 0 and (bsz * seq) % ROW_TILE == 0 and topk <= SEQ_TILE

    inv_freq = 1.0 / (ROPE_THETA ** (np.arange(0, HEAD_DIM, 2, dtype=np.float64) / HEAD_DIM))
    ang = np.arange(seq, dtype=np.float64)[:, None] * inv_freq[None, :]
    cos, sin = np.cos(ang).astype(np.float32), np.sin(ang).astype(np.float32)
    cos_n, sin_n = np.tile(cos, (1, LANES // cos.shape[1])), np.tile(sin, (1, LANES // sin.shape[1]))
    cos_t, sin_t = np.ascontiguousarray(cos.T), np.ascontiguousarray(sin.T)

    for layer in range(depth):
        wq, wk, wv, wqi, wki, wwi, wgb, wgc, wh = _split_cols(w_in[layer].astype(BF16), IN_SPLIT_SIZES)
        pad_ki = jnp.zeros((D_MODEL, LANES - IDX_DIM), BF16)
        wn = jnp.concatenate([_group_halves_cols(wk), wgb, wgc, wh,
                              wki, pad_ki, _rotate_half_cols(wki), pad_ki], axis=1)
        pad_wi = jnp.zeros((D_MODEL, 2 * SUBLANES - IDX_HEADS), BF16)
        wt = jnp.concatenate([wq, wqi, wv, wwi, pad_wi], axis=1).T
        g = mix_norm_g[layer]
        kn, kin, qt, qit, vt, w_t, cm = _proj_call(
            x, wn, wt, cos_n, sin_n, cos_t, sin_t, conv_w[layer], g[ATT_WIDTH:][None, :])
        am = _dsa_call(kn, kin, vt, qt, qit, w_t, g[:ATT_WIDTH][None, :], topk)
        y = _ffn_call(
            x.reshape(bsz * seq, D_MODEL), am.reshape(bsz * seq, ATT_WIDTH), cm.reshape(bsz * seq, CONV_WIDTH),
            w_out[layer].astype(BF16), ln1_g[layer][None, :], ln1_b[layer][None, :],
            w_gate_up[layer].astype(BF16), w_down[layer].astype(BF16),
            ln2_g[layer][None, :], ln2_b[layer][None, :], alpha)
        x = y.reshape(bsz, seq, D_MODEL)
    return x
```

```python
import functools

import jax
import numpy as np
import jax.numpy as jnp
from jax import lax
from jax.experimental import pallas as pl
from jax.experimental.pallas import tpu as pltpu

D_MODEL = 1024
ATT_HEADS = 8
HEAD_DIM = 64
ATT_WIDTH = ATT_HEADS * HEAD_DIM
CONV_WIDTH = D_MODEL - ATT_WIDTH
CONV_K = 3
IDX_HEADS = 8
IDX_DIM = 64
INDEX_TOPK_MAX = 256
ROPE_THETA = 10000.0
NORM_EPS = 1e-5
IN_SPLIT_SIZES = (ATT_WIDTH, ATT_WIDTH, ATT_WIDTH, IDX_HEADS * IDX_DIM, IDX_DIM, IDX_HEADS,
                  CONV_WIDTH, CONV_WIDTH, CONV_WIDTH)

LANES = 128
SUBLANES = 8
SEQ_TILE = 256
PROJ_GROUPS = 4
ROW_TILE = 512
FFN_SPLIT = 2
SCORE_UNROLLS = (4, 2, 1)
COUNT_UNROLLS = (4, 1)
COUNT_ACCS = 4
SORT_GROUP = 16
ATTN_UNROLLS = (8, 4, 2, 1)
HEAD_GROUP = 4
V_ROWS = 80
LOG2E = 1.4426950408889634
VMEM_LIMIT = 56 * 1024 * 1024

F32 = jnp.float32
BF16 = jnp.bfloat16
INT_MIN = -2 ** 31
KEY_NEG_INF = (0xFF800000 ^ 0x7FFFFFFF) - 2 ** 32
KEY_POS_INF = 0x7F800000


def _proj_kernel(x_ref, wn_ref, wt_ref, cosn_ref, sinn_ref, cost_ref, sint_ref, convw_ref, gconv_ref,
                 kn_ref, kin_ref, qt_ref, qit_ref, vt_ref, wt_out_ref, cm_ref, u_ref, *, w_scale):
    t = pl.program_id(1)
    tm = qt_ref.shape[3]
    half = HEAD_DIM // 2
    q_scale = (HEAD_DIM ** -0.5) * LOG2E
    o_ki = ATT_WIDTH + 3 * CONV_WIDTH
    o_v = ATT_WIDTH + IDX_HEADS * IDX_DIM
    o_c = ATT_WIDTH
    pad_row = lax.broadcasted_iota(jnp.int32, (V_ROWS - HEAD_DIM, tm), 0)
    ones_pad = jnp.where(pad_row == 0, 1.0, 0.0).astype(BF16)
    cw = convw_ref[...]

    @pl.when(t == 0)
    def _():
        u_ref[0:SUBLANES, :] = jnp.zeros((SUBLANES, CONV_WIDTH), F32)

    xbs = [x_ref[0, g * tm:(g + 1) * tm, :].astype(BF16) for g in range(PROJ_GROUPS)]
    nats = [jnp.dot(xb, wn_ref[...], preferred_element_type=F32) for xb in xbs]
    trs = [lax.dot_general(wt_ref[...], xb, (((1,), (1,)), ((), ())), preferred_element_type=F32)
           for xb in xbs]

    for g in range(PROJ_GROUPS):
        nat, tr = nats[g], trs[g]
        rows = slice(g * tm, (g + 1) * tm)

        cos_n = cosn_ref[rows, :]
        sin_n = sinn_ref[rows, :]
        for grp in range(ATT_HEADS // HEAD_GROUP):
            c0 = grp * 2 * LANES
            x1 = nat[:, c0:c0 + LANES]
            x2 = nat[:, c0 + LANES:c0 + 2 * LANES]
            kn_ref[0, rows, c0:c0 + LANES] = (x1 * cos_n - x2 * sin_n).astype(BF16)
            kn_ref[0, rows, c0 + LANES:c0 + 2 * LANES] = (x2 * cos_n + x1 * sin_n).astype(BF16)
        ki_rope = nat[:, o_ki:o_ki + LANES] * cos_n + nat[:, o_ki + LANES:o_ki + 2 * LANES] * sin_n
        kin_ref[0, rows, :] = ki_rope[:, :IDX_DIM].astype(BF16)

        cos_t = cost_ref[:, rows]
        sin_t = sint_ref[:, rows]
        for h in range(ATT_HEADS):
            r0 = h * HEAD_DIM
            x1 = tr[r0:r0 + half]
            x2 = tr[r0 + half:r0 + HEAD_DIM]
            qt_ref[0, g, r0:r0 + half, :] = ((x1 * cos_t - x2 * sin_t) * q_scale).astype(BF16)
            qt_ref[0, g, r0 + half:r0 + HEAD_DIM, :] = ((x2 * cos_t + x1 * sin_t) * q_scale).astype(BF16)
        for h in range(IDX_HEADS):
            r0 = ATT_WIDTH + h * IDX_DIM
            x1 = tr[r0:r0 + half]
            x2 = tr[r0 + half:r0 + IDX_DIM]
            qit_ref[0, g, h * IDX_DIM:h * IDX_DIM + half, :] = (x1 * cos_t - x2 * sin_t).astype(BF16)
            qit_ref[0, g, h * IDX_DIM + half:(h + 1) * IDX_DIM, :] = (x2 * cos_t + x1 * sin_t).astype(BF16)
        for h in range(ATT_HEADS):
            vt_ref[0, g, h * V_ROWS:h * V_ROWS + HEAD_DIM, :] = (
                tr[o_v + h * HEAD_DIM:o_v + (h + 1) * HEAD_DIM].astype(BF16))
            vt_ref[0, g, h * V_ROWS + HEAD_DIM:(h + 1) * V_ROWS, :] = ones_pad
        wt_out_ref[0, g] = tr[o_v + ATT_WIDTH:o_v + ATT_WIDTH + IDX_HEADS] * w_scale

        gate_b = nat[:, o_c:o_c + CONV_WIDTH]
        u = nat[:, o_c + CONV_WIDTH:o_c + 2 * CONV_WIDTH] * nat[:, o_c + 2 * CONV_WIDTH:o_c + 3 * CONV_WIDTH]
        base = SUBLANES + g * tm
        u_ref[base:base + tm, :] = u
        um1 = u_ref[base - 1:base - 1 + tm, :]
        um2 = u_ref[base - 2:base - 2 + tm, :]
        conv_out = gate_b * (cw[0:1, :] * um2 + cw[1:2, :] * um1 + cw[2:3, :] * u)
        ms = jnp.mean(conv_out * conv_out, axis=-1, keepdims=True)
        cm_ref[0, rows, :] = (conv_out * lax.rsqrt(ms + NORM_EPS) * gconv_ref[...]).astype(BF16)

    tail = SUBLANES + PROJ_GROUPS * tm
    u_ref[0:SUBLANES, :] = u_ref[tail - SUBLANES:tail, :]


def _proj_call(x, wn, wt, cos_n, sin_n, cos_t, sin_t, conv_w, g_conv):
    bsz, seq, _ = x.shape
    tm = SEQ_TILE
    nt = seq // tm
    gm = PROJ_GROUPS * tm
    n_nat = wn.shape[1]
    n_tr = wt.shape[0]
    w_scale = (IDX_DIM ** -0.5) * (IDX_HEADS ** -0.5)
    const2 = lambda b, t: (0, 0)
    out_shapes = (
        jax.ShapeDtypeStruct((bsz, seq, ATT_WIDTH), BF16),
        jax.ShapeDtypeStruct((bsz, seq, IDX_DIM), BF16),
        jax.ShapeDtypeStruct((bsz, nt, ATT_WIDTH, tm), BF16),
        jax.ShapeDtypeStruct((bsz, nt, IDX_HEADS * IDX_DIM, tm), BF16),
        jax.ShapeDtypeStruct((bsz, nt, ATT_HEADS * V_ROWS, tm), BF16),
        jax.ShapeDtypeStruct((bsz, nt, IDX_HEADS, tm), F32),
        jax.ShapeDtypeStruct((bsz, seq, CONV_WIDTH), BF16),
    )
    tile4 = lambda b, t: (b, t, 0, 0)
    return pl.pallas_call(
        functools.partial(_proj_kernel, w_scale=w_scale),
        grid=(bsz, nt // PROJ_GROUPS),
        in_specs=[
            pl.BlockSpec((1, gm, D_MODEL), lambda b, t: (b, t, 0)),
            pl.BlockSpec((D_MODEL, n_nat), const2),
            pl.BlockSpec((n_tr, D_MODEL), const2),
            pl.BlockSpec((gm, LANES), lambda b, t: (t, 0)),
            pl.BlockSpec((gm, LANES), lambda b, t: (t, 0)),
            pl.BlockSpec((HEAD_DIM // 2, gm), lambda b, t: (0, t)),
            pl.BlockSpec((HEAD_DIM // 2, gm), lambda b, t: (0, t)),
            pl.BlockSpec((CONV_K, CONV_WIDTH), const2),
            pl.BlockSpec((1, CONV_WIDTH), const2),
        ],
        out_specs=(
            pl.BlockSpec((1, gm, ATT_WIDTH), lambda b, t: (b, t, 0)),
            pl.BlockSpec((1, gm, IDX_DIM), lambda b, t: (b, t, 0)),
            pl.BlockSpec((1, PROJ_GROUPS, ATT_WIDTH, tm), tile4),
            pl.BlockSpec((1, PROJ_GROUPS, IDX_HEADS * IDX_DIM, tm), tile4),
            pl.BlockSpec((1, PROJ_GROUPS, ATT_HEADS * V_ROWS, tm), tile4),
            pl.BlockSpec((1, PROJ_GROUPS, IDX_HEADS, tm), tile4),
            pl.BlockSpec((1, gm, CONV_WIDTH), lambda b, t: (b, t, 0)),
        ),
        out_shape=out_shapes,
        scratch_shapes=[pltpu.VMEM((gm + SUBLANES, CONV_WIDTH), F32)],
        compiler_params=pltpu.CompilerParams(
            dimension_semantics=("arbitrary", "arbitrary"), vmem_limit_bytes=VMEM_LIMIT),
        name="proj_rope_conv",
    )(x, wn, wt, cos_n, sin_n, cos_t, sin_t, conv_w, g_conv)


def _key_to_float(key):
    key = jnp.clip(key, KEY_NEG_INF, KEY_POS_INF)
    bits = jnp.where(key >= 0, key, key ^ 0x7FFFFFFF)
    return lax.bitcast_convert_type(bits, F32)


def _after(value, anchor):
    return jnp.maximum(value, jnp.minimum(anchor, -jnp.inf))


def _sorting_network(n):
    pairs = []

    def merge(lo, size, r):
        step = 2 * r
        if step < size:
            merge(lo, size, step)
            merge(lo + r, size, step)
            pairs.extend((i, i + r) for i in range(lo + r, lo + size - r, step))
        else:
            pairs.append((lo, lo + r))

    def sort(lo, size):
        if size > 1:
            sort(lo, size // 2)
            sort(lo + size // 2, size // 2)
            merge(lo, size, 1)

    sort(0, n)
    return pairs


def _sorted_group_count(grp, pred):
    size = len(grp)
    levels = size.bit_length() - 1
    masks = []

    def pick(cands, decisions):
        if len(cands) == 1:
            return cands[0]
        mid = len(cands) // 2
        return jnp.where(decisions[0], pick(cands[mid:], decisions[1:]), pick(cands[:mid], decisions[1:]))

    for level in range(1, levels + 1):
        step = size >> level
        cands = [grp[(2 * t + 1) * step - 1] for t in range(1 << (level - 1))]
        masks.append(pred(pick(cands, masks)))
    every = pred(grp[size - 1])
    count = jnp.where(every, 2, jnp.where(masks[levels - 1], 1, 0))
    for level in range(1, levels):
        count = count + jnp.where(masks[level - 1], size >> level, 0)
    return count


def _fold_rows(a):
    return a.reshape(a.shape[0] // SUBLANES, SUBLANES, a.shape[1])


def _dsa_kernel(kn_ref, kin_ref, vt_ref, qt_ref, qit_ref, w_ref, g_ref, out_ref,
                score_ref, sorted_ref, wbd_ref, acc_ref, at_ref, lg_ref, *, topk, seq):
    i = pl.program_id(1)
    tq = qt_ref.shape[3]
    tk = tq
    nchunks = i + 1
    neg_inf = jnp.float32(-jnp.inf)

    row_iota = lax.broadcasted_iota(jnp.int32, (tk, tq), 0)
    qpos = i * tq + lax.broadcasted_iota(jnp.int32, (tk, tq), 1)

    def chunk_off(c):
        return pl.multiple_of(c * tk, tk)

    def over_chunks(widths, body, carry):
        start = 0
        for width in widths:
            trips = (nchunks - start) // width
            carry = lax.fori_loop(
                0, trips, lambda it, cr, s=start, w=width: body([s + w * it + k for k in range(w)], cr), carry)
            start = start + trips * width
        return carry

    qi_all = qit_ref[0, 0]
    w_rows = w_ref[0, 0]

    def score_body(c, carry):
        off = chunk_off(c)
        kic = kin_ref[0, pl.ds(off, tk), :]
        sc = jnp.zeros((tk, tq), F32)
        for h in range(IDX_HEADS):
            d = jnp.dot(kic, qi_all[h * IDX_DIM:(h + 1) * IDX_DIM, :], preferred_element_type=F32)
            sc = sc + jnp.maximum(d, 0.0) * w_rows[h:h + 1, :]
        sc = jnp.where(off + row_iota <= qpos, sc, neg_inf)
        score_ref[pl.ds(off, tk), :] = sc
        rows = _fold_rows(sc)
        sorted_rows = []
        for g in range(tk // SUBLANES // SORT_GROUP):
            grp = [rows[SORT_GROUP * g + r] for r in range(SORT_GROUP)]
            for a, b in _sorting_network(SORT_GROUP):
                grp[a], grp[b] = jnp.maximum(grp[a], grp[b]), jnp.minimum(grp[a], grp[b])
            sorted_rows.extend(grp)
        sorted_ref[pl.ds(off, tk), :] = jnp.stack(sorted_rows).reshape(tk, tq)
        return carry

    over_chunks(SCORE_UNROLLS, lambda chunks, carry: functools.reduce(lambda cr, c: score_body(c, cr), chunks, carry), 0)

    row_iota3 = (lax.broadcasted_iota(jnp.int32, (tk // SUBLANES, SUBLANES, tq), 0) * SUBLANES
                 + lax.broadcasted_iota(jnp.int32, (tk // SUBLANES, SUBLANES, tq), 1))
    qpos3 = i * tq + lax.broadcasted_iota(jnp.int32, (tk // SUBLANES, SUBLANES, tq), 2)

    def count_total(chunk_partials, n_parts):
        def count_chunks(chunks, accs):
            for c in chunks:
                accs = tuple(acc + part for acc, part in zip(accs, chunk_partials(c)))
            return accs

        accs = over_chunks(COUNT_UNROLLS, count_chunks, (jnp.zeros((SUBLANES, tq), jnp.int32),) * n_parts)
        acc = functools.reduce(lambda x, y: x + y, accs)
        for shift in (4, 2, 1):
            acc = acc + pltpu.roll(acc, shift, axis=0)
        return acc

    def count_rows(pred_fn):
        def partials(c):
            off = chunk_off(c)
            m = pred_fn(_fold_rows(score_ref[pl.ds(off, tk), :]), off).astype(jnp.int32)
            per = m.shape[0] // COUNT_ACCS
            return [jnp.sum(m[g * per:(g + 1) * per], axis=0) for g in range(COUNT_ACCS)]
        return count_total(partials, COUNT_ACCS)

    blocks_per_chunk = tk // SUBLANES // SORT_GROUP

    def count_sorted(pred):
        def partials(c):
            rows = _fold_rows(sorted_ref[pl.ds(chunk_off(c), tk), :])
            return [_sorted_group_count([rows[SORT_GROUP * g + r] for r in range(SORT_GROUP)], pred)
                    for g in range(blocks_per_chunk)]
        return count_total(partials, blocks_per_chunk)

    def search_body(p, carry):
        u, n_u = carry
        trial = u | jnp.left_shift(jnp.int32(1), 31 - p)
        cand = _key_to_float(trial ^ INT_MIN)
        cnt = count_sorted(lambda sc: sc >= cand)
        take = cnt >= topk
        return jnp.where(take, trial, u), jnp.where(take, cnt, n_u)

    u, n_ge = lax.fori_loop(0, 32, search_body,
                            (jnp.zeros((SUBLANES, tq), jnp.int32),
                             jnp.zeros((SUBLANES, tq), jnp.int32) + nchunks * tk))
    thr = _key_to_float(u ^ INT_MIN)

    needs_rewrite = (jnp.max(n_ge) > topk) | (i * tq < topk)

    def resolve_ties():
        want = topk - count_sorted(lambda sc: sc > thr)

        def body(p, jp):
            trial = jp + jnp.left_shift(jnp.int32(1), (seq - 1).bit_length() - 1 - p)
            cnt = count_rows(lambda sc, off: (sc == thr[None]) & (off + row_iota3 <= trial[None]))
            return jnp.where(cnt < want, trial, jp)
        jlim = lax.fori_loop(0, (seq - 1).bit_length(), body, jnp.full((SUBLANES, tq), -1, jnp.int32)) + 1

        def rewrite(c, carry):
            off = chunk_off(c)
            sc = _fold_rows(score_ref[pl.ds(off, tk), :])
            kpos = off + row_iota3
            sel = ((sc > thr[None]) | ((sc == thr[None]) & (kpos <= jlim[None]))) & (kpos <= qpos3)
            score_ref[pl.ds(off, tk), :] = jnp.where(sel, 1.0, -1.0).reshape(tk, tq)
            return carry

        lax.fori_loop(0, nchunks, rewrite, 0)
        return jnp.zeros((SUBLANES, tq), F32)

    thr_sel = lax.cond(needs_rewrite, resolve_ties, lambda: thr)

    q_all = qt_ref[0, 0]
    half = HEAD_DIM // 2
    n_pairs = ATT_HEADS // 2
    pairs_per_group = HEAD_GROUP // 2
    for j in range(n_pairs):
        first = 2 * (j % pairs_per_group)
        row_blocks = []
        for part in range(2):
            for hh in range(HEAD_GROUP):
                r0 = ((j // pairs_per_group) * HEAD_GROUP + hh) * HEAD_DIM + part * half
                pieces = [q_all[r0:r0 + half] if hh == first + col else jnp.zeros((half, tq), BF16)
                          for col in range(2)]
                row_blocks.append(jnp.concatenate(pieces, axis=1))
        wbd_ref[j] = jnp.concatenate(row_blocks, axis=0)
    acc_ref[...] = jnp.zeros(acc_ref.shape, F32)

    def attn_chunks(chunks, m_all):
        rows = len(chunks) * tk
        off = chunk_off(chunks[0])
        sel = _fold_rows(score_ref[pl.ds(off, rows), :]) >= thr_sel[None]
        bias = jnp.where(sel, 0.0, neg_inf).reshape(rows, tq)
        bias2 = jnp.concatenate([bias, bias], axis=1)

        def pair_logits(j):
            c0 = (j // pairs_per_group) * HEAD_GROUP * HEAD_DIM
            kc = kn_ref[0, pl.ds(off, rows), c0:c0 + HEAD_GROUP * HEAD_DIM]
            return jnp.dot(kc, wbd_ref[j], preferred_element_type=F32)

        m_run = [m_all[:, 2 * j * tq:2 * (j + 1) * tq] for j in range(n_pairs)]
        lgs_raw = [pair_logits(0)]
        stage_a = []
        for j in range(n_pairs):
            lg = lgs_raw[j] + bias2
            m_old = m_run[j]
            if j + 1 < n_pairs:
                lgs_raw.append(pair_logits(j + 1))
                m_old = _after(m_old, lgs_raw[j + 1][0:1, :])
            m_new = jnp.maximum(m_old, jnp.max(jnp.max(_fold_rows(lg), axis=0), axis=0, keepdims=True))
            m_run[j] = m_new
            lg_ref[j, 0:rows, :] = lg
            stage_a.append((m_old, m_new))
        for j in range(n_pairs):
            m_old, m_new = stage_a[j]
            m_use = jnp.where(m_new == neg_inf, 0.0, m_new)
            if j + 1 < n_pairs:
                m_use = _after(m_use, stage_a[j + 1][1])
            alpha = jnp.exp2(m_old - m_use)
            p = jnp.exp2(lg_ref[j, 0:rows, :] - m_use).astype(BF16)
            for hh in range(2):
                h = 2 * j + hh
                v_t = jnp.concatenate([vt_ref[0, c, h * V_ROWS:(h + 1) * V_ROWS, :] for c in chunks], axis=1)
                pv = jnp.dot(v_t, p[:, hh * tq:(hh + 1) * tq], preferred_element_type=F32)
                acc_ref[h] = acc_ref[h] * alpha[:, hh * tq:(hh + 1) * tq] + pv
        return jnp.concatenate(m_run, axis=1)

    over_chunks(ATTN_UNROLLS, attn_chunks, jnp.full((1, ATT_HEADS * tq), neg_inf, F32))
    for h in range(ATT_HEADS):
        o = acc_ref[h]
        at_ref[h * HEAD_DIM:(h + 1) * HEAD_DIM, :] = o[:HEAD_DIM] / o[HEAD_DIM:HEAD_DIM + 1]

    a_t = at_ref[...]
    ms = jnp.mean(a_t * a_t, axis=0, keepdims=True)
    y = (a_t * lax.rsqrt(ms + NORM_EPS)).T
    out_ref[0] = (y * g_ref[...]).astype(BF16)


def _dsa_call(kn, kin, vt, qt, qit, w_t, g_attn, topk):
    bsz, seq, _ = kn.shape
    nt, tq = qt.shape[1], qt.shape[3]
    tile4 = lambda b, i: (b, i, 0, 0)
    return pl.pallas_call(
        functools.partial(_dsa_kernel, topk=topk, seq=seq),
        grid=(bsz, nt),
        in_specs=[
            pl.BlockSpec((1, seq, ATT_WIDTH), lambda b, i: (b, 0, 0)),
            pl.BlockSpec((1, seq, IDX_DIM), lambda b, i: (b, 0, 0)),
            pl.BlockSpec((1, nt, ATT_HEADS * V_ROWS, tq), lambda b, i: (b, 0, 0, 0)),
            pl.BlockSpec((1, 1, ATT_WIDTH, tq), tile4),
            pl.BlockSpec((1, 1, IDX_HEADS * IDX_DIM, tq), tile4),
            pl.BlockSpec((1, 1, IDX_HEADS, tq), tile4),
            pl.BlockSpec((1, ATT_WIDTH), lambda b, i: (0, 0)),
        ],
        out_specs=pl.BlockSpec((1, tq, ATT_WIDTH), lambda b, i: (b, i, 0)),
        out_shape=jax.ShapeDtypeStruct((bsz, seq, ATT_WIDTH), BF16),
        scratch_shapes=[
            pltpu.VMEM((seq, tq), F32),
            pltpu.VMEM((seq, tq), F32),
            pltpu.VMEM((ATT_HEADS // 2, HEAD_GROUP * HEAD_DIM, 2 * tq), BF16),
            pltpu.VMEM((ATT_HEADS, V_ROWS, tq), F32),
            pltpu.VMEM((ATT_WIDTH, tq), F32),
            pltpu.VMEM((ATT_HEADS // 2, max(ATTN_UNROLLS) * tq, 2 * tq), F32),
        ],
        compiler_params=pltpu.CompilerParams(
            dimension_semantics=("arbitrary", "arbitrary"), vmem_limit_bytes=VMEM_LIMIT),
        name="dsa_attention",
    )(kn, kin, vt, qt, qit, w_t, g_attn)


def _layer_norm(y, g, b):
    mu = jnp.mean(y, axis=-1, keepdims=True)
    d = y - mu
    var = jnp.mean(d * d, axis=-1, keepdims=True)
    return d * lax.rsqrt(var + NORM_EPS) * g + b


def _ffn_kernel(x_ref, am_ref, cm_ref, wout_ref, g1_ref, b1_ref, wgu_ref, wd_ref, g2_ref, b2_ref,
                o_ref, *, alpha, d_ff):
    n = FFN_SPLIT
    group = x_ref.shape[0] // n
    rows = [slice(r * group, (r + 1) * group) for r in range(n)]
    mix = [jnp.dot(am_ref[rw, :], wout_ref[0:ATT_WIDTH, :], preferred_element_type=F32)
           + jnp.dot(cm_ref[rw, :], wout_ref[ATT_WIDTH:D_MODEL, :], preferred_element_type=F32) for rw in rows]
    x1 = []
    for r in range(n):
        b1 = b1_ref[...] if r + 1 == n else _after(b1_ref[...], mix[r + 1][0:1, :])
        x1.append(_layer_norm(alpha * x_ref[rows[r], :] + mix[r], g1_ref[...], b1))
    gu = [jnp.dot(x1[r].astype(BF16), wgu_ref[...], preferred_element_type=F32) for r in range(n)]
    ffn = []
    for r in range(n):
        gate = gu[r][:, :d_ff]
        hidden = (gate * jax.nn.sigmoid(gate) * gu[r][:, d_ff:]).astype(BF16)
        ffn.append(jnp.dot(hidden, wd_ref[...], preferred_element_type=F32))
    for r in range(n):
        b2 = b2_ref[...] if r + 1 == n else _after(b2_ref[...], ffn[r + 1][0:1, :])
        o_ref[rows[r], :] = _layer_norm(alpha * x1[r] + ffn[r], g2_ref[...], b2)


def _ffn_call(x2d, am, cm, wout, g1, b1, wgu, wd, g2, b2, alpha):
    rows = x2d.shape[0]
    tm = ROW_TILE
    d_ff = wd.shape[0]
    row = lambda r: (r, 0)
    const = lambda r: (0, 0)
    resident = functools.partial(pl.BlockSpec, index_map=const, pipeline_mode=pl.Buffered(1))
    return pl.pallas_call(
        functools.partial(_ffn_kernel, alpha=alpha, d_ff=d_ff),
        grid=(rows // tm,),
        in_specs=[
            pl.BlockSpec((tm, D_MODEL), row),
            pl.BlockSpec((tm, ATT_WIDTH), row),
            pl.BlockSpec((tm, CONV_WIDTH), row),
            resident((D_MODEL, D_MODEL)),
            resident((1, D_MODEL)),
            resident((1, D_MODEL)),
            resident((D_MODEL, 2 * d_ff)),
            resident((d_ff, D_MODEL)),
            resident((1, D_MODEL)),
            resident((1, D_MODEL)),
        ],
        out_specs=pl.BlockSpec((tm, D_MODEL), row),
        out_shape=jax.ShapeDtypeStruct((rows, D_MODEL), F32),
        compiler_params=pltpu.CompilerParams(
            dimension_semantics=("arbitrary",), vmem_limit_bytes=VMEM_LIMIT),
        name="outproj_ffn",
    )(x2d, am, cm, wout, g1, b1, wgu, wd, g2, b2)


def _rotate_half_cols(w):
    d_in, n = w.shape
    w4 = w.reshape(d_in, n // HEAD_DIM, 2, HEAD_DIM // 2)
    return jnp.concatenate([-w4[:, :, 1:2, :], w4[:, :, 0:1, :]], axis=2).reshape(d_in, n)


def _group_halves_cols(w):
    d_in, n = w.shape
    w5 = w.reshape(d_in, n // HEAD_DIM // HEAD_GROUP, HEAD_GROUP, 2, HEAD_DIM // 2)
    return jnp.swapaxes(w5, 2, 3).reshape(d_in, n)


def _split_cols(w, sizes):
    out, o = [], 0
    for s in sizes:
        out.append(w[:, o:o + s])
        o += s
    return out


def kernel(x, w_in, conv_w, mix_norm_g, w_out, ln1_g, ln1_b, w_gate_up, w_down, ln2_g, ln2_b):
    bsz, seq, _ = x.shape
    depth = w_in.shape[0]
    alpha = (2 * depth) ** 0.25
    topk = min(INDEX_TOPK_MAX, seq // 4)
    assert seq % (SEQ_TILE * PROJ_GROUPS) == 0 and (bsz * seq) % ROW_TILE == 0 and topk <= SEQ_TILE

    inv_freq = 1.0 / (ROPE_THETA ** (np.arange(0, HEAD_DIM, 2, dtype=np.float64) / HEAD_DIM))
    ang = np.arange(seq, dtype=np.float64)[:, None] * inv_freq[None, :]
    cos, sin = np.cos(ang).astype(np.float32), np.sin(ang).astype(np.float32)
    cos_n, sin_n = np.tile(cos, (1, LANES // cos.shape[1])), np.tile(sin, (1, LANES // sin.shape[1]))
    cos_t, sin_t = np.ascontiguousarray(cos.T), np.ascontiguousarray(sin.T)

    for layer in range(depth):
        wq, wk, wv, wqi, wki, wwi, wgb, wgc, wh = _split_cols(w_in[layer].astype(BF16), IN_SPLIT_SIZES)
        pad_ki = jnp.zeros((D_MODEL, LANES - IDX_DIM), BF16)
        wn = jnp.concatenate([_group_halves_cols(wk), wgb, wgc, wh,
                              wki, pad_ki, _rotate_half_cols(wki), pad_ki], axis=1)
        pad_wi = jnp.zeros((D_MODEL, 2 * SUBLANES - IDX_HEADS), BF16)
        wt = jnp.concatenate([wq, wqi, wv, wwi, pad_wi], axis=1).T
        g = mix_norm_g[layer]
        kn, kin, qt, qit, vt, w_t, cm = _proj_call(
            x, wn, wt, cos_n, sin_n, cos_t, sin_t, conv_w[layer], g[ATT_WIDTH:][None, :])
        am = _dsa_call(kn, kin, vt, qt, qit, w_t, g[:ATT_WIDTH][None, :], topk)
        y = _ffn_call(
            x.reshape(bsz * seq, D_MODEL), am.reshape(bsz * seq, ATT_WIDTH), cm.reshape(bsz * seq, CONV_WIDTH),
            w_out[layer].astype(BF16), ln1_g[layer][None, :], ln1_b[layer][None, :],
            w_gate_up[layer].astype(BF16), w_down[layer].astype(BF16),
            ln2_g[layer][None, :], ln2_b[layer][None, :], alpha)
        x = y.reshape(bsz, seq, D_MODEL)
    return x
```

```python
import functools

import jax
import numpy as np
import jax.numpy as jnp
from jax import lax
from jax.experimental import pallas as pl
from jax.experimental.pallas import tpu as pltpu

D_MODEL = 1024
ATT_HEADS = 8
HEAD_DIM = 64
ATT_WIDTH = ATT_HEADS * HEAD_DIM
CONV_WIDTH = D_MODEL - ATT_WIDTH
CONV_K = 3
IDX_HEADS = 8
IDX_DIM = 64
INDEX_TOPK_MAX = 256
ROPE_THETA = 10000.0
NORM_EPS = 1e-5
IN_SPLIT_SIZES = (ATT_WIDTH, ATT_WIDTH, ATT_WIDTH, IDX_HEADS * IDX_DIM, IDX_DIM, IDX_HEADS,
                  CONV_WIDTH, CONV_WIDTH, CONV_WIDTH)

LANES = 128
SUBLANES = 8
SEQ_TILE = 256
PROJ_GROUPS = 4
ROW_TILE = 512
FFN_SPLIT = 2
SCORE_UNROLLS = (4, 2, 1)
COUNT_UNROLLS = (4, 1)
COUNT_ACCS = 4
SORT_GROUP = 16
ATTN_UNROLLS = (8, 4, 2, 1)
HEAD_GROUP = 4
V_ROWS = 80
LOG2E = 1.4426950408889634
VMEM_LIMIT = 56 * 1024 * 1024

F32 = jnp.float32
BF16 = jnp.bfloat16
INT_MIN = -2 ** 31
KEY_NEG_INF = (0xFF800000 ^ 0x7FFFFFFF) - 2 ** 32
KEY_POS_INF = 0x7F800000


def _proj_kernel(x_ref, wn_ref, wt_ref, cosn_ref, sinn_ref, cost_ref, sint_ref, convw_ref, gconv_ref,
                 kn_ref, kin_ref, qt_ref, qit_ref, vt_ref, wt_out_ref, cm_ref, u_ref, *, w_scale):
    t = pl.program_id(1)
    tm = qt_ref.shape[3]
    half = HEAD_DIM // 2
    q_scale = (HEAD_DIM ** -0.5) * LOG2E
    o_ki = ATT_WIDTH + 3 * CONV_WIDTH
    o_v = ATT_WIDTH + IDX_HEADS * IDX_DIM
    o_c = ATT_WIDTH
    pad_row = lax.broadcasted_iota(jnp.int32, (V_ROWS - HEAD_DIM, tm), 0)
    ones_pad = jnp.where(pad_row == 0, 1.0, 0.0).astype(BF16)
    cw = convw_ref[...]

    @pl.when(t == 0)
    def _():
        u_ref[0:SUBLANES, :] = jnp.zeros((SUBLANES, CONV_WIDTH), F32)

    xbs = [x_ref[0, g * tm:(g + 1) * tm, :].astype(BF16) for g in range(PROJ_GROUPS)]
    nats = [jnp.dot(xb, wn_ref[...], preferred_element_type=F32) for xb in xbs]
    trs = [lax.dot_general(wt_ref[...], xb, (((1,), (1,)), ((), ())), preferred_element_type=F32)
           for xb in xbs]

    for g in range(PROJ_GROUPS):
        nat, tr = nats[g], trs[g]
        rows = slice(g * tm, (g + 1) * tm)

        cos_n = cosn_ref[rows, :]
        sin_n = sinn_ref[rows, :]
        for grp in range(ATT_HEADS // HEAD_GROUP):
            c0 = grp * 2 * LANES
            x1 = nat[:, c0:c0 + LANES]
            x2 = nat[:, c0 + LANES:c0 + 2 * LANES]
            kn_ref[0, rows, c0:c0 + LANES] = (x1 * cos_n - x2 * sin_n).astype(BF16)
            kn_ref[0, rows, c0 + LANES:c0 + 2 * LANES] = (x2 * cos_n + x1 * sin_n).astype(BF16)
        ki_rope = nat[:, o_ki:o_ki + LANES] * cos_n + nat[:, o_ki + LANES:o_ki + 2 * LANES] * sin_n
        kin_ref[0, rows, :] = ki_rope[:, :IDX_DIM].astype(BF16)

        cos_t = cost_ref[:, rows]
        sin_t = sint_ref[:, rows]
        for h in range(ATT_HEADS):
            r0 = h * HEAD_DIM
            x1 = tr[r0:r0 + half]
            x2 = tr[r0 + half:r0 + HEAD_DIM]
            qt_ref[0, g, r0:r0 + half, :] = ((x1 * cos_t - x2 * sin_t) * q_scale).astype(BF16)
            qt_ref[0, g, r0 + half:r0 + HEAD_DIM, :] = ((x2 * cos_t + x1 * sin_t) * q_scale).astype(BF16)
        for h in range(IDX_HEADS):
            r0 = ATT_WIDTH + h * IDX_DIM
            x1 = tr[r0:r0 + half]
            x2 = tr[r0 + half:r0 + IDX_DIM]
            qit_ref[0, g, h * IDX_DIM:h * IDX_DIM + half, :] = (x1 * cos_t - x2 * sin_t).astype(BF16)
            qit_ref[0, g, h * IDX_DIM + half:(h + 1) * IDX_DIM, :] = (x2 * cos_t + x1 * sin_t).astype(BF16)
        for h in range(ATT_HEADS):
            vt_ref[0, g, h * V_ROWS:h * V_ROWS + HEAD_DIM, :] = (
                tr[o_v + h * HEAD_DIM:o_v + (h + 1) * HEAD_DIM].astype(BF16))
            vt_ref[0, g, h * V_ROWS + HEAD_DIM:(h + 1) * V_ROWS, :] = ones_pad
        wt_out_ref[0, g] = tr[o_v + ATT_WIDTH:o_v + ATT_WIDTH + IDX_HEADS] * w_scale

        gate_b = nat[:, o_c:o_c + CONV_WIDTH]
        u = nat[:, o_c + CONV_WIDTH:o_c + 2 * CONV_WIDTH] * nat[:, o_c + 2 * CONV_WIDTH:o_c + 3 * CONV_WIDTH]
        base = SUBLANES + g * tm
        u_ref[base:base + tm, :] = u
        um1 = u_ref[base - 1:base - 1 + tm, :]
        um2 = u_ref[base - 2:base - 2 + tm, :]
        conv_out = gate_b * (cw[0:1, :] * um2 + cw[1:2, :] * um1 + cw[2:3, :] * u)
        ms = jnp.mean(conv_out * conv_out, axis=-1, keepdims=True)
        cm_ref[0, rows, :] = (conv_out * lax.rsqrt(ms + NORM_EPS) * gconv_ref[...]).astype(BF16)

    tail = SUBLANES + PROJ_GROUPS * tm
    u_ref[0:SUBLANES, :] = u_ref[tail - SUBLANES:tail, :]


def _proj_call(x, wn, wt, cos_n, sin_n, cos_t, sin_t, conv_w, g_conv):
    bsz, seq, _ = x.shape
    tm = SEQ_TILE
    nt = seq // tm
    gm = PROJ_GROUPS * tm
    n_nat = wn.shape[1]
    n_tr = wt.shape[0]
    w_scale = (IDX_DIM ** -0.5) * (IDX_HEADS ** -0.5)
    const2 = lambda b, t: (0, 0)
    out_shapes = (
        jax.ShapeDtypeStruct((bsz, seq, ATT_WIDTH), BF16),
        jax.ShapeDtypeStruct((bsz, seq, IDX_DIM), BF16),
        jax.ShapeDtypeStruct((bsz, nt, ATT_WIDTH, tm), BF16),
        jax.ShapeDtypeStruct((bsz, nt, IDX_HEADS * IDX_DIM, tm), BF16),
        jax.ShapeDtypeStruct((bsz, nt, ATT_HEADS * V_ROWS, tm), BF16),
        jax.ShapeDtypeStruct((bsz, nt, IDX_HEADS, tm), F32),
        jax.ShapeDtypeStruct((bsz, seq, CONV_WIDTH), BF16),
    )
    tile4 = lambda b, t: (b, t, 0, 0)
    return pl.pallas_call(
        functools.partial(_proj_kernel, w_scale=w_scale),
        grid=(bsz, nt // PROJ_GROUPS),
        in_specs=[
            pl.BlockSpec((1, gm, D_MODEL), lambda b, t: (b, t, 0)),
            pl.BlockSpec((D_MODEL, n_nat), const2),
            pl.BlockSpec((n_tr, D_MODEL), const2),
            pl.BlockSpec((gm, LANES), lambda b, t: (t, 0)),
            pl.BlockSpec((gm, LANES), lambda b, t: (t, 0)),
            pl.BlockSpec((HEAD_DIM // 2, gm), lambda b, t: (0, t)),
            pl.BlockSpec((HEAD_DIM // 2, gm), lambda b, t: (0, t)),
            pl.BlockSpec((CONV_K, CONV_WIDTH), const2),
            pl.BlockSpec((1, CONV_WIDTH), const2),
        ],
        out_specs=(
            pl.BlockSpec((1, gm, ATT_WIDTH), lambda b, t: (b, t, 0)),
            pl.BlockSpec((1, gm, IDX_DIM), lambda b, t: (b, t, 0)),
            pl.BlockSpec((1, PROJ_GROUPS, ATT_WIDTH, tm), tile4),
            pl.BlockSpec((1, PROJ_GROUPS, IDX_HEADS * IDX_DIM, tm), tile4),
            pl.BlockSpec((1, PROJ_GROUPS, ATT_HEADS * V_ROWS, tm), tile4),
            pl.BlockSpec((1, PROJ_GROUPS, IDX_HEADS, tm), tile4),
            pl.BlockSpec((1, gm, CONV_WIDTH), lambda b, t: (b, t, 0)),
        ),
        out_shape=out_shapes,
        scratch_shapes=[pltpu.VMEM((gm + SUBLANES, CONV_WIDTH), F32)],
        compiler_params=pltpu.CompilerParams(
            dimension_semantics=("arbitrary", "arbitrary"), vmem_limit_bytes=VMEM_LIMIT),
        name="proj_rope_conv",
    )(x, wn, wt, cos_n, sin_n, cos_t, sin_t, conv_w, g_conv)


def _key_to_float(key):
    key = jnp.clip(key, KEY_NEG_INF, KEY_POS_INF)
    bits = jnp.where(key >= 0, key, key ^ 0x7FFFFFFF)
    return lax.bitcast_convert_type(bits, F32)


def _after(value, anchor):
    return jnp.maximum(value, jnp.minimum(anchor, -jnp.inf))


def _sorting_network(n):
    pairs = []

    def merge(lo, size, r):
        step = 2 * r
        if step < size:
            merge(lo, size, step)
            merge(lo + r, size, step)
            pairs.extend((i, i + r) for i in range(lo + r, lo + size - r, step))
        else:
            pairs.append((lo, lo + r))

    def sort(lo, size):
        if size > 1:
            sort(lo, size // 2)
            sort(lo + size // 2, size // 2)
            merge(lo, size, 1)

    sort(0, n)
    return pairs


def _sorted_group_count(grp, pred):
    size = len(grp)
    levels = size.bit_length() - 1
    masks = []

    def pick(cands, decisions):
        if len(cands) == 1:
            return cands[0]
        mid = len(cands) // 2
        return jnp.where(decisions[0], pick(cands[mid:], decisions[1:]), pick(cands[:mid], decisions[1:]))

    for level in range(1, levels + 1):
        step = size >> level
        cands = [grp[(2 * t + 1) * step - 1] for t in range(1 << (level - 1))]
        masks.append(pred(pick(cands, masks)))
    every = pred(grp[size - 1])
    count = jnp.where(every, 2, jnp.where(masks[levels - 1], 1, 0))
    for level in range(1, levels):
        count = count + jnp.where(masks[level - 1], size >> level, 0)
    return count


def _fold_rows(a):
    return a.reshape(a.shape[0] // SUBLANES, SUBLANES, a.shape[1])


def _dsa_kernel(kn_ref, kin_ref, vt_ref, qt_ref, qit_ref, w_ref, g_ref, out_ref,
                score_ref, sorted_ref, wbd_ref, acc_ref, at_ref, lg_ref, *, topk, seq):
    i = pl.program_id(1)
    tq = qt_ref.shape[3]
    tk = tq
    nchunks = i + 1
    neg_inf = jnp.float32(-jnp.inf)

    row_iota = lax.broadcasted_iota(jnp.int32, (tk, tq), 0)
    qpos = i * tq + lax.broadcasted_iota(jnp.int32, (tk, tq), 1)

    def chunk_off(c):
        return pl.multiple_of(c * tk, tk)

    def over_chunks(widths, body, carry):
        start = 0
        for width in widths:
            trips = (nchunks - start) // width
            carry = lax.fori_loop(
                0, trips, lambda it, cr, s=start, w=width: body([s + w * it + k for k in range(w)], cr), carry)
            start = start + trips * width
        return carry

    qi_all = qit_ref[0, 0]
    w_rows = w_ref[0, 0]

    def score_body(c, carry):
        off = chunk_off(c)
        kic = kin_ref[0, pl.ds(off, tk), :]
        sc = jnp.zeros((tk, tq), F32)
        for h in range(IDX_HEADS):
            d = jnp.dot(kic, qi_all[h * IDX_DIM:(h + 1) * IDX_DIM, :], preferred_element_type=F32)
            sc = sc + jnp.maximum(d, 0.0) * w_rows[h:h + 1, :]
        sc = jnp.where(off + row_iota <= qpos, sc, neg_inf)
        score_ref[pl.ds(off, tk), :] = sc
        rows = _fold_rows(sc)
        sorted_rows = []
        for g in range(tk // SUBLANES // SORT_GROUP):
            grp = [rows[SORT_GROUP * g + r] for r in range(SORT_GROUP)]
            for a, b in _sorting_network(SORT_GROUP):
                grp[a], grp[b] = jnp.maximum(grp[a], grp[b]), jnp.minimum(grp[a], grp[b])
            sorted_rows.extend(grp)
        sorted_ref[pl.ds(off, tk), :] = jnp.stack(sorted_rows).reshape(tk, tq)
        return carry

    over_chunks(SCORE_UNROLLS, lambda chunks, carry: functools.reduce(lambda cr, c: score_body(c, cr), chunks, carry), 0)

    row_iota3 = (lax.broadcasted_iota(jnp.int32, (tk // SUBLANES, SUBLANES, tq), 0) * SUBLANES
                 + lax.broadcasted_iota(jnp.int32, (tk // SUBLANES, SUBLANES, tq), 1))
    qpos3 = i * tq + lax.broadcasted_iota(jnp.int32, (tk // SUBLANES, SUBLANES, tq), 2)

    def count_total(chunk_partials, n_parts):
        def count_chunks(chunks, accs):
            for c in chunks:
                accs = tuple(acc + part for acc, part in zip(accs, chunk_partials(c)))
            return accs

        accs = over_chunks(COUNT_UNROLLS, count_chunks, (jnp.zeros((SUBLANES, tq), jnp.int32),) * n_parts)
        acc = functools.reduce(lambda x, y: x + y, accs)
        for shift in (4, 2, 1):
            acc = acc + pltpu.roll(acc, shift, axis=0)
        return acc

    def count_rows(pred_fn):
        def partials(c):
            off = chunk_off(c)
            m = pred_fn(_fold_rows(score_ref[pl.ds(off, tk), :]), off).astype(jnp.int32)
            per = m.shape[0] // COUNT_ACCS
            return [jnp.sum(m[g * per:(g + 1) * per], axis=0) for g in range(COUNT_ACCS)]
        return count_total(partials, COUNT_ACCS)

    blocks_per_chunk = tk // SUBLANES // SORT_GROUP

    def count_sorted(pred):
        def partials(c):
            rows = _fold_rows(sorted_ref[pl.ds(chunk_off(c), tk), :])
            return [_sorted_group_count([rows[SORT_GROUP * g + r] for r in range(SORT_GROUP)], pred)
                    for g in range(blocks_per_chunk)]
        return count_total(partials, blocks_per_chunk)

    def search_body(p, carry):
        u, n_u = carry
        trial = u | jnp.left_shift(jnp.int32(1), 31 - p)
        cand = _key_to_float(trial ^ INT_MIN)
        cnt = count_sorted(lambda sc: sc >= cand)
        take = cnt >= topk
        return jnp.where(take, trial, u), jnp.where(take, cnt, n_u)

    u, n_ge = lax.fori_loop(0, 32, search_body,
                            (jnp.zeros((SUBLANES, tq), jnp.int32),
                             jnp.zeros((SUBLANES, tq), jnp.int32) + nchunks * tk))
    thr = _key_to_float(u ^ INT_MIN)

    needs_rewrite = (jnp.max(n_ge) > topk) | (i * tq < topk)

    def resolve_ties():
        want = topk - count_sorted(lambda sc: sc > thr)

        def body(p, jp):
            trial = jp + jnp.left_shift(jnp.int32(1), (seq - 1).bit_length() - 1 - p)
            cnt = count_rows(lambda sc, off: (sc == thr[None]) & (off + row_iota3 <= trial[None]))
            return jnp.where(cnt < want, trial, jp)
        jlim = lax.fori_loop(0, (seq - 1).bit_length(), body, jnp.full((SUBLANES, tq), -1, jnp.int32)) + 1

        def rewrite(c, carry):
            off = chunk_off(c)
            sc = _fold_rows(score_ref[pl.ds(off, tk), :])
            kpos = off + row_iota3
            sel = ((sc > thr[None]) | ((sc == thr[None]) & (kpos <= jlim[None]))) & (kpos <= qpos3)
            score_ref[pl.ds(off, tk), :] = jnp.where(sel, 1.0, -1.0).reshape(tk, tq)
            return carry

        lax.fori_loop(0, nchunks, rewrite, 0)
        return jnp.zeros((SUBLANES, tq), F32)

    thr_sel = lax.cond(needs_rewrite, resolve_ties, lambda: thr)

    q_all = qt_ref[0, 0]
    half = HEAD_DIM // 2
    for h in range(ATT_HEADS):
        row_blocks = []
        for part in range(2):
            for hh in range(HEAD_GROUP):
                r0 = ((h // HEAD_GROUP) * HEAD_GROUP + hh) * HEAD_DIM + part * half
                row_blocks.append(q_all[r0:r0 + half] if hh == h % HEAD_GROUP else jnp.zeros((half, tq), BF16))
        wbd_ref[h] = jnp.concatenate(row_blocks, axis=0)
    acc_ref[...] = jnp.zeros(acc_ref.shape, F32)

    def attn_chunks(chunks, m_all):
        rows = len(chunks) * tk
        off = chunk_off(chunks[0])
        sel = _fold_rows(score_ref[pl.ds(off, rows), :]) >= thr_sel[None]
        bias = jnp.where(sel, 0.0, neg_inf).reshape(rows, tq)

        def head_logits(h):
            c0 = (h // HEAD_GROUP) * HEAD_GROUP * HEAD_DIM
            kc = kn_ref[0, pl.ds(off, rows), c0:c0 + HEAD_GROUP * HEAD_DIM]
            return jnp.dot(kc, wbd_ref[h], preferred_element_type=F32)

        m_run = [m_all[:, h * tq:(h + 1) * tq] for h in range(ATT_HEADS)]
        lgs_raw = [head_logits(0)]
        stage_a = []
        for h in range(ATT_HEADS):
            lg = lgs_raw[h] + bias
            m_old = m_run[h]
            if h + 1 < ATT_HEADS:
                lgs_raw.append(head_logits(h + 1))
                m_old = _after(m_old, lgs_raw[h + 1][0:1, :])
            m_new = jnp.maximum(m_old, jnp.max(jnp.max(_fold_rows(lg), axis=0), axis=0, keepdims=True))
            m_run[h] = m_new
            lg_ref[h, 0:rows, :] = lg
            stage_a.append((m_old, m_new))
        for h in range(ATT_HEADS):
            m_old, m_new = stage_a[h]
            m_use = jnp.where(m_new == neg_inf, 0.0, m_new)
            if h + 1 < ATT_HEADS:
                m_use = _after(m_use, stage_a[h + 1][1])
            alpha = jnp.exp2(m_old - m_use)
            p = jnp.exp2(lg_ref[h, 0:rows, :] - m_use).astype(BF16)
            v_t = jnp.concatenate([vt_ref[0, c, h * V_ROWS:(h + 1) * V_ROWS, :] for c in chunks], axis=1)
            acc_ref[h] = acc_ref[h] * alpha + jnp.dot(v_t, p, preferred_element_type=F32)
        return jnp.concatenate(m_run, axis=1)

    over_chunks(ATTN_UNROLLS, attn_chunks, jnp.full((1, ATT_HEADS * tq), neg_inf, F32))
    for h in range(ATT_HEADS):
        o = acc_ref[h]
        at_ref[h * HEAD_DIM:(h + 1) * HEAD_DIM, :] = o[:HEAD_DIM] / o[HEAD_DIM:HEAD_DIM + 1]

    a_t = at_ref[...]
    ms = jnp.mean(a_t * a_t, axis=0, keepdims=True)
    y = (a_t * lax.rsqrt(ms + NORM_EPS)).T
    out_ref[0] = (y * g_ref[...]).astype(BF16)


def _dsa_call(kn, kin, vt, qt, qit, w_t, g_attn, topk):
    bsz, seq, _ = kn.shape
    nt, tq = qt.shape[1], qt.shape[3]
    tile4 = lambda b, i: (b, i, 0, 0)
    return pl.pallas_call(
        functools.partial(_dsa_kernel, topk=topk, seq=seq),
        grid=(bsz, nt),
        in_specs=[
            pl.BlockSpec((1, seq, ATT_WIDTH), lambda b, i: (b, 0, 0)),
            pl.BlockSpec((1, seq, IDX_DIM), lambda b, i: (b, 0, 0)),
            pl.BlockSpec((1, nt, ATT_HEADS * V_ROWS, tq), lambda b, i: (b, 0, 0, 0)),
            pl.BlockSpec((1, 1, ATT_WIDTH, tq), tile4),
            pl.BlockSpec((1, 1, IDX_HEADS * IDX_DIM, tq), tile4),
            pl.BlockSpec((1, 1, IDX_HEADS, tq), tile4),
            pl.BlockSpec((1, ATT_WIDTH), lambda b, i: (0, 0)),
        ],
        out_specs=pl.BlockSpec((1, tq, ATT_WIDTH), lambda b, i: (b, i, 0)),
        out_shape=jax.ShapeDtypeStruct((bsz, seq, ATT_WIDTH), BF16),
        scratch_shapes=[
            pltpu.VMEM((seq, tq), F32),
            pltpu.VMEM((seq, tq), F32),
            pltpu.VMEM((ATT_HEADS, HEAD_GROUP * HEAD_DIM, tq), BF16),
            pltpu.VMEM((ATT_HEADS, V_ROWS, tq), F32),
            pltpu.VMEM((ATT_WIDTH, tq), F32),
            pltpu.VMEM((ATT_HEADS, max(ATTN_UNROLLS) * tq, tq), F32),
        ],
        compiler_params=pltpu.CompilerParams(
            dimension_semantics=("arbitrary", "arbitrary"), vmem_limit_bytes=VMEM_LIMIT),
        name="dsa_attention",
    )(kn, kin, vt, qt, qit, w_t, g_attn)


def _layer_norm(y, g, b):
    mu = jnp.mean(y, axis=-1, keepdims=True)
    d = y - mu
    var = jnp.mean(d * d, axis=-1, keepdims=True)
    return d * lax.rsqrt(var + NORM_EPS) * g + b


def _ffn_kernel(x_ref, am_ref, cm_ref, wout_ref, g1_ref, b1_ref, wgu_ref, wd_ref, g2_ref, b2_ref,
                o_ref, *, alpha, d_ff):
    n = FFN_SPLIT
    group = x_ref.shape[0] // n
    rows = [slice(r * group, (r + 1) * group) for r in range(n)]
    mix = [jnp.dot(am_ref[rw, :], wout_ref[0:ATT_WIDTH, :], preferred_element_type=F32)
           + jnp.dot(cm_ref[rw, :], wout_ref[ATT_WIDTH:D_MODEL, :], preferred_element_type=F32) for rw in rows]
    x1 = []
    for r in range(n):
        b1 = b1_ref[...] if r + 1 == n else _after(b1_ref[...], mix[r + 1][0:1, :])
        x1.append(_layer_norm(alpha * x_ref[rows[r], :] + mix[r], g1_ref[...], b1))
    gu = [jnp.dot(x1[r].astype(BF16), wgu_ref[...], preferred_element_type=F32) for r in range(n)]
    ffn = []
    for r in range(n):
        gate = gu[r][:, :d_ff]
        hidden = (gate * jax.nn.sigmoid(gate) * gu[r][:, d_ff:]).astype(BF16)
        ffn.append(jnp.dot(hidden, wd_ref[...], preferred_element_type=F32))
    for r in range(n):
        b2 = b2_ref[...] if r + 1 == n else _after(b2_ref[...], ffn[r + 1][0:1, :])
        o_ref[rows[r], :] = _layer_norm(alpha * x1[r] + ffn[r], g2_ref[...], b2)


def _ffn_call(x2d, am, cm, wout, g1, b1, wgu, wd, g2, b2, alpha):
    rows = x2d.shape[0]
    tm = ROW_TILE
    d_ff = wd.shape[0]
    row = lambda r: (r, 0)
    const = lambda r: (0, 0)
    resident = functools.partial(pl.BlockSpec, index_map=const, pipeline_mode=pl.Buffered(1))
    return pl.pallas_call(
        functools.partial(_ffn_kernel, alpha=alpha, d_ff=d_ff),
        grid=(rows // tm,),
        in_specs=[
            pl.BlockSpec((tm, D_MODEL), row),
            pl.BlockSpec((tm, ATT_WIDTH), row),
            pl.BlockSpec((tm, CONV_WIDTH), row),
            resident((D_MODEL, D_MODEL)),
            resident((1, D_MODEL)),
            resident((1, D_MODEL)),
            resident((D_MODEL, 2 * d_ff)),
            resident((d_ff, D_MODEL)),
            resident((1, D_MODEL)),
            resident((1, D_MODEL)),
        ],
        out_specs=pl.BlockSpec((tm, D_MODEL), row),
        out_shape=jax.ShapeDtypeStruct((rows, D_MODEL), F32),
        compiler_params=pltpu.CompilerParams(
            dimension_semantics=("arbitrary",), vmem_limit_bytes=VMEM_LIMIT),
        name="outproj_ffn",
    )(x2d, am, cm, wout, g1, b1, wgu, wd, g2, b2)


def _rotate_half_cols(w):
    d_in, n = w.shape
    w4 = w.reshape(d_in, n // HEAD_DIM, 2, HEAD_DIM // 2)
    return jnp.concatenate([-w4[:, :, 1:2, :], w4[:, :, 0:1, :]], axis=2).reshape(d_in, n)


def _group_halves_cols(w):
    d_in, n = w.shape
    w5 = w.reshape(d_in, n // HEAD_DIM // HEAD_GROUP, HEAD_GROUP, 2, HEAD_DIM // 2)
    return jnp.swapaxes(w5, 2, 3).reshape(d_in, n)


def _split_cols(w, sizes):
    out, o = [], 0
    for s in sizes:
        out.append(w[:, o:o + s])
        o += s
    return out


def kernel(x, w_in, conv_w, mix_norm_g, w_out, ln1_g, ln1_b, w_gate_up, w_down, ln2_g, ln2_b):
    bsz, seq, _ = x.shape
    depth = w_in.shape[0]
    alpha = (2 * depth) ** 0.25
    topk = min(INDEX_TOPK_MAX, seq // 4)
    assert seq % (SEQ_TILE * PROJ_GROUPS) == 0 and (bsz * seq) % ROW_TILE == 0 and topk <= SEQ_TILE

    inv_freq = 1.0 / (ROPE_THETA ** (np.arange(0, HEAD_DIM, 2, dtype=np.float64) / HEAD_DIM))
    ang = np.arange(seq, dtype=np.float64)[:, None] * inv_freq[None, :]
    cos, sin = np.cos(ang).astype(np.float32), np.sin(ang).astype(np.float32)
    cos_n, sin_n = np.tile(cos, (1, LANES // cos.shape[1])), np.tile(sin, (1, LANES // sin.shape[1]))
    cos_t, sin_t = np.ascontiguousarray(cos.T), np.ascontiguousarray(sin.T)

    for layer in range(depth):
        wq, wk, wv, wqi, wki, wwi, wgb, wgc, wh = _split_cols(w_in[layer].astype(BF16), IN_SPLIT_SIZES)
        pad_ki = jnp.zeros((D_MODEL, LANES - IDX_DIM), BF16)
        wn = jnp.concatenate([_group_halves_cols(wk), wgb, wgc, wh,
                              wki, pad_ki, _rotate_half_cols(wki), pad_ki], axis=1)
        pad_wi = jnp.zeros((D_MODEL, 2 * SUBLANES - IDX_HEADS), BF16)
        wt = jnp.concatenate([wq, wqi, wv, wwi, pad_wi], axis=1).T
        g = mix_norm_g[layer]
        kn, kin, qt, qit, vt, w_t, cm = _proj_call(
            x, wn, wt, cos_n, sin_n, cos_t, sin_t, conv_w[layer], g[ATT_WIDTH:][None, :])
        am = _dsa_call(kn, kin, vt, qt, qit, w_t, g[:ATT_WIDTH][None, :], topk)
        y = _ffn_call(
            x.reshape(bsz * seq, D_MODEL), am.reshape(bsz * seq, ATT_WIDTH), cm.reshape(bsz * seq, CONV_WIDTH),
            w_out[layer].astype(BF16), ln1_g[layer][None, :], ln1_b[layer][None, :],
            w_gate_up[layer].astype(BF16), w_down[layer].astype(BF16),
            ln2_g[layer][None, :], ln2_b[layer][None, :], alpha)
        x = y.reshape(bsz, seq, D_MODEL)
    return x
```

```python
import functools

import jax
import numpy as np
import jax.numpy as jnp
from jax import lax
from jax.experimental import pallas as pl
from jax.experimental.pallas import tpu as pltpu

D_MODEL = 1024
ATT_HEADS = 8
HEAD_DIM = 64
ATT_WIDTH = ATT_HEADS * HEAD_DIM
CONV_WIDTH = D_MODEL - ATT_WIDTH
CONV_K = 3
IDX_HEADS = 8
IDX_DIM = 64
INDEX_TOPK_MAX = 256
ROPE_THETA = 10000.0
NORM_EPS = 1e-5
IN_SPLIT_SIZES = (ATT_WIDTH, ATT_WIDTH, ATT_WIDTH, IDX_HEADS * IDX_DIM, IDX_DIM, IDX_HEADS,
                  CONV_WIDTH, CONV_WIDTH, CONV_WIDTH)

LANES = 128
SUBLANES = 8
SEQ_TILE = 256
PROJ_GROUPS = 4
ROW_TILE = 512
FFN_SPLIT = 2
SCORE_UNROLLS = (4, 2, 1)
COUNT_UNROLLS = (4, 1)
COUNT_ACCS = 4
SORT_GROUP = 16
ATTN_UNROLLS = (8, 4, 2, 1)
HEAD_GROUP = 4
V_ROWS = 80
LOG2E = 1.4426950408889634
VMEM_LIMIT = 56 * 1024 * 1024

F32 = jnp.float32
BF16 = jnp.bfloat16
INT_MIN = -2 ** 31
KEY_NEG_INF = (0xFF800000 ^ 0x7FFFFFFF) - 2 ** 32
KEY_POS_INF = 0x7F800000


def _proj_kernel(x_ref, wn_ref, wt_ref, cosn_ref, sinn_ref, cost_ref, sint_ref, convw_ref, gconv_ref,
                 kn_ref, kin_ref, qt_ref, qit_ref, vt_ref, wt_out_ref, cm_ref, u_ref, *, w_scale):
    t = pl.program_id(1)
    tm = qt_ref.shape[3]
    half = HEAD_DIM // 2
    q_scale = (HEAD_DIM ** -0.5) * LOG2E
    o_ki = ATT_WIDTH + 3 * CONV_WIDTH
    o_v = ATT_WIDTH + IDX_HEADS * IDX_DIM
    o_c = ATT_WIDTH
    pad_row = lax.broadcasted_iota(jnp.int32, (V_ROWS - HEAD_DIM, tm), 0)
    ones_pad = jnp.where(pad_row == 0, 1.0, 0.0).astype(BF16)
    cw = convw_ref[...]

    @pl.when(t == 0)
    def _():
        u_ref[0:SUBLANES, :] = jnp.zeros((SUBLANES, CONV_WIDTH), F32)

    xbs = [x_ref[0, g * tm:(g + 1) * tm, :].astype(BF16) for g in range(PROJ_GROUPS)]
    nats = [jnp.dot(xb, wn_ref[...], preferred_element_type=F32) for xb in xbs]
    trs = [lax.dot_general(wt_ref[...], xb, (((1,), (1,)), ((), ())), preferred_element_type=F32)
           for xb in xbs]

    for g in range(PROJ_GROUPS):
        nat, tr = nats[g], trs[g]
        rows = slice(g * tm, (g + 1) * tm)

        cos_n = cosn_ref[rows, :]
        sin_n = sinn_ref[rows, :]
        for grp in range(ATT_HEADS // HEAD_GROUP):
            c0 = grp * 2 * LANES
            x1 = nat[:, c0:c0 + LANES]
            x2 = nat[:, c0 + LANES:c0 + 2 * LANES]
            kn_ref[0, rows, c0:c0 + LANES] = (x1 * cos_n - x2 * sin_n).astype(BF16)
            kn_ref[0, rows, c0 + LANES:c0 + 2 * LANES] = (x2 * cos_n + x1 * sin_n).astype(BF16)
        ki_rope = nat[:, o_ki:o_ki + LANES] * cos_n + nat[:, o_ki + LANES:o_ki + 2 * LANES] * sin_n
        kin_ref[0, rows, :] = ki_rope[:, :IDX_DIM].astype(BF16)

        cos_t = cost_ref[:, rows]
        sin_t = sint_ref[:, rows]
        for h in range(ATT_HEADS):
            r0 = h * HEAD_DIM
            x1 = tr[r0:r0 + half]
            x2 = tr[r0 + half:r0 + HEAD_DIM]
            qt_ref[0, g, r0:r0 + half, :] = ((x1 * cos_t - x2 * sin_t) * q_scale).astype(BF16)
            qt_ref[0, g, r0 + half:r0 + HEAD_DIM, :] = ((x2 * cos_t + x1 * sin_t) * q_scale).astype(BF16)
        for h in range(IDX_HEADS):
            r0 = ATT_WIDTH + h * IDX_DIM
            x1 = tr[r0:r0 + half]
            x2 = tr[r0 + half:r0 + IDX_DIM]
            qit_ref[0, g, h * IDX_DIM:h * IDX_DIM + half, :] = (x1 * cos_t - x2 * sin_t).astype(BF16)
            qit_ref[0, g, h * IDX_DIM + half:(h + 1) * IDX_DIM, :] = (x2 * cos_t + x1 * sin_t).astype(BF16)
        for h in range(ATT_HEADS):
            vt_ref[0, g, h * V_ROWS:h * V_ROWS + HEAD_DIM, :] = (
                tr[o_v + h * HEAD_DIM:o_v + (h + 1) * HEAD_DIM].astype(BF16))
            vt_ref[0, g, h * V_ROWS + HEAD_DIM:(h + 1) * V_ROWS, :] = ones_pad
        wt_out_ref[0, g] = tr[o_v + ATT_WIDTH:o_v + ATT_WIDTH + IDX_HEADS] * w_scale

        gate_b = nat[:, o_c:o_c + CONV_WIDTH]
        u = nat[:, o_c + CONV_WIDTH:o_c + 2 * CONV_WIDTH] * nat[:, o_c + 2 * CONV_WIDTH:o_c + 3 * CONV_WIDTH]
        base = SUBLANES + g * tm
        u_ref[base:base + tm, :] = u
        um1 = u_ref[base - 1:base - 1 + tm, :]
        um2 = u_ref[base - 2:base - 2 + tm, :]
        conv_out = gate_b * (cw[0:1, :] * um2 + cw[1:2, :] * um1 + cw[2:3, :] * u)
        ms = jnp.mean(conv_out * conv_out, axis=-1, keepdims=True)
        cm_ref[0, rows, :] = (conv_out * lax.rsqrt(ms + NORM_EPS) * gconv_ref[...]).astype(BF16)

    tail = SUBLANES + PROJ_GROUPS * tm
    u_ref[0:SUBLANES, :] = u_ref[tail - SUBLANES:tail, :]


def _proj_call(x, wn, wt, cos_n, sin_n, cos_t, sin_t, conv_w, g_conv):
    bsz, seq, _ = x.shape
    tm = SEQ_TILE
    nt = seq // tm
    gm = PROJ_GROUPS * tm
    n_nat = wn.shape[1]
    n_tr = wt.shape[0]
    w_scale = (IDX_DIM ** -0.5) * (IDX_HEADS ** -0.5)
    const2 = lambda b, t: (0, 0)
    out_shapes = (
        jax.ShapeDtypeStruct((bsz, seq, ATT_WIDTH), BF16),
        jax.ShapeDtypeStruct((bsz, seq, IDX_DIM), BF16),
        jax.ShapeDtypeStruct((bsz, nt, ATT_WIDTH, tm), BF16),
        jax.ShapeDtypeStruct((bsz, nt, IDX_HEADS * IDX_DIM, tm), BF16),
        jax.ShapeDtypeStruct((bsz, nt, ATT_HEADS * V_ROWS, tm), BF16),
        jax.ShapeDtypeStruct((bsz, nt, IDX_HEADS, tm), F32),
        jax.ShapeDtypeStruct((bsz, seq, CONV_WIDTH), BF16),
    )
    tile4 = lambda b, t: (b, t, 0, 0)
    return pl.pallas_call(
        functools.partial(_proj_kernel, w_scale=w_scale),
        grid=(bsz, nt // PROJ_GROUPS),
        in_specs=[
            pl.BlockSpec((1, gm, D_MODEL), lambda b, t: (b, t, 0)),
            pl.BlockSpec((D_MODEL, n_nat), const2),
            pl.BlockSpec((n_tr, D_MODEL), const2),
            pl.BlockSpec((gm, LANES), lambda b, t: (t, 0)),
            pl.BlockSpec((gm, LANES), lambda b, t: (t, 0)),
            pl.BlockSpec((HEAD_DIM // 2, gm), lambda b, t: (0, t)),
            pl.BlockSpec((HEAD_DIM // 2, gm), lambda b, t: (0, t)),
            pl.BlockSpec((CONV_K, CONV_WIDTH), const2),
            pl.BlockSpec((1, CONV_WIDTH), const2),
        ],
        out_specs=(
            pl.BlockSpec((1, gm, ATT_WIDTH), lambda b, t: (b, t, 0)),
            pl.BlockSpec((1, gm, IDX_DIM), lambda b, t: (b, t, 0)),
            pl.BlockSpec((1, PROJ_GROUPS, ATT_WIDTH, tm), tile4),
            pl.BlockSpec((1, PROJ_GROUPS, IDX_HEADS * IDX_DIM, tm), tile4),
            pl.BlockSpec((1, PROJ_GROUPS, ATT_HEADS * V_ROWS, tm), tile4),
            pl.BlockSpec((1, PROJ_GROUPS, IDX_HEADS, tm), tile4),
            pl.BlockSpec((1, gm, CONV_WIDTH), lambda b, t: (b, t, 0)),
        ),
        out_shape=out_shapes,
        scratch_shapes=[pltpu.VMEM((gm + SUBLANES, CONV_WIDTH), F32)],
        compiler_params=pltpu.CompilerParams(
            dimension_semantics=("arbitrary", "arbitrary"), vmem_limit_bytes=VMEM_LIMIT),
        name="proj_rope_conv",
    )(x, wn, wt, cos_n, sin_n, cos_t, sin_t, conv_w, g_conv)


def _key_to_float(key):
    key = jnp.clip(key, KEY_NEG_INF, KEY_POS_INF)
    bits = jnp.where(key >= 0, key, key ^ 0x7FFFFFFF)
    return lax.bitcast_convert_type(bits, F32)


def _after(value, anchor):
    return jnp.maximum(value, jnp.minimum(anchor, -jnp.inf))


def _sorting_network(n):
    pairs = []

    def merge(lo, size, r):
        step = 2 * r
        if step < size:
            merge(lo, size, step)
            merge(lo + r, size, step)
            pairs.extend((i, i + r) for i in range(lo + r, lo + size - r, step))
        else:
            pairs.append((lo, lo + r))

    def sort(lo, size):
        if size > 1:
            sort(lo, size // 2)
            sort(lo + size // 2, size // 2)
            merge(lo, size, 1)

    sort(0, n)
    return pairs


def _sorted_group_count(grp, pred):
    size = len(grp)
    levels = size.bit_length() - 1
    masks = []

    def pick(cands, decisions):
        if len(cands) == 1:
            return cands[0]
        mid = len(cands) // 2
        return jnp.where(decisions[0], pick(cands[mid:], decisions[1:]), pick(cands[:mid], decisions[1:]))

    for level in range(1, levels + 1):
        step = size >> level
        cands = [grp[(2 * t + 1) * step - 1] for t in range(1 << (level - 1))]
        masks.append(pred(pick(cands, masks)))
    every = pred(grp[size - 1])
    count = jnp.where(every, 2, jnp.where(masks[levels - 1], 1, 0))
    for level in range(1, levels):
        count = count + jnp.where(masks[level - 1], size >> level, 0)
    return count


def _fold_rows(a):
    return a.reshape(a.shape[0] // SUBLANES, SUBLANES, a.shape[1])


def _dsa_kernel(kn_ref, kin_ref, vt_ref, qt_ref, qit_ref, w_ref, g_ref, out_ref,
                score_ref, sorted_ref, wbd_ref, acc_ref, at_ref, lg_ref, *, topk, seq):
    i = pl.program_id(1)
    tq = qt_ref.shape[3]
    tk = tq
    nchunks = i + 1
    neg_inf = jnp.float32(-jnp.inf)

    row_iota3 = (lax.broadcasted_iota(jnp.int32, (tk // SUBLANES, SUBLANES, tq), 0) * SUBLANES
                 + lax.broadcasted_iota(jnp.int32, (tk // SUBLANES, SUBLANES, tq), 1))
    qpos3 = i * tq + lax.broadcasted_iota(jnp.int32, (tk // SUBLANES, SUBLANES, tq), 2)
    qpos_row = i * tq + lax.broadcasted_iota(jnp.int32, (SUBLANES, tq), 1)

    def chunk_off(c):
        return pl.multiple_of(c * tk, tk)

    def over_chunks(widths, body, carry):
        start = 0
        for width in widths:
            trips = (nchunks - start) // width
            carry = lax.fori_loop(
                0, trips, lambda it, cr, s=start, w=width: body([s + w * it + k for k in range(w)], cr), carry)
            start = start + trips * width
        return carry

    qi_all = qit_ref[0, 0]
    w_rows = w_ref[0, 0]

    def score_body(c, carry):
        off = chunk_off(c)
        kic = kin_ref[0, pl.ds(off, tk), :]
        sc = jnp.zeros((tk, tq), F32)
        for h in range(IDX_HEADS):
            d = jnp.dot(kic, qi_all[h * IDX_DIM:(h + 1) * IDX_DIM, :], preferred_element_type=F32)
            sc = sc + jnp.maximum(d, 0.0) * w_rows[h:h + 1, :]
        sc = jnp.where(row_iota3 <= (qpos_row - off)[None], _fold_rows(sc), neg_inf).reshape(tk, tq)
        score_ref[pl.ds(off, tk), :] = sc
        rows = _fold_rows(sc)
        sorted_rows = []
        for g in range(tk // SUBLANES // SORT_GROUP):
            grp = [rows[SORT_GROUP * g + r] for r in range(SORT_GROUP)]
            for a, b in _sorting_network(SORT_GROUP):
                grp[a], grp[b] = jnp.maximum(grp[a], grp[b]), jnp.minimum(grp[a], grp[b])
            sorted_rows.extend(grp)
        sorted_ref[pl.ds(off, tk), :] = jnp.stack(sorted_rows).reshape(tk, tq)
        return carry

    over_chunks(SCORE_UNROLLS, lambda chunks, carry: functools.reduce(lambda cr, c: score_body(c, cr), chunks, carry), 0)

    def count_total(chunk_partials, n_parts):
        def count_chunks(chunks, accs):
            for c in chunks:
                accs = tuple(acc + part for acc, part in zip(accs, chunk_partials(c)))
            return accs

        accs = over_chunks(COUNT_UNROLLS, count_chunks, (jnp.zeros((SUBLANES, tq), jnp.int32),) * n_parts)
        acc = functools.reduce(lambda x, y: x + y, accs)
        for shift in (4, 2, 1):
            acc = acc + pltpu.roll(acc, shift, axis=0)
        return acc

    def count_rows(pred_fn):
        def partials(c):
            off = chunk_off(c)
            m = pred_fn(_fold_rows(score_ref[pl.ds(off, tk), :]), off).astype(jnp.int32)
            per = m.shape[0] // COUNT_ACCS
            return [jnp.sum(m[g * per:(g + 1) * per], axis=0) for g in range(COUNT_ACCS)]
        return count_total(partials, COUNT_ACCS)

    blocks_per_chunk = tk // SUBLANES // SORT_GROUP

    def count_sorted(pred):
        def partials(c):
            rows = _fold_rows(sorted_ref[pl.ds(chunk_off(c), tk), :])
            return [_sorted_group_count([rows[SORT_GROUP * g + r] for r in range(SORT_GROUP)], pred)
                    for g in range(blocks_per_chunk)]
        return count_total(partials, blocks_per_chunk)

    def search_body(p, carry):
        u, n_u = carry
        trial = u | jnp.left_shift(jnp.int32(1), 31 - p)
        cand = _key_to_float(trial ^ INT_MIN)
        cnt = count_sorted(lambda sc: sc >= cand)
        take = cnt >= topk
        return jnp.where(take, trial, u), jnp.where(take, cnt, n_u)

    search_init = (jnp.zeros((SUBLANES, tq), jnp.int32), jnp.zeros((SUBLANES, tq), jnp.int32) + nchunks * tk)
    u, n_ge = lax.cond((i + 1) * tq > topk, lambda: lax.fori_loop(0, 32, search_body, search_init),
                       lambda: search_init)
    thr = _key_to_float(u ^ INT_MIN)

    needs_rewrite = (jnp.max(n_ge) > topk) | (i * tq < topk)

    def resolve_ties():
        want = topk - count_sorted(lambda sc: sc > thr)

        def body(p, jp):
            trial = jp + jnp.left_shift(jnp.int32(1), (seq - 1).bit_length() - 1 - p)
            cnt = count_rows(lambda sc, off: (sc == thr[None]) & (off + row_iota3 <= trial[None]))
            return jnp.where(cnt < want, trial, jp)
        jlim = lax.fori_loop(0, (seq - 1).bit_length(), body, jnp.full((SUBLANES, tq), -1, jnp.int32)) + 1

        def rewrite(c, carry):
            off = chunk_off(c)
            sc = _fold_rows(score_ref[pl.ds(off, tk), :])
            kpos = off + row_iota3
            sel = ((sc > thr[None]) | ((sc == thr[None]) & (kpos <= jlim[None]))) & (kpos <= qpos3)
            score_ref[pl.ds(off, tk), :] = jnp.where(sel, 1.0, -1.0).reshape(tk, tq)
            return carry

        lax.fori_loop(0, nchunks, rewrite, 0)
        return jnp.zeros((SUBLANES, tq), F32)

    thr_sel = lax.cond(needs_rewrite, resolve_ties, lambda: thr)

    q_all = qt_ref[0, 0]
    half = HEAD_DIM // 2
    for h in range(ATT_HEADS):
        row_blocks = []
        for part in range(2):
            for hh in range(HEAD_GROUP):
                r0 = ((h // HEAD_GROUP) * HEAD_GROUP + hh) * HEAD_DIM + part * half
                row_blocks.append(q_all[r0:r0 + half] if hh == h % HEAD_GROUP else jnp.zeros((half, tq), BF16))
        wbd_ref[h] = jnp.concatenate(row_blocks, axis=0)
    acc_ref[...] = jnp.zeros(acc_ref.shape, F32)

    def attn_chunks(chunks, m_all):
        rows = len(chunks) * tk
        off = chunk_off(chunks[0])
        sel = _fold_rows(score_ref[pl.ds(off, rows), :]) >= thr_sel[None]
        bias = jnp.where(sel, 0.0, neg_inf).reshape(rows, tq)

        def head_logits(h):
            c0 = (h // HEAD_GROUP) * HEAD_GROUP * HEAD_DIM
            kc = kn_ref[0, pl.ds(off, rows), c0:c0 + HEAD_GROUP * HEAD_DIM]
            return jnp.dot(kc, wbd_ref[h], preferred_element_type=F32)

        m_run = [m_all[:, h * tq:(h + 1) * tq] for h in range(ATT_HEADS)]
        lgs_raw = [head_logits(0)]
        stage_a = []
        for h in range(ATT_HEADS):
            lg = lgs_raw[h] + bias
            m_old = m_run[h]
            if h + 1 < ATT_HEADS:
                lgs_raw.append(head_logits(h + 1))
                m_old = _after(m_old, lgs_raw[h + 1][0:1, :])
            m_new = jnp.maximum(m_old, jnp.max(jnp.max(_fold_rows(lg), axis=0), axis=0, keepdims=True))
            m_run[h] = m_new
            lg_ref[h, 0:rows, :] = lg
            stage_a.append((m_old, m_new))
        for h in range(ATT_HEADS):
            m_old, m_new = stage_a[h]
            m_use = jnp.where(m_new == neg_inf, 0.0, m_new)
            if h + 1 < ATT_HEADS:
                m_use = _after(m_use, stage_a[h + 1][1])
            alpha = jnp.exp2(m_old - m_use)
            p = jnp.exp2(lg_ref[h, 0:rows, :] - m_use).astype(BF16)
            v_t = jnp.concatenate([vt_ref[0, c, h * V_ROWS:(h + 1) * V_ROWS, :] for c in chunks], axis=1)
            acc_ref[h] = acc_ref[h] * alpha + jnp.dot(v_t, p, preferred_element_type=F32)
        return jnp.concatenate(m_run, axis=1)

    over_chunks(ATTN_UNROLLS, attn_chunks, jnp.full((1, ATT_HEADS * tq), neg_inf, F32))
    for h in range(ATT_HEADS):
        o = acc_ref[h]
        at_ref[h * HEAD_DIM:(h + 1) * HEAD_DIM, :] = o[:HEAD_DIM] / o[HEAD_DIM:HEAD_DIM + 1]

    a_t = at_ref[...]
    ms = jnp.mean(a_t * a_t, axis=0, keepdims=True)
    y = (a_t * lax.rsqrt(ms + NORM_EPS)).T
    out_ref[0] = (y * g_ref[...]).astype(BF16)


def _dsa_call(kn, kin, vt, qt, qit, w_t, g_attn, topk):
    bsz, seq, _ = kn.shape
    nt, tq = qt.shape[1], qt.shape[3]
    tile4 = lambda b, i: (b, i, 0, 0)
    return pl.pallas_call(
        functools.partial(_dsa_kernel, topk=topk, seq=seq),
        grid=(bsz, nt),
        in_specs=[
            pl.BlockSpec((1, seq, ATT_WIDTH), lambda b, i: (b, 0, 0)),
            pl.BlockSpec((1, seq, IDX_DIM), lambda b, i: (b, 0, 0)),
            pl.BlockSpec((1, nt, ATT_HEADS * V_ROWS, tq), lambda b, i: (b, 0, 0, 0)),
            pl.BlockSpec((1, 1, ATT_WIDTH, tq), tile4),
            pl.BlockSpec((1, 1, IDX_HEADS * IDX_DIM, tq), tile4),
            pl.BlockSpec((1, 1, IDX_HEADS, tq), tile4),
            pl.BlockSpec((1, ATT_WIDTH), lambda b, i: (0, 0)),
        ],
        out_specs=pl.BlockSpec((1, tq, ATT_WIDTH), lambda b, i: (b, i, 0)),
        out_shape=jax.ShapeDtypeStruct((bsz, seq, ATT_WIDTH), BF16),
        scratch_shapes=[
            pltpu.VMEM((seq, tq), F32),
            pltpu.VMEM((seq, tq), F32),
            pltpu.VMEM((ATT_HEADS, HEAD_GROUP * HEAD_DIM, tq), BF16),
            pltpu.VMEM((ATT_HEADS, V_ROWS, tq), F32),
            pltpu.VMEM((ATT_WIDTH, tq), F32),
            pltpu.VMEM((ATT_HEADS, max(ATTN_UNROLLS) * tq, tq), F32),
        ],
        compiler_params=pltpu.CompilerParams(
            dimension_semantics=("arbitrary", "arbitrary"), vmem_limit_bytes=VMEM_LIMIT),
        name="dsa_attention",
    )(kn, kin, vt, qt, qit, w_t, g_attn)


def _layer_norm(y, g, b):
    mu = jnp.mean(y, axis=-1, keepdims=True)
    d = y - mu
    var = jnp.mean(d * d, axis=-1, keepdims=True)
    return d * lax.rsqrt(var + NORM_EPS) * g + b


def _ffn_kernel(x_ref, am_ref, cm_ref, wout_ref, g1_ref, b1_ref, wgu_ref, wd_ref, g2_ref, b2_ref,
                o_ref, *, alpha, d_ff):
    n = FFN_SPLIT
    group = x_ref.shape[0] // n
    rows = [slice(r * group, (r + 1) * group) for r in range(n)]
    mix = [jnp.dot(am_ref[rw, :], wout_ref[0:ATT_WIDTH, :], preferred_element_type=F32)
           + jnp.dot(cm_ref[rw, :], wout_ref[ATT_WIDTH:D_MODEL, :], preferred_element_type=F32) for rw in rows]
    x1 = []
    for r in range(n):
        b1 = b1_ref[...] if r + 1 == n else _after(b1_ref[...], mix[r + 1][0:1, :])
        x1.append(_layer_norm(alpha * x_ref[rows[r], :] + mix[r], g1_ref[...], b1))
    gu = [jnp.dot(x1[r].astype(BF16), wgu_ref[...], preferred_element_type=F32) for r in range(n)]
    ffn = []
    for r in range(n):
        gate = gu[r][:, :d_ff]
        hidden = (gate * jax.nn.sigmoid(gate) * gu[r][:, d_ff:]).astype(BF16)
        ffn.append(jnp.dot(hidden, wd_ref[...], preferred_element_type=F32))
    for r in range(n):
        b2 = b2_ref[...] if r + 1 == n else _after(b2_ref[...], ffn[r + 1][0:1, :])
        o_ref[rows[r], :] = _layer_norm(alpha * x1[r] + ffn[r], g2_ref[...], b2)


def _ffn_call(x2d, am, cm, wout, g1, b1, wgu, wd, g2, b2, alpha):
    rows = x2d.shape[0]
    tm = ROW_TILE
    d_ff = wd.shape[0]
    row = lambda r: (r, 0)
    const = lambda r: (0, 0)
    resident = functools.partial(pl.BlockSpec, index_map=const, pipeline_mode=pl.Buffered(1))
    return pl.pallas_call(
        functools.partial(_ffn_kernel, alpha=alpha, d_ff=d_ff),
        grid=(rows // tm,),
        in_specs=[
            pl.BlockSpec((tm, D_MODEL), row),
            pl.BlockSpec((tm, ATT_WIDTH), row),
            pl.BlockSpec((tm, CONV_WIDTH), row),
            resident((D_MODEL, D_MODEL)),
            resident((1, D_MODEL)),
            resident((1, D_MODEL)),
            resident((D_MODEL, 2 * d_ff)),
            resident((d_ff, D_MODEL)),
            resident((1, D_MODEL)),
            resident((1, D_MODEL)),
        ],
        out_specs=pl.BlockSpec((tm, D_MODEL), row),
        out_shape=jax.ShapeDtypeStruct((rows, D_MODEL), F32),
        compiler_params=pltpu.CompilerParams(
            dimension_semantics=("arbitrary",), vmem_limit_bytes=VMEM_LIMIT),
        name="outproj_ffn",
    )(x2d, am, cm, wout, g1, b1, wgu, wd, g2, b2)


def _rotate_half_cols(w):
    d_in, n = w.shape
    w4 = w.reshape(d_in, n // HEAD_DIM, 2, HEAD_DIM // 2)
    return jnp.concatenate([-w4[:, :, 1:2, :], w4[:, :, 0:1, :]], axis=2).reshape(d_in, n)


def _group_halves_cols(w):
    d_in, n = w.shape
    w5 = w.reshape(d_in, n // HEAD_DIM // HEAD_GROUP, HEAD_GROUP, 2, HEAD_DIM // 2)
    return jnp.swapaxes(w5, 2, 3).reshape(d_in, n)


def _split_cols(w, sizes):
    out, o = [], 0
    for s in sizes:
        out.append(w[:, o:o + s])
        o += s
    return out


def kernel(x, w_in, conv_w, mix_norm_g, w_out, ln1_g, ln1_b, w_gate_up, w_down, ln2_g, ln2_b):
    bsz, seq, _ = x.shape
    depth = w_in.shape[0]
    alpha = (2 * depth) ** 0.25
    topk = min(INDEX_TOPK_MAX, seq // 4)
    assert seq % (SEQ_TILE * PROJ_GROUPS) == 0 and (bsz * seq) % ROW_TILE == 0 and topk <= SEQ_TILE

    inv_freq = 1.0 / (ROPE_THETA ** (np.arange(0, HEAD_DIM, 2, dtype=np.float64) / HEAD_DIM))
    ang = np.arange(seq, dtype=np.float64)[:, None] * inv_freq[None, :]
    cos, sin = np.cos(ang).astype(np.float32), np.sin(ang).astype(np.float32)
    cos_n, sin_n = np.tile(cos, (1, LANES // cos.shape[1])), np.tile(sin, (1, LANES // sin.shape[1]))
    cos_t, sin_t = np.ascontiguousarray(cos.T), np.ascontiguousarray(sin.T)

    for layer in range(depth):
        wq, wk, wv, wqi, wki, wwi, wgb, wgc, wh = _split_cols(w_in[layer].astype(BF16), IN_SPLIT_SIZES)
        pad_ki = jnp.zeros((D_MODEL, LANES - IDX_DIM), BF16)
        wn = jnp.concatenate([_group_halves_cols(wk), wgb, wgc, wh,
                              wki, pad_ki, _rotate_half_cols(wki), pad_ki], axis=1)
        pad_wi = jnp.zeros((D_MODEL, 2 * SUBLANES - IDX_HEADS), BF16)
        wt = jnp.concatenate([wq, wqi, wv, wwi, pad_wi], axis=1).T
        g = mix_norm_g[layer]
        kn, kin, qt, qit, vt, w_t, cm = _proj_call(
            x, wn, wt, cos_n, sin_n, cos_t, sin_t, conv_w[layer], g[ATT_WIDTH:][None, :])
        am = _dsa_call(kn, kin, vt, qt, qit, w_t, g[:ATT_WIDTH][None, :], topk)
        y = _ffn_call(
            x.reshape(bsz * seq, D_MODEL), am.reshape(bsz * seq, ATT_WIDTH), cm.reshape(bsz * seq, CONV_WIDTH),
            w_out[layer].astype(BF16), ln1_g[layer][None, :], ln1_b[layer][None, :],
            w_gate_up[layer].astype(BF16), w_down[layer].astype(BF16),
            ln2_g[layer][None, :], ln2_b[layer][None, :], alpha)
        x = y.reshape(bsz, seq, D_MODEL)
    return x
```

```python
import functools

import jax
import numpy as np
import jax.numpy as jnp
from jax import lax
from jax.experimental import pallas as pl
from jax.experimental.pallas import tpu as pltpu

D_MODEL = 1024
ATT_HEADS = 8
HEAD_DIM = 64
ATT_WIDTH = ATT_HEADS * HEAD_DIM
CONV_WIDTH = D_MODEL - ATT_WIDTH
CONV_K = 3
IDX_HEADS = 8
IDX_DIM = 64
INDEX_TOPK_MAX = 256
ROPE_THETA = 10000.0
NORM_EPS = 1e-5
IN_SPLIT_SIZES = (ATT_WIDTH, ATT_WIDTH, ATT_WIDTH, IDX_HEADS * IDX_DIM, IDX_DIM, IDX_HEADS,
                  CONV_WIDTH, CONV_WIDTH, CONV_WIDTH)

LANES = 128
SUBLANES = 8
SEQ_TILE = 256
PROJ_GROUPS = 4
ROW_TILE = 512
FFN_SPLIT = 2
SCORE_UNROLLS = (4, 2, 1)
COUNT_UNROLLS = (4, 1)
COUNT_ACCS = 4
SORT_GROUP = 16
ATTN_UNROLLS = (8, 4, 2, 1)
HEAD_GROUP = 4
V_ROWS = 80
LOG2E = 1.4426950408889634
VMEM_LIMIT = 56 * 1024 * 1024

F32 = jnp.float32
BF16 = jnp.bfloat16
INT_MIN = -2 ** 31
KEY_NEG_INF = (0xFF800000 ^ 0x7FFFFFFF) - 2 ** 32
KEY_POS_INF = 0x7F800000


def _proj_kernel(x_ref, wn_ref, wt_ref, cosn_ref, sinn_ref, cost_ref, sint_ref, convw_ref, gconv_ref,
                 kn_ref, kin_ref, qt_ref, qit_ref, vt_ref, wt_out_ref, cm_ref, u_ref, *, w_scale):
    t = pl.program_id(1)
    tm = qt_ref.shape[3]
    half = HEAD_DIM // 2
    q_scale = (HEAD_DIM ** -0.5) * LOG2E
    o_ki = ATT_WIDTH + 3 * CONV_WIDTH
    o_v = ATT_WIDTH + IDX_HEADS * IDX_DIM
    o_c = ATT_WIDTH
    pad_row = lax.broadcasted_iota(jnp.int32, (V_ROWS - HEAD_DIM, tm), 0)
    ones_pad = jnp.where(pad_row == 0, 1.0, 0.0).astype(BF16)
    cw = convw_ref[...]

    @pl.when(t == 0)
    def _():
        u_ref[0:SUBLANES, :] = jnp.zeros((SUBLANES, CONV_WIDTH), F32)

    xbs = [x_ref[0, g * tm:(g + 1) * tm, :].astype(BF16) for g in range(PROJ_GROUPS)]
    nats = [jnp.dot(xb, wn_ref[...], preferred_element_type=F32) for xb in xbs]
    trs = [lax.dot_general(wt_ref[...], xb, (((1,), (1,)), ((), ())), preferred_element_type=F32)
           for xb in xbs]

    for g in range(PROJ_GROUPS):
        nat, tr = nats[g], trs[g]
        rows = slice(g * tm, (g + 1) * tm)

        cos_n = cosn_ref[rows, :]
        sin_n = sinn_ref[rows, :]
        for grp in range(ATT_HEADS // HEAD_GROUP):
            c0 = grp * 2 * LANES
            x1 = nat[:, c0:c0 + LANES]
            x2 = nat[:, c0 + LANES:c0 + 2 * LANES]
            kn_ref[0, rows, c0:c0 + LANES] = (x1 * cos_n - x2 * sin_n).astype(BF16)
            kn_ref[0, rows, c0 + LANES:c0 + 2 * LANES] = (x2 * cos_n + x1 * sin_n).astype(BF16)
        ki_rope = nat[:, o_ki:o_ki + LANES] * cos_n + nat[:, o_ki + LANES:o_ki + 2 * LANES] * sin_n
        kin_ref[0, rows, :] = ki_rope[:, :IDX_DIM].astype(BF16)

        cos_t = cost_ref[:, rows]
        sin_t = sint_ref[:, rows]
        for h in range(ATT_HEADS):
            r0 = h * HEAD_DIM
            x1 = tr[r0:r0 + half]
            x2 = tr[r0 + half:r0 + HEAD_DIM]
            qt_ref[0, g, r0:r0 + half, :] = ((x1 * cos_t - x2 * sin_t) * q_scale).astype(BF16)
            qt_ref[0, g, r0 + half:r0 + HEAD_DIM, :] = ((x2 * cos_t + x1 * sin_t) * q_scale).astype(BF16)
        for h in range(IDX_HEADS):
            r0 = ATT_WIDTH + h * IDX_DIM
            x1 = tr[r0:r0 + half]
            x2 = tr[r0 + half:r0 + IDX_DIM]
            qit_ref[0, g, h * IDX_DIM:h * IDX_DIM + half, :] = (x1 * cos_t - x2 * sin_t).astype(BF16)
            qit_ref[0, g, h * IDX_DIM + half:(h + 1) * IDX_DIM, :] = (x2 * cos_t + x1 * sin_t).astype(BF16)
        for h in range(ATT_HEADS):
            vt_ref[0, g, h * V_ROWS:h * V_ROWS + HEAD_DIM, :] = (
                tr[o_v + h * HEAD_DIM:o_v + (h + 1) * HEAD_DIM].astype(BF16))
            vt_ref[0, g, h * V_ROWS + HEAD_DIM:(h + 1) * V_ROWS, :] = ones_pad
        wt_out_ref[0, g] = tr[o_v + ATT_WIDTH:o_v + ATT_WIDTH + IDX_HEADS] * w_scale

        gate_b = nat[:, o_c:o_c + CONV_WIDTH]
        u = nat[:, o_c + CONV_WIDTH:o_c + 2 * CONV_WIDTH] * nat[:, o_c + 2 * CONV_WIDTH:o_c + 3 * CONV_WIDTH]
        base = SUBLANES + g * tm
        u_ref[base:base + tm, :] = u
        um1 = u_ref[base - 1:base - 1 + tm, :]
        um2 = u_ref[base - 2:base - 2 + tm, :]
        conv_out = gate_b * (cw[0:1, :] * um2 + cw[1:2, :] * um1 + cw[2:3, :] * u)
        ms = jnp.mean(conv_out * conv_out, axis=-1, keepdims=True)
        cm_ref[0, rows, :] = (conv_out * lax.rsqrt(ms + NORM_EPS) * gconv_ref[...]).astype(BF16)

    tail = SUBLANES + PROJ_GROUPS * tm
    u_ref[0:SUBLANES, :] = u_ref[tail - SUBLANES:tail, :]


def _proj_call(x, wn, wt, cos_n, sin_n, cos_t, sin_t, conv_w, g_conv):
    bsz, seq, _ = x.shape
    tm = SEQ_TILE
    nt = seq // tm
    gm = PROJ_GROUPS * tm
    n_nat = wn.shape[1]
    n_tr = wt.shape[0]
    w_scale = (IDX_DIM ** -0.5) * (IDX_HEADS ** -0.5)
    const2 = lambda b, t: (0, 0)
    out_shapes = (
        jax.ShapeDtypeStruct((bsz, seq, ATT_WIDTH), BF16),
        jax.ShapeDtypeStruct((bsz, seq, IDX_DIM), BF16),
        jax.ShapeDtypeStruct((bsz, nt, ATT_WIDTH, tm), BF16),
        jax.ShapeDtypeStruct((bsz, nt, IDX_HEADS * IDX_DIM, tm), BF16),
        jax.ShapeDtypeStruct((bsz, nt, ATT_HEADS * V_ROWS, tm), BF16),
        jax.ShapeDtypeStruct((bsz, nt, IDX_HEADS, tm), F32),
        jax.ShapeDtypeStruct((bsz, seq, CONV_WIDTH), BF16),
    )
    tile4 = lambda b, t: (b, t, 0, 0)
    return pl.pallas_call(
        functools.partial(_proj_kernel, w_scale=w_scale),
        grid=(bsz, nt // PROJ_GROUPS),
        in_specs=[
            pl.BlockSpec((1, gm, D_MODEL), lambda b, t: (b, t, 0)),
            pl.BlockSpec((D_MODEL, n_nat), const2),
            pl.BlockSpec((n_tr, D_MODEL), const2),
            pl.BlockSpec((gm, LANES), lambda b, t: (t, 0)),
            pl.BlockSpec((gm, LANES), lambda b, t: (t, 0)),
            pl.BlockSpec((HEAD_DIM // 2, gm), lambda b, t: (0, t)),
            pl.BlockSpec((HEAD_DIM // 2, gm), lambda b, t: (0, t)),
            pl.BlockSpec((CONV_K, CONV_WIDTH), const2),
            pl.BlockSpec((1, CONV_WIDTH), const2),
        ],
        out_specs=(
            pl.BlockSpec((1, gm, ATT_WIDTH), lambda b, t: (b, t, 0)),
            pl.BlockSpec((1, gm, IDX_DIM), lambda b, t: (b, t, 0)),
            pl.BlockSpec((1, PROJ_GROUPS, ATT_WIDTH, tm), tile4),
            pl.BlockSpec((1, PROJ_GROUPS, IDX_HEADS * IDX_DIM, tm), tile4),
            pl.BlockSpec((1, PROJ_GROUPS, ATT_HEADS * V_ROWS, tm), tile4),
            pl.BlockSpec((1, PROJ_GROUPS, IDX_HEADS, tm), tile4),
            pl.BlockSpec((1, gm, CONV_WIDTH), lambda b, t: (b, t, 0)),
        ),
        out_shape=out_shapes,
        scratch_shapes=[pltpu.VMEM((gm + SUBLANES, CONV_WIDTH), F32)],
        compiler_params=pltpu.CompilerParams(
            dimension_semantics=("arbitrary", "arbitrary"), vmem_limit_bytes=VMEM_LIMIT),
        name="proj_rope_conv",
    )(x, wn, wt, cos_n, sin_n, cos_t, sin_t, conv_w, g_conv)


def _key_to_float(key):
    key = jnp.clip(key, KEY_NEG_INF, KEY_POS_INF)
    bits = jnp.where(key >= 0, key, key ^ 0x7FFFFFFF)
    return lax.bitcast_convert_type(bits, F32)


def _after(value, anchor):
    return jnp.maximum(value, jnp.minimum(anchor, -jnp.inf))


def _sorting_network(n):
    pairs = []

    def merge(lo, size, r):
        step = 2 * r
        if step < size:
            merge(lo, size, step)
            merge(lo + r, size, step)
            pairs.extend((i, i + r) for i in range(lo + r, lo + size - r, step))
        else:
            pairs.append((lo, lo + r))

    def sort(lo, size):
        if size > 1:
            sort(lo, size // 2)
            sort(lo + size // 2, size // 2)
            merge(lo, size, 1)

    sort(0, n)
    return pairs


def _sorted_group_count(grp, pred):
    size = len(grp)
    levels = size.bit_length() - 1
    masks = []

    def pick(cands, decisions):
        if len(cands) == 1:
            return cands[0]
        mid = len(cands) // 2
        return jnp.where(decisions[0], pick(cands[mid:], decisions[1:]), pick(cands[:mid], decisions[1:]))

    for level in range(1, levels + 1):
        step = size >> level
        cands = [grp[(2 * t + 1) * step - 1] for t in range(1 << (level - 1))]
        masks.append(pred(pick(cands, masks)))
    every = pred(grp[size - 1])
    count = jnp.where(every, 2, jnp.where(masks[levels - 1], 1, 0))
    for level in range(1, levels):
        count = count + jnp.where(masks[level - 1], size >> level, 0)
    return count


def _fold_rows(a):
    return a.reshape(a.shape[0] // SUBLANES, SUBLANES, a.shape[1])


def _dsa_kernel(kn_ref, kin_ref, vt_ref, qt_ref, qit_ref, w_ref, g_ref, out_ref,
                score_ref, sorted_ref, wbd_ref, acc_ref, at_ref, lg_ref, *, topk, seq):
    i = pl.program_id(1)
    tq = qt_ref.shape[3]
    tk = tq
    nchunks = i + 1
    neg_inf = jnp.float32(-jnp.inf)

    row_iota3 = (lax.broadcasted_iota(jnp.int32, (tk // SUBLANES, SUBLANES, tq), 0) * SUBLANES
                 + lax.broadcasted_iota(jnp.int32, (tk // SUBLANES, SUBLANES, tq), 1))
    qpos3 = i * tq + lax.broadcasted_iota(jnp.int32, (tk // SUBLANES, SUBLANES, tq), 2)
    qpos_row = i * tq + lax.broadcasted_iota(jnp.int32, (SUBLANES, tq), 1)

    def chunk_off(c):
        return pl.multiple_of(c * tk, tk)

    def over_chunks(widths, body, carry):
        start = 0
        for width in widths:
            trips = (nchunks - start) // width
            carry = lax.fori_loop(
                0, trips, lambda it, cr, s=start, w=width: body([s + w * it + k for k in range(w)], cr), carry)
            start = start + trips * width
        return carry

    qi_all = qit_ref[0, 0]
    w_rows = w_ref[0, 0]

    def score_body(c, carry):
        off = chunk_off(c)
        kic = kin_ref[0, pl.ds(off, tk), :]
        sc = jnp.zeros((tk, tq), F32)
        for h in range(IDX_HEADS):
            d = jnp.dot(kic, qi_all[h * IDX_DIM:(h + 1) * IDX_DIM, :], preferred_element_type=F32)
            sc = sc + jnp.maximum(d, 0.0) * w_rows[h:h + 1, :]
        sc = jnp.where(row_iota3 <= (qpos_row - off)[None], _fold_rows(sc), neg_inf).reshape(tk, tq)
        score_ref[pl.ds(off, tk), :] = sc
        rows = _fold_rows(sc)
        sorted_rows = []
        for g in range(tk // SUBLANES // SORT_GROUP):
            grp = [rows[SORT_GROUP * g + r] for r in range(SORT_GROUP)]
            for a, b in _sorting_network(SORT_GROUP):
                grp[a], grp[b] = jnp.maximum(grp[a], grp[b]), jnp.minimum(grp[a], grp[b])
            sorted_rows.extend(grp)
        sorted_ref[pl.ds(off, tk), :] = jnp.stack(sorted_rows).reshape(tk, tq)
        return carry

    over_chunks(SCORE_UNROLLS, lambda chunks, carry: functools.reduce(lambda cr, c: score_body(c, cr), chunks, carry), 0)

    def count_total(chunk_partials, n_parts):
        def count_chunks(chunks, accs):
            for c in chunks:
                accs = tuple(acc + part for acc, part in zip(accs, chunk_partials(c)))
            return accs

        accs = over_chunks(COUNT_UNROLLS, count_chunks, (jnp.zeros((SUBLANES, tq), jnp.int32),) * n_parts)
        acc = functools.reduce(lambda x, y: x + y, accs)
        for shift in (4, 2, 1):
            acc = acc + pltpu.roll(acc, shift, axis=0)
        return acc

    def count_rows(pred_fn):
        def partials(c):
            off = chunk_off(c)
            m = pred_fn(_fold_rows(score_ref[pl.ds(off, tk), :]), off).astype(jnp.int32)
            per = m.shape[0] // COUNT_ACCS
            return [jnp.sum(m[g * per:(g + 1) * per], axis=0) for g in range(COUNT_ACCS)]
        return count_total(partials, COUNT_ACCS)

    blocks_per_chunk = tk // SUBLANES // SORT_GROUP

    def count_sorted(pred):
        def partials(c):
            rows = _fold_rows(sorted_ref[pl.ds(chunk_off(c), tk), :])
            return [_sorted_group_count([rows[SORT_GROUP * g + r] for r in range(SORT_GROUP)], pred)
                    for g in range(blocks_per_chunk)]
        return count_total(partials, blocks_per_chunk)

    def search_body(p, carry):
        u, n_u = carry
        trial = u | jnp.left_shift(jnp.int32(1), 31 - p)
        cand = _key_to_float(trial ^ INT_MIN)
        cnt = count_sorted(lambda sc: sc >= cand)
        take = cnt >= topk
        return jnp.where(take, trial, u), jnp.where(take, cnt, n_u)

    search_init = (jnp.zeros((SUBLANES, tq), jnp.int32), jnp.zeros((SUBLANES, tq), jnp.int32) + nchunks * tk)
    u, n_ge = lax.cond((i + 1) * tq > topk, lambda: lax.fori_loop(0, 32, search_body, search_init),
                       lambda: search_init)
    thr = _key_to_float(u ^ INT_MIN)

    needs_rewrite = (jnp.max(n_ge) > topk) | (i * tq < topk)

    def resolve_ties():
        want = topk - count_sorted(lambda sc: sc > thr)

        def body(p, jp):
            trial = jp + jnp.left_shift(jnp.int32(1), (seq - 1).bit_length() - 1 - p)
            cnt = count_rows(lambda sc, off: (sc == thr[None]) & (off + row_iota3 <= trial[None]))
            return jnp.where(cnt < want, trial, jp)
        jlim = lax.fori_loop(0, (seq - 1).bit_length(), body, jnp.full((SUBLANES, tq), -1, jnp.int32)) + 1

        def rewrite(c, carry):
            off = chunk_off(c)
            sc = _fold_rows(score_ref[pl.ds(off, tk), :])
            kpos = off + row_iota3
            sel = ((sc > thr[None]) | ((sc == thr[None]) & (kpos <= jlim[None]))) & (kpos <= qpos3)
            score_ref[pl.ds(off, tk), :] = jnp.where(sel, 1.0, -1.0).reshape(tk, tq)
            return carry

        lax.fori_loop(0, nchunks, rewrite, 0)
        return jnp.zeros((SUBLANES, tq), F32)

    thr_sel = lax.cond(needs_rewrite, resolve_ties, lambda: thr)

    q_all = qt_ref[0, 0]
    half = HEAD_DIM // 2
    @pl.when(i == 0)
    def _():
        wbd_ref[...] = jnp.zeros(wbd_ref.shape, BF16)

    for h in range(ATT_HEADS):
        for part in range(2):
            r0 = h * HEAD_DIM + part * half
            d0 = (part * HEAD_GROUP + h % HEAD_GROUP) * half
            wbd_ref[h, d0:d0 + half, :] = q_all[r0:r0 + half]
    acc_ref[...] = jnp.zeros(acc_ref.shape, F32)

    def attn_chunks(chunks, m_all):
        rows = len(chunks) * tk
        off = chunk_off(chunks[0])
        sel = _fold_rows(score_ref[pl.ds(off, rows), :]) >= thr_sel[None]
        bias = jnp.where(sel, 0.0, neg_inf).reshape(rows, tq)

        def head_logits(h):
            c0 = (h // HEAD_GROUP) * HEAD_GROUP * HEAD_DIM
            kc = kn_ref[0, pl.ds(off, rows), c0:c0 + HEAD_GROUP * HEAD_DIM]
            return jnp.dot(kc, wbd_ref[h], preferred_element_type=F32)

        m_run = [m_all[:, h * tq:(h + 1) * tq] for h in range(ATT_HEADS)]
        lgs_raw = [head_logits(0)]
        stage_a = []
        for h in range(ATT_HEADS):
            lg = lgs_raw[h] + bias
            m_old = m_run[h]
            if h + 1 < ATT_HEADS:
                lgs_raw.append(head_logits(h + 1))
                m_old = _after(m_old, lgs_raw[h + 1][0:1, :])
            m_new = jnp.maximum(m_old, jnp.max(jnp.max(_fold_rows(lg), axis=0), axis=0, keepdims=True))
            m_run[h] = m_new
            lg_ref[h, 0:rows, :] = lg
            stage_a.append((m_old, m_new))
        for h in range(ATT_HEADS):
            m_old, m_new = stage_a[h]
            m_use = jnp.where(m_new == neg_inf, 0.0, m_new)
            if h + 1 < ATT_HEADS:
                m_use = _after(m_use, stage_a[h + 1][1])
            alpha = jnp.exp2(m_old - m_use)
            p = jnp.exp2(lg_ref[h, 0:rows, :] - m_use).astype(BF16)
            v_t = jnp.concatenate([vt_ref[0, c, h * V_ROWS:(h + 1) * V_ROWS, :] for c in chunks], axis=1)
            acc_ref[h] = acc_ref[h] * alpha + jnp.dot(v_t, p, preferred_element_type=F32)
        return jnp.concatenate(m_run, axis=1)

    over_chunks(ATTN_UNROLLS, attn_chunks, jnp.full((1, ATT_HEADS * tq), neg_inf, F32))
    for h in range(ATT_HEADS):
        o = acc_ref[h]
        at_ref[h * HEAD_DIM:(h + 1) * HEAD_DIM, :] = o[:HEAD_DIM] / o[HEAD_DIM:HEAD_DIM + 1]

    a_t = at_ref[...]
    ms = jnp.mean(a_t * a_t, axis=0, keepdims=True)
    y = (a_t * lax.rsqrt(ms + NORM_EPS)).T
    out_ref[0] = (y * g_ref[...]).astype(BF16)


def _dsa_call(kn, kin, vt, qt, qit, w_t, g_attn, topk):
    bsz, seq, _ = kn.shape
    nt, tq = qt.shape[1], qt.shape[3]
    tile4 = lambda b, i: (b, i, 0, 0)
    return pl.pallas_call(
        functools.partial(_dsa_kernel, topk=topk, seq=seq),
        grid=(bsz, nt),
        in_specs=[
            pl.BlockSpec((1, seq, ATT_WIDTH), lambda b, i: (b, 0, 0)),
            pl.BlockSpec((1, seq, IDX_DIM), lambda b, i: (b, 0, 0)),
            pl.BlockSpec((1, nt, ATT_HEADS * V_ROWS, tq), lambda b, i: (b, 0, 0, 0)),
            pl.BlockSpec((1, 1, ATT_WIDTH, tq), tile4),
            pl.BlockSpec((1, 1, IDX_HEADS * IDX_DIM, tq), tile4),
            pl.BlockSpec((1, 1, IDX_HEADS, tq), tile4),
            pl.BlockSpec((1, ATT_WIDTH), lambda b, i: (0, 0)),
        ],
        out_specs=pl.BlockSpec((1, tq, ATT_WIDTH), lambda b, i: (b, i, 0)),
        out_shape=jax.ShapeDtypeStruct((bsz, seq, ATT_WIDTH), BF16),
        scratch_shapes=[
            pltpu.VMEM((seq, tq), F32),
            pltpu.VMEM((seq, tq), F32),
            pltpu.VMEM((ATT_HEADS, HEAD_GROUP * HEAD_DIM, tq), BF16),
            pltpu.VMEM((ATT_HEADS, V_ROWS, tq), F32),
            pltpu.VMEM((ATT_WIDTH, tq), F32),
            pltpu.VMEM((ATT_HEADS, max(ATTN_UNROLLS) * tq, tq), F32),
        ],
        compiler_params=pltpu.CompilerParams(
            dimension_semantics=("arbitrary", "arbitrary"), vmem_limit_bytes=VMEM_LIMIT),
        name="dsa_attention",
    )(kn, kin, vt, qt, qit, w_t, g_attn)


def _layer_norm(y, g, b):
    mu = jnp.mean(y, axis=-1, keepdims=True)
    d = y - mu
    var = jnp.mean(d * d, axis=-1, keepdims=True)
    return d * lax.rsqrt(var + NORM_EPS) * g + b


def _ffn_kernel(x_ref, am_ref, cm_ref, wout_ref, g1_ref, b1_ref, wgu_ref, wd_ref, g2_ref, b2_ref,
                o_ref, *, alpha, d_ff):
    n = FFN_SPLIT
    group = x_ref.shape[0] // n
    rows = [slice(r * group, (r + 1) * group) for r in range(n)]
    mix = [jnp.dot(am_ref[rw, :], wout_ref[0:ATT_WIDTH, :], preferred_element_type=F32)
           + jnp.dot(cm_ref[rw, :], wout_ref[ATT_WIDTH:D_MODEL, :], preferred_element_type=F32) for rw in rows]
    x1 = []
    for r in range(n):
        b1 = b1_ref[...] if r + 1 == n else _after(b1_ref[...], mix[r + 1][0:1, :])
        x1.append(_layer_norm(alpha * x_ref[rows[r], :] + mix[r], g1_ref[...], b1))
    gu = [jnp.dot(x1[r].astype(BF16), wgu_ref[...], preferred_element_type=F32) for r in range(n)]
    ffn = []
    for r in range(n):
        gate = gu[r][:, :d_ff]
        hidden = (gate * jax.nn.sigmoid(gate) * gu[r][:, d_ff:]).astype(BF16)
        ffn.append(jnp.dot(hidden, wd_ref[...], preferred_element_type=F32))
    for r in range(n):
        b2 = b2_ref[...] if r + 1 == n else _after(b2_ref[...], ffn[r + 1][0:1, :])
        o_ref[rows[r], :] = _layer_norm(alpha * x1[r] + ffn[r], g2_ref[...], b2)


def _ffn_call(x2d, am, cm, wout, g1, b1, wgu, wd, g2, b2, alpha):
    rows = x2d.shape[0]
    tm = ROW_TILE
    d_ff = wd.shape[0]
    row = lambda r: (r, 0)
    const = lambda r: (0, 0)
    resident = functools.partial(pl.BlockSpec, index_map=const, pipeline_mode=pl.Buffered(1))
    return pl.pallas_call(
        functools.partial(_ffn_kernel, alpha=alpha, d_ff=d_ff),
        grid=(rows // tm,),
        in_specs=[
            pl.BlockSpec((tm, D_MODEL), row),
            pl.BlockSpec((tm, ATT_WIDTH), row),
            pl.BlockSpec((tm, CONV_WIDTH), row),
            resident((D_MODEL, D_MODEL)),
            resident((1, D_MODEL)),
            resident((1, D_MODEL)),
            resident((D_MODEL, 2 * d_ff)),
            resident((d_ff, D_MODEL)),
            resident((1, D_MODEL)),
            resident((1, D_MODEL)),
        ],
        out_specs=pl.BlockSpec((tm, D_MODEL), row),
        out_shape=jax.ShapeDtypeStruct((rows, D_MODEL), F32),
        compiler_params=pltpu.CompilerParams(
            dimension_semantics=("arbitrary",), vmem_limit_bytes=VMEM_LIMIT),
        name="outproj_ffn",
    )(x2d, am, cm, wout, g1, b1, wgu, wd, g2, b2)


def _rotate_half_cols(w):
    d_in, n = w.shape
    w4 = w.reshape(d_in, n // HEAD_DIM, 2, HEAD_DIM // 2)
    return jnp.concatenate([-w4[:, :, 1:2, :], w4[:, :, 0:1, :]], axis=2).reshape(d_in, n)


def _group_halves_cols(w):
    d_in, n = w.shape
    w5 = w.reshape(d_in, n // HEAD_DIM // HEAD_GROUP, HEAD_GROUP, 2, HEAD_DIM // 2)
    return jnp.swapaxes(w5, 2, 3).reshape(d_in, n)


def _split_cols(w, sizes):
    out, o = [], 0
    for s in sizes:
        out.append(w[:, o:o + s])
        o += s
    return out


def kernel(x, w_in, conv_w, mix_norm_g, w_out, ln1_g, ln1_b, w_gate_up, w_down, ln2_g, ln2_b):
    bsz, seq, _ = x.shape
    depth = w_in.shape[0]
    alpha = (2 * depth) ** 0.25
    topk = min(INDEX_TOPK_MAX, seq // 4)
    assert seq % (SEQ_TILE * PROJ_GROUPS) == 0 and (bsz * seq) % ROW_TILE == 0 and topk <= SEQ_TILE

    inv_freq = 1.0 / (ROPE_THETA ** (np.arange(0, HEAD_DIM, 2, dtype=np.float64) / HEAD_DIM))
    ang = np.arange(seq, dtype=np.float64)[:, None] * inv_freq[None, :]
    cos, sin = np.cos(ang).astype(np.float32), np.sin(ang).astype(np.float32)
    cos_n, sin_n = np.tile(cos, (1, LANES // cos.shape[1])), np.tile(sin, (1, LANES // sin.shape[1]))
    cos_t, sin_t = np.ascontiguousarray(cos.T), np.ascontiguousarray(sin.T)

    for layer in range(depth):
        wq, wk, wv, wqi, wki, wwi, wgb, wgc, wh = _split_cols(w_in[layer].astype(BF16), IN_SPLIT_SIZES)
        pad_ki = jnp.zeros((D_MODEL, LANES - IDX_DIM), BF16)
        wn = jnp.concatenate([_group_halves_cols(wk), wgb, wgc, wh,
                              wki, pad_ki, _rotate_half_cols(wki), pad_ki], axis=1)
        pad_wi = jnp.zeros((D_MODEL, 2 * SUBLANES - IDX_HEADS), BF16)
        wt = jnp.concatenate([wq, wqi, wv, wwi, pad_wi], axis=1).T
        g = mix_norm_g[layer]
        kn, kin, qt, qit, vt, w_t, cm = _proj_call(
            x, wn, wt, cos_n, sin_n, cos_t, sin_t, conv_w[layer], g[ATT_WIDTH:][None, :])
        am = _dsa_call(kn, kin, vt, qt, qit, w_t, g[:ATT_WIDTH][None, :], topk)
        y = _ffn_call(
            x.reshape(bsz * seq, D_MODEL), am.reshape(bsz * seq, ATT_WIDTH), cm.reshape(bsz * seq, CONV_WIDTH),
            w_out[layer].astype(BF16), ln1_g[layer][None, :], ln1_b[layer][None, :],
            w_gate_up[layer].astype(BF16), w_down[layer].astype(BF16),
            ln2_g[layer][None, :], ln2_b[layer][None, :], alpha)
        x = y.reshape(bsz, seq, D_MODEL)
    return x
```
